```python
import jax, jax.numpy as jnp
from jax import lax
import numpy as np

D_MODEL = 1024
BATCH = 16
SEQ = 2048
DEPTH = 1

N_META = 16
POOL_WIDTH = 512
POOL_WINDOWS = (2, 4, 8, 16)
POOL_GROUP = POOL_WIDTH // len(POOL_WINDOWS)
HEAD_DIM = 64
N_HEADS = 8
N_KV_HEADS = 2
Q_PER_KV = N_HEADS // N_KV_HEADS
ATTN_WIDTH = N_HEADS * HEAD_DIM
KV_WIDTH = N_KV_HEADS * HEAD_DIM
WINDOW = 128
BLOCK = 128
ROT_DIM = HEAD_DIM // 4
ROPE_THETA = 500000.0
MIX_WIDTH = POOL_WIDTH + ATTN_WIDTH
IN_WIDTH = POOL_WIDTH + ATTN_WIDTH + 2 * KV_WIDTH
N_GROUPS = 4
EXPERTS_PER_GROUP = 4
N_EXPERTS = N_GROUPS * EXPERTS_PER_GROUP
TOP_K = 2
D_EXPERT = 256
EPS = 1e-6
NEG_INF = -1e30

kernel_name = "hymba_pool_swa_sink_hmoe"


def rms_norm(x, gain):
    xf = x.astype(jnp.float32)
    y = xf * lax.rsqrt(jnp.mean(xf * xf, axis=-1, keepdims=True) + EPS)
    return (y * gain.astype(jnp.float32)).astype(x.dtype)


def partial_rope(t, pos):
    half = ROT_DIM // 2
    inv_freq = 1.0 / (ROPE_THETA ** (jnp.arange(half, dtype=jnp.float32) / half))
    ang = pos.astype(jnp.float32)[:, None] * inv_freq[None, :]
    cos = jnp.cos(ang)[:, None, :]
    sin = jnp.sin(ang)[:, None, :]
    tf = t.astype(jnp.float32)
    x1 = tf[..., :half]
    x2 = tf[..., half:ROT_DIM]
    out = jnp.concatenate([x1 * cos - x2 * sin, x2 * cos + x1 * sin, tf[..., ROT_DIM:]], axis=-1)
    return out.astype(t.dtype)


def multiscale_pool(u, w_pool, pool_scale):
    L = u.shape[1]
    pos = jnp.arange(L)
    outs = []
    for g, w in enumerate(POOL_WINDOWS):
        ug = u[..., g * POOL_GROUP:(g + 1) * POOL_GROUP].astype(jnp.float32)
        c = jnp.cumsum(ug, axis=1)
        c_prev = jnp.pad(c, ((0, 0), (w, 0), (0, 0)))[:, :L]
        count = jnp.minimum(pos + 1, w).astype(jnp.float32)[None, :, None]
        mixed = ((c - c_prev) / count - ug).astype(u.dtype)
        outs.append(jnp.einsum('blc,cd->bld', mixed, w_pool[g]))
    return jnp.concatenate(outs, axis=-1) * pool_scale


def sliding_window_attention(q, k, v, sinks):
    B, L = q.shape[0], q.shape[1]
    nb = -(-L // BLOCK)
    Lp = nb * BLOCK
    pad = ((0, 0), (0, Lp - L), (0, 0), (0, 0))
    qb = jnp.pad(q, pad).reshape(B, nb, BLOCK, N_KV_HEADS, Q_PER_KV, HEAD_DIM)
    kb = jnp.pad(k, pad).reshape(B, nb, BLOCK, N_KV_HEADS, HEAD_DIM)
    vb = jnp.pad(v, pad).reshape(B, nb, BLOCK, N_KV_HEADS, HEAD_DIM)
    shift = ((0, 0), (1, 0), (0, 0), (0, 0), (0, 0))
    k_band = jnp.concatenate([jnp.pad(kb, shift)[:, :nb], kb], axis=2)
    v_band = jnp.concatenate([jnp.pad(vb, shift)[:, :nb], vb], axis=2)
    k_meta = k[:, :N_META]
    v_meta = v[:, :N_META]
    scale = HEAD_DIM ** -0.5
    s_band = jnp.einsum('bnqkgd,bnjkd->bnkgqj', qb, k_band).astype(jnp.float32) * scale
    s_meta = jnp.einsum('bnqkgd,bmkd->bnkgqm', qb, k_meta).astype(jnp.float32) * scale
    q_pos = jnp.arange(nb)[:, None] * BLOCK + jnp.arange(BLOCK)[None, :]
    k_pos = jnp.arange(nb)[:, None] * BLOCK - BLOCK + jnp.arange(2 * BLOCK)[None, :]
    diff = q_pos[:, :, None] - k_pos[:, None, :]
    band_ok = (diff >= 0) & (diff < WINDOW) & (k_pos[:, None, :] >= 0)
    meta_ok = (q_pos[:, :, None] - jnp.arange(N_META)[None, None, :]) >= WINDOW
    s_band = jnp.where(band_ok[None, :, None, None], s_band, NEG_INF)
    s_meta = jnp.where(meta_ok[None, :, None, None], s_meta, NEG_INF)
    sink = jnp.broadcast_to(
        sinks.astype(jnp.float32).reshape(N_KV_HEADS, Q_PER_KV)[None, None, :, :, None, None],
        s_band.shape[:-1] + (1,))
    p = jax.nn.softmax(jnp.concatenate([s_band, s_meta, sink], axis=-1), axis=-1)
    p_band = p[..., :2 * BLOCK].astype(v.dtype)
    p_meta = p[..., 2 * BLOCK:2 * BLOCK + N_META].astype(v.dtype)
    o = (jnp.einsum('bnkgqj,bnjkd->bnqkgd', p_band, v_band)
         + jnp.einsum('bnkgqm,bmkd->bnqkgd', p_meta, v_meta))
    return o.reshape(B, Lp, ATTN_WIDTH)[:, :L]


def hierarchical_moe(x, w_group_router, w_expert_router, w_gate, w_up, w_down):
    B, L, D = x.shape
    t = x.reshape(B * L, D)
    group_probs = jax.nn.softmax((t @ w_group_router).astype(jnp.float32), axis=-1)
    g_prob, g_idx = lax.top_k(group_probs, 1)
    expert_logits = (t @ w_expert_router).astype(jnp.float32).reshape(-1, N_GROUPS, EXPERTS_PER_GROUP)
    g_onehot = jax.nn.one_hot(g_idx[:, 0], N_GROUPS, dtype=jnp.float32)
    in_group = jnp.einsum('tge,tg->te', expert_logits, g_onehot)
    e_logit, e_idx = lax.top_k(in_group, TOP_K)
    e_w = jax.nn.softmax(e_logit, axis=-1) * g_prob
    expert_id = g_idx * EXPERTS_PER_GROUP + e_idx
    gates = jnp.sum(jax.nn.one_hot(expert_id, N_EXPERTS, dtype=jnp.float32) * e_w[..., None], axis=1)
    hid = jax.nn.silu(jnp.einsum('td,edf->tef', t, w_gate)) * jnp.einsum('td,edf->tef', t, w_up)
    hid = hid * gates.astype(hid.dtype)[..., None]
    y = jnp.einsum('tef,efd->td', hid, w_down)
    return y.reshape(B, L, D)


def setup_inputs(seed: int = 0) -> dict:
    key = jax.random.key(seed)
    ks = jax.random.split(key, 20)
    f32 = jnp.float32
    n = lambda k, shape, s: jax.random.normal(k, shape, f32) * s
    return {
        "x": n(ks[0], (BATCH, SEQ, D_MODEL), 1.0),
        "meta_tokens": n(ks[1], (N_META, D_MODEL), 1.0),
        "attn_norm_gain": 1.0 + n(ks[2], (DEPTH, D_MODEL), 0.02),
        "w_in": n(ks[3], (DEPTH, D_MODEL, IN_WIDTH), D_MODEL ** -0.5),
        "w_pool": n(ks[4], (DEPTH, len(POOL_WINDOWS), POOL_GROUP, POOL_GROUP), POOL_GROUP ** -0.5),
        "pool_scale": 1.0 + n(ks[5], (DEPTH, POOL_WIDTH), 0.02),
        "q_norm_gain": 1.0 + n(ks[6], (DEPTH, HEAD_DIM), 0.02),
        "k_norm_gain": 1.0 + n(ks[7], (DEPTH, HEAD_DIM), 0.02),
        "attn_sinks": n(ks[8], (DEPTH, N_HEADS), 0.5),
        "w_out": n(ks[9], (DEPTH, MIX_WIDTH, D_MODEL), MIX_WIDTH ** -0.5),
        "ffn_norm_gain": 1.0 + n(ks[10], (DEPTH, D_MODEL), 0.02),
        "w_group_router": n(ks[11], (DEPTH, D_MODEL, N_GROUPS), D_MODEL ** -0.5),
        "w_expert_router": n(ks[12], (DEPTH, D_MODEL, N_EXPERTS), D_MODEL ** -0.5),
        "w_gate": n(ks[13], (DEPTH, N_EXPERTS, D_MODEL, D_EXPERT), D_MODEL ** -0.5),
        "w_up": n(ks[14], (DEPTH, N_EXPERTS, D_MODEL, D_EXPERT), D_MODEL ** -0.5),
        "w_down": n(ks[15], (DEPTH, N_EXPERTS, D_EXPERT, D_MODEL), D_EXPERT ** -0.5),
    }


def reference(x, meta_tokens, attn_norm_gain, w_in, w_pool, pool_scale, q_norm_gain, k_norm_gain,
              attn_sinks, w_out, ffn_norm_gain, w_group_router, w_expert_router, w_gate, w_up, w_down):
    B, S, D = x.shape
    meta = jnp.broadcast_to(meta_tokens[None].astype(x.dtype), (B, N_META, D))
    h = jnp.concatenate([meta, x], axis=1)
    L = S + N_META
    pos = jnp.arange(L)
    splits = [POOL_WIDTH, POOL_WIDTH + ATTN_WIDTH, POOL_WIDTH + ATTN_WIDTH + KV_WIDTH]
    for layer in range(DEPTH):
        a = rms_norm(h, attn_norm_gain[layer])
        proj = a @ w_in[layer]
        u, q, k, v = jnp.split(proj, splits, axis=-1)
        y_pool = multiscale_pool(u, w_pool[layer], pool_scale[layer])
        q = partial_rope(rms_norm(q.reshape(B, L, N_HEADS, HEAD_DIM), q_norm_gain[layer]), pos)
        k = partial_rope(rms_norm(k.reshape(B, L, N_KV_HEADS, HEAD_DIM), k_norm_gain[layer]), pos)
        v = v.reshape(B, L, N_KV_HEADS, HEAD_DIM)
        y_attn = sliding_window_attention(q, k, v, attn_sinks[layer])
        h = h + jnp.concatenate([y_pool, y_attn], axis=-1) @ w_out[layer]
        m = rms_norm(h, ffn_norm_gain[layer])
        h = h + hierarchical_moe(m, w_group_router[layer], w_expert_router[layer],
                                 w_gate[layer], w_up[layer], w_down[layer])
    return h[:, N_META:]
```

```python
import functools

import numpy as np
import jax
import jax.numpy as jnp
from jax import lax
from jax.experimental import pallas as pl
from jax.experimental.pallas import tpu as pltpu

D_MODEL = 1024
N_META = 16
POOL_WIDTH = 512
POOL_WINDOWS = (2, 4, 8, 16)
POOL_GROUP = 128
HEAD_DIM = 64
N_HEADS = 8
N_KV_HEADS = 2
ATTN_WIDTH = N_HEADS * HEAD_DIM
KV_WIDTH = N_KV_HEADS * HEAD_DIM
WINDOW = 128
ROT_DIM = HEAD_DIM // 4
ROPE_THETA = 500000.0
IN_WIDTH = POOL_WIDTH + ATTN_WIDTH + 2 * KV_WIDTH
N_GROUPS = 4
EXPERTS_PER_GROUP = 4
N_EXPERTS = 16
D_EXPERT = 256
EPS = 1e-6
NEG_INF = -1e30

LANES = 128
KV_EXP = 4 * LANES
TQ_IN = 512
TQ_ATT = 256
TM_MOE = 256
VMEM_LIMIT = 56 * 1024 * 1024

BF16 = jnp.bfloat16
F32 = jnp.float32


def _rope_tables(n_pos):
    half = ROT_DIM // 2
    inv_freq = 1.0 / (ROPE_THETA ** (np.arange(half, dtype=np.float64) / half))
    ang = np.arange(n_pos, dtype=np.float64)[:, None] * inv_freq[None, :]
    cos, sin = np.cos(ang), np.sin(ang)
    c = np.ones((n_pos, HEAD_DIM)); c[:, :half] = cos; c[:, half:ROT_DIM] = cos
    sa = np.zeros((n_pos, HEAD_DIM)); sa[:, :half] = -sin
    sb = np.zeros((n_pos, HEAD_DIM)); sb[:, half:ROT_DIM] = sin
    tab = np.concatenate([np.tile(c, (1, 2)), np.tile(sa, (1, 2)), np.tile(sb, (1, 2))], axis=1)
    return jnp.asarray(tab, dtype=F32)


def _rms_rows(x, gain):
    ms = jnp.mean(x * x, axis=-1, keepdims=True)
    return x * lax.rsqrt(ms + EPS) * gain


def _head_norm_rope(xc, gain2, rope):
    lo = lax.broadcasted_iota(jnp.int32, xc.shape, 1) < HEAD_DIM
    sq = xc * xc
    s_lo = jnp.sum(jnp.where(lo, sq, 0.0), axis=-1, keepdims=True)
    s_hi = jnp.sum(jnp.where(lo, 0.0, sq), axis=-1, keepdims=True)
    ms = jnp.where(lo, s_lo, s_hi) * (1.0 / HEAD_DIM)
    y = xc * lax.rsqrt(ms + EPS) * gain2
    c, sa, sb = rope[:, 0:LANES], rope[:, LANES:2 * LANES], rope[:, 2 * LANES:3 * LANES]
    half = ROT_DIM // 2
    return y * c + pltpu.roll(y, LANES - half, 1) * sa + pltpu.roll(y, half, 1) * sb


def _expand_kv(t):
    lo = lax.broadcasted_iota(jnp.int32, t.shape, 1) < HEAD_DIM
    sw = pltpu.roll(t, HEAD_DIM, 1)
    z = jnp.zeros_like(t)
    return jnp.concatenate([jnp.where(lo, t, z), jnp.where(lo, z, sw),
                            jnp.where(lo, sw, z), jnp.where(lo, z, t)], axis=1)


def _project(x, gain, w_in):
    a = _rms_rows(x, gain).astype(BF16)
    return jnp.dot(a, w_in, preferred_element_type=F32)


def _meta_kernel(meta_ref, gain_ref, win_ref, kgain_ref, rope_ref, u_ref, kd_ref, vd_ref):
    proj = _project(meta_ref[...], gain_ref[...], win_ref[...])
    u_ref[...] = proj[:, :POOL_WIDTH]
    k = proj[:, POOL_WIDTH + ATTN_WIDTH:POOL_WIDTH + ATTN_WIDTH + KV_WIDTH]
    v = proj[:, POOL_WIDTH + ATTN_WIDTH + KV_WIDTH:]
    k = _head_norm_rope(k, kgain_ref[...], rope_ref[...])
    kd_ref[...] = _expand_kv(k).astype(BF16)
    vd_ref[...] = _expand_kv(v).astype(BF16)


def _in_kernel(x_ref, umeta_ref, gain_ref, win_ref, wpool_ref, pscale_ref, qgain_ref, kgain_ref,
               rope_ref, yp_ref, q_ref, kd_ref, vd_ref, carry_ref):
    j = pl.program_id(1)

    @pl.when(j == 0)
    def _():
        carry_ref[...] = umeta_ref[...]

    proj = _project(x_ref[0], gain_ref[...], win_ref[...])
    rope = rope_ref[...]

    u = proj[:, :POOL_WIDTH]
    ext = jnp.concatenate([carry_ref[...], u], axis=0)
    acc = ext
    for gi, w in enumerate(POOL_WINDOWS):
        lo = gi * POOL_GROUP
        acc = acc[:, POOL_GROUP * (1 if gi else 0):]
        acc = acc + pltpu.roll(acc, w // 2, 0)
        mixed = acc[N_META:, :POOL_GROUP] * (1.0 / w) - u[:, lo:lo + POOL_GROUP]
        y = jnp.dot(mixed.astype(BF16), wpool_ref[gi], preferred_element_type=F32)
        yp_ref[0, :, lo:lo + POOL_GROUP] = (y * pscale_ref[:, lo:lo + POOL_GROUP]).astype(BF16)
    carry_ref[...] = u[TQ_IN - N_META:, :]

    scale = HEAD_DIM ** -0.5
    for c in range(ATTN_WIDTH // LANES):
        lo = POOL_WIDTH + c * LANES
        qc = _head_norm_rope(proj[:, lo:lo + LANES], qgain_ref[...], rope)
        q_ref[0, :, c * LANES:(c + 1) * LANES] = (qc * scale).astype(BF16)
    k = proj[:, POOL_WIDTH + ATTN_WIDTH:POOL_WIDTH + ATTN_WIDTH + KV_WIDTH]
    v = proj[:, POOL_WIDTH + ATTN_WIDTH + KV_WIDTH:]
    k = _head_norm_rope(k, kgain_ref[...], rope)
    kd_ref[0] = _expand_kv(k).astype(BF16)
    vd_ref[0] = _expand_kv(v).astype(BF16)


def _route(logits):
    lane = lax.broadcasted_iota(jnp.int32, logits.shape, 1).astype(F32)
    big = float(LANES)
    gmask = (lane >= N_EXPERTS) & (lane < N_EXPERTS + N_GROUPS)
    gl = jnp.where(gmask, logits, NEG_INF)
    gmax = jnp.max(gl, axis=-1, keepdims=True)
    gsum = jnp.sum(jnp.where(gmask, jnp.exp(gl - gmax), 0.0), axis=-1, keepdims=True)
    g_prob = 1.0 / gsum
    g_idx = jnp.min(jnp.where(gl == gmax, lane, big), axis=-1, keepdims=True) - N_EXPERTS
    e_lo = g_idx * EXPERTS_PER_GROUP
    emask = (lane >= e_lo) & (lane < e_lo + EXPERTS_PER_GROUP)
    el = jnp.where(emask, logits, NEG_INF)
    e1 = jnp.max(el, axis=-1, keepdims=True)
    i1 = jnp.min(jnp.where(el == e1, lane, big), axis=-1, keepdims=True)
    el2 = jnp.where(lane == i1, NEG_INF, el)
    e2 = jnp.max(el2, axis=-1, keepdims=True)
    i2 = jnp.min(jnp.where(el2 == e2, lane, big), axis=-1, keepdims=True)
    t = jnp.exp(e2 - e1)
    w1 = 1.0 / (1.0 + t)
    w2 = t * w1
    return jnp.where(lane == i1, w1 * g_prob, 0.0) + jnp.where(lane == i2, w2 * g_prob, 0.0)


def _attn_kernel(sink_ref, x_ref, yp_ref, q_ref, kd_ref, vd_ref, kmeta_ref, vmeta_ref, wout_ref,
                 fgain_ref, wr_ref, h2_ref, m_ref, gates_ref, yattn_ref):
    j = pl.program_id(1)
    nt = (((1,), (1,)), ((), ()))
    for r in range(TQ_ATT // WINDOW):
        q0 = j * TQ_ATT + r * WINDOW
        start = pl.multiple_of(jnp.maximum(q0 - WINDOW, 0), WINDOW)
        qpos = q0 + lax.broadcasted_iota(jnp.int32, (WINDOW, 2 * WINDOW), 0)
        kpos = start + lax.broadcasted_iota(jnp.int32, (WINDOW, 2 * WINDOW), 1)
        d = qpos - kpos
        bias = jnp.where((d >= 0) & (d < WINDOW), 0.0, NEG_INF)
        for c in range(ATTN_WIDTH // LANES):
            g = c // (N_HEADS // N_KV_HEADS // 2)
            qc = q_ref[0, r * WINDOW:(r + 1) * WINDOW, c * LANES:(c + 1) * LANES]
            acc = jnp.zeros((WINDOW, LANES), F32)
            for hh in range(2):
                col = (2 * g + hh) * LANES
                kb = kd_ref[0, pl.ds(start, 2 * WINDOW), col:col + LANES]
                vb = vd_ref[0, pl.ds(start, 2 * WINDOW), col:col + LANES]
                km = kmeta_ref[:, col:col + LANES]
                vm = vmeta_ref[:, col:col + LANES]
                sink = sink_ref[2 * c + hh]
                s = lax.dot_general(qc, kb, nt, preferred_element_type=F32) + bias
                sm = lax.dot_general(qc, km, nt, preferred_element_type=F32)
                mx = jnp.maximum(jnp.max(s, axis=-1, keepdims=True),
                                 jnp.max(sm, axis=-1, keepdims=True))
                mx = jnp.maximum(mx, sink)
                p = jnp.exp(s - mx)
                pm = jnp.exp(sm - mx)
                den = (jnp.sum(p, axis=-1, keepdims=True) + jnp.sum(pm, axis=-1, keepdims=True)
                       + jnp.exp(sink - mx))
                o = (jnp.dot(p.astype(BF16), vb, preferred_element_type=F32)
                     + jnp.dot(pm.astype(BF16), vm, preferred_element_type=F32))
                acc = acc + o * (1.0 / den)
            yattn_ref[r * WINDOW:(r + 1) * WINDOW, c * LANES:(c + 1) * LANES] = acc.astype(BF16)

    mix = (jnp.dot(yp_ref[0], wout_ref[:POOL_WIDTH, :], preferred_element_type=F32)
           + jnp.dot(yattn_ref[...], wout_ref[POOL_WIDTH:, :], preferred_element_type=F32))
    h2 = x_ref[0] + mix
    h2_ref[0] = h2
    m = _rms_rows(h2, fgain_ref[...]).astype(BF16)
    m_ref[0] = m
    logits = jnp.dot(m, wr_ref[...], preferred_element_type=F32)
    gates_ref[0] = _route(logits)


def _moe_kernel(m_ref, gates_ref, h2_ref, wg_ref, wu_ref, wd_ref, out_ref, hid_ref):
    m = m_ref[...]
    for e in range(N_EXPERTS):
        g = jnp.dot(m, wg_ref[e], preferred_element_type=F32)
        u = jnp.dot(m, wu_ref[e], preferred_element_type=F32)
        hid = g * (1.0 / (1.0 + jnp.exp(-g))) * u * gates_ref[:, e:e + 1]
        hid_ref[:, e * D_EXPERT:(e + 1) * D_EXPERT] = hid.astype(BF16)
    out_ref[...] = h2_ref[...] + jnp.dot(hid_ref[...], wd_ref[...], preferred_element_type=F32)


def _const_spec(shape):
    n = len(shape)
    return pl.BlockSpec(shape, lambda *_: (0,) * n)


def _resident_spec(shape):
    n = len(shape)
    return pl.BlockSpec(shape, lambda *_: (0,) * n, pipeline_mode=pl.Buffered(1))


def kernel(x, meta_tokens, attn_norm_gain, w_in, w_pool, pool_scale, q_norm_gain, k_norm_gain,
           attn_sinks, w_out, ffn_norm_gain, w_group_router, w_expert_router, w_gate, w_up, w_down):
    B, S, D = x.shape
    assert D == D_MODEL and S % TQ_IN == 0 and S % TQ_ATT == 0 and (B * S) % TM_MOE == 0
    assert w_in.shape[0] == 1, "single layer"
    T = B * S

    rope = _rope_tables(N_META + S)
    again = attn_norm_gain[0][None, :]
    fgain = ffn_norm_gain[0][None, :]
    qgain2 = jnp.tile(q_norm_gain[0], 2)[None, :]
    kgain2 = jnp.tile(k_norm_gain[0], 2)[None, :]
    pscale = pool_scale[0][None, :]
    w_in_b = w_in[0].astype(BF16)
    w_pool_b = w_pool[0].astype(BF16)
    w_out_b = w_out[0].astype(BF16)
    w_r = jnp.concatenate(
        [w_expert_router[0], w_group_router[0],
         jnp.zeros((D, LANES - N_EXPERTS - N_GROUPS), F32)], axis=1).astype(BF16)
    wg = w_gate[0].astype(BF16)
    wu = w_up[0].astype(BF16)
    wd = w_down[0].astype(BF16).reshape(N_EXPERTS * D_EXPERT, D)
    params = pltpu.CompilerParams(vmem_limit_bytes=VMEM_LIMIT)

    u_meta, kd_meta, vd_meta = pl.pallas_call(
        _meta_kernel,
        out_shape=(jax.ShapeDtypeStruct((N_META, POOL_WIDTH), F32),
                   jax.ShapeDtypeStruct((N_META, KV_EXP), BF16),
                   jax.ShapeDtypeStruct((N_META, KV_EXP), BF16)),
        compiler_params=params,
        name="meta_proj",
    )(meta_tokens, again, w_in_b, kgain2, rope[:N_META])

    yp, q, kd, vd = pl.pallas_call(
        _in_kernel,
        grid=(B, S // TQ_IN),
        in_specs=[
            pl.BlockSpec((1, TQ_IN, D), lambda b, j: (b, j, 0)),
            _const_spec((N_META, POOL_WIDTH)),
            _const_spec((1, D)),
            _const_spec((D, IN_WIDTH)),
            _const_spec((len(POOL_WINDOWS), POOL_GROUP, POOL_GROUP)),
            _const_spec((1, POOL_WIDTH)),
            _const_spec((1, LANES)),
            _const_spec((1, LANES)),
            pl.BlockSpec((TQ_IN, 3 * LANES), lambda b, j: (j, 0)),
        ],
        out_specs=(
            pl.BlockSpec((1, TQ_IN, POOL_WIDTH), lambda b, j: (b, j, 0)),
            pl.BlockSpec((1, TQ_IN, ATTN_WIDTH), lambda b, j: (b, j, 0)),
            pl.BlockSpec((1, TQ_IN, KV_EXP), lambda b, j: (b, j, 0)),
            pl.BlockSpec((1, TQ_IN, KV_EXP), lambda b, j: (b, j, 0)),
        ),
        out_shape=(jax.ShapeDtypeStruct((B, S, POOL_WIDTH), BF16),
                   jax.ShapeDtypeStruct((B, S, ATTN_WIDTH), BF16),
                   jax.ShapeDtypeStruct((B, S, KV_EXP), BF16),
                   jax.ShapeDtypeStruct((B, S, KV_EXP), BF16)),
        scratch_shapes=[pltpu.VMEM((N_META, POOL_WIDTH), F32)],
        compiler_params=pltpu.CompilerParams(
            dimension_semantics=("arbitrary", "arbitrary"), vmem_limit_bytes=VMEM_LIMIT),
        name="in_proj",
    )(x, u_meta, again, w_in_b, w_pool_b, pscale, qgain2, kgain2, rope[N_META:])

    h2, m, gates = pl.pallas_call(
        _attn_kernel,
        grid=(B, S // TQ_ATT),
        in_specs=[
            pl.BlockSpec(memory_space=pltpu.SMEM),
            pl.BlockSpec((1, TQ_ATT, D), lambda b, j: (b, j, 0)),
            pl.BlockSpec((1, TQ_ATT, POOL_WIDTH), lambda b, j: (b, j, 0)),
            pl.BlockSpec((1, TQ_ATT, ATTN_WIDTH), lambda b, j: (b, j, 0)),
            pl.BlockSpec((1, S, KV_EXP), lambda b, j: (b, 0, 0)),
            pl.BlockSpec((1, S, KV_EXP), lambda b, j: (b, 0, 0)),
            _const_spec((N_META, KV_EXP)),
            _const_spec((N_META, KV_EXP)),
            _const_spec((D, D)),
            _const_spec((1, D)),
            _const_spec((D, LANES)),
        ],
        out_specs=(
            pl.BlockSpec((1, TQ_ATT, D), lambda b, j: (b, j, 0)),
            pl.BlockSpec((1, TQ_ATT, D), lambda b, j: (b, j, 0)),
            pl.BlockSpec((1, TQ_ATT, LANES), lambda b, j: (b, j, 0)),
        ),
        out_shape=(jax.ShapeDtypeStruct((B, S, D), F32),
                   jax.ShapeDtypeStruct((B, S, D), BF16),
                   jax.ShapeDtypeStruct((B, S, LANES), F32)),
        scratch_shapes=[pltpu.VMEM((TQ_ATT, ATTN_WIDTH), BF16)],
        compiler_params=pltpu.CompilerParams(
            dimension_semantics=("arbitrary", "arbitrary"), vmem_limit_bytes=VMEM_LIMIT),
        name="attn_out",
    )(attn_sinks[0], x, yp, q, kd, vd, kd_meta, vd_meta, w_out_b, fgain, w_r)

    out = pl.pallas_call(
        _moe_kernel,
        grid=(T // TM_MOE,),
        in_specs=[
            pl.BlockSpec((TM_MOE, D), lambda i: (i, 0)),
            pl.BlockSpec((TM_MOE, LANES), lambda i: (i, 0)),
            pl.BlockSpec((TM_MOE, D), lambda i: (i, 0)),
            _resident_spec((N_EXPERTS, D, D_EXPERT)),
            _resident_spec((N_EXPERTS, D, D_EXPERT)),
            _resident_spec((N_EXPERTS * D_EXPERT, D)),
        ],
        out_specs=pl.BlockSpec((TM_MOE, D), lambda i: (i, 0)),
        out_shape=jax.ShapeDtypeStruct((T, D), F32),
        scratch_shapes=[pltpu.VMEM((TM_MOE, N_EXPERTS * D_EXPERT), BF16)],
        compiler_params=pltpu.CompilerParams(
            dimension_semantics=("arbitrary",), vmem_limit_bytes=VMEM_LIMIT),
        name="moe",
    )(m.reshape(T, D), gates.reshape(T, LANES), h2.reshape(T, D), wg, wu, wd)
    return out.reshape(B, S, D)
```

```python
import functools

import numpy as np
import jax
import jax.numpy as jnp
from jax import lax
from jax.experimental import pallas as pl
from jax.experimental.pallas import tpu as pltpu

D_MODEL = 1024
N_META = 16
POOL_WIDTH = 512
POOL_WINDOWS = (2, 4, 8, 16)
POOL_GROUP = 128
HEAD_DIM = 64
N_HEADS = 8
N_KV_HEADS = 2
ATTN_WIDTH = N_HEADS * HEAD_DIM
KV_WIDTH = N_KV_HEADS * HEAD_DIM
WINDOW = 128
ROT_DIM = HEAD_DIM // 4
ROPE_THETA = 500000.0
IN_WIDTH = POOL_WIDTH + ATTN_WIDTH + 2 * KV_WIDTH
N_GROUPS = 4
EXPERTS_PER_GROUP = 4
N_EXPERTS = 16
D_EXPERT = 256
EPS = 1e-6
NEG_INF = -1e30

LANES = 128
KV_EXP = 4 * LANES
TQ_IN = 512
TQ_ATT = 256
TM_MOE = 512
SUB = 144
BF16_ROWS = 16
MXU_DEPTH = 256
YS_ROWS = -(-((TM_MOE + N_GROUPS * (SUB - 1)) // SUB * SUB) // MXU_DEPTH) * MXU_DEPTH
YS_SHORT = YS_ROWS - MXU_DEPTH
GID_LANE = N_EXPERTS
LO_SHIFT = 32
D_GROUP = EXPERTS_PER_GROUP * D_EXPERT
VMEM_LIMIT = 56 * 1024 * 1024

BF16 = jnp.bfloat16
F32 = jnp.float32


def _rope_tables(n_pos):
    half = ROT_DIM // 2
    inv_freq = 1.0 / (ROPE_THETA ** (np.arange(half, dtype=np.float64) / half))
    ang = np.arange(n_pos, dtype=np.float64)[:, None] * inv_freq[None, :]
    cos, sin = np.cos(ang), np.sin(ang)
    c = np.ones((n_pos, HEAD_DIM)); c[:, :half] = cos; c[:, half:ROT_DIM] = cos
    sa = np.zeros((n_pos, HEAD_DIM)); sa[:, :half] = -sin
    sb = np.zeros((n_pos, HEAD_DIM)); sb[:, half:ROT_DIM] = sin
    tab = np.concatenate([np.tile(c, (1, 2)), np.tile(sa, (1, 2)), np.tile(sb, (1, 2))], axis=1)
    return jnp.asarray(tab, dtype=F32)


def _rms_rows(x, gain):
    ms = jnp.mean(x * x, axis=-1, keepdims=True)
    return x * lax.rsqrt(ms + EPS) * gain


def _head_norm_rope(xc, gain2, rope):
    lo = lax.broadcasted_iota(jnp.int32, xc.shape, 1) < HEAD_DIM
    sq = xc * xc
    s_lo = jnp.sum(jnp.where(lo, sq, 0.0), axis=-1, keepdims=True)
    s_hi = jnp.sum(jnp.where(lo, 0.0, sq), axis=-1, keepdims=True)
    ms = jnp.where(lo, s_lo, s_hi) * (1.0 / HEAD_DIM)
    y = xc * lax.rsqrt(ms + EPS) * gain2
    c, sa, sb = rope[:, 0:LANES], rope[:, LANES:2 * LANES], rope[:, 2 * LANES:3 * LANES]
    half = ROT_DIM // 2
    return y * c + pltpu.roll(y, LANES - half, 1) * sa + pltpu.roll(y, half, 1) * sb


def _expand_kv(t):
    lo = lax.broadcasted_iota(jnp.int32, t.shape, 1) < HEAD_DIM
    sw = pltpu.roll(t, HEAD_DIM, 1)
    z = jnp.zeros_like(t)
    return jnp.concatenate([jnp.where(lo, t, z), jnp.where(lo, z, sw),
                            jnp.where(lo, sw, z), jnp.where(lo, z, t)], axis=1)


def _project(x, gain, w_in):
    a = _rms_rows(x, gain).astype(BF16)
    return jnp.dot(a, w_in, preferred_element_type=F32)


def _meta_kernel(meta_ref, gain_ref, win_ref, kgain_ref, rope_ref, u_ref, kd_ref, vd_ref):
    proj = _project(meta_ref[...], gain_ref[...], win_ref[...])
    u_ref[...] = proj[:, :POOL_WIDTH]
    k = proj[:, POOL_WIDTH + ATTN_WIDTH:POOL_WIDTH + ATTN_WIDTH + KV_WIDTH]
    v = proj[:, POOL_WIDTH + ATTN_WIDTH + KV_WIDTH:]
    k = _head_norm_rope(k, kgain_ref[...], rope_ref[...])
    kd_ref[...] = _expand_kv(k).astype(BF16)
    vd_ref[...] = _expand_kv(v).astype(BF16)


def _in_kernel(x_ref, umeta_ref, gain_ref, win_ref, wpool_ref, pscale_ref, qgain_ref, kgain_ref,
               rope_ref, yp_ref, q_ref, kd_ref, vd_ref, carry_ref):
    j = pl.program_id(1)

    @pl.when(j == 0)
    def _():
        carry_ref[...] = umeta_ref[...]

    proj = _project(x_ref[0], gain_ref[...], win_ref[...])
    rope = rope_ref[...]

    u = proj[:, :POOL_WIDTH]
    ext = jnp.concatenate([carry_ref[...], u], axis=0)
    acc = ext
    for gi, w in enumerate(POOL_WINDOWS):
        lo = gi * POOL_GROUP
        acc = acc[:, POOL_GROUP * (1 if gi else 0):]
        acc = acc + pltpu.roll(acc, w // 2, 0)
        mixed = acc[N_META:, :POOL_GROUP] * (1.0 / w) - u[:, lo:lo + POOL_GROUP]
        y = jnp.dot(mixed.astype(BF16), wpool_ref[gi], preferred_element_type=F32)
        yp_ref[0, :, lo:lo + POOL_GROUP] = (y * pscale_ref[:, lo:lo + POOL_GROUP]).astype(BF16)
    carry_ref[...] = u[TQ_IN - N_META:, :]

    scale = HEAD_DIM ** -0.5
    for c in range(ATTN_WIDTH // LANES):
        lo = POOL_WIDTH + c * LANES
        qc = _head_norm_rope(proj[:, lo:lo + LANES], qgain_ref[...], rope)
        q_ref[0, :, c * LANES:(c + 1) * LANES] = (qc * scale).astype(BF16)
    k = proj[:, POOL_WIDTH + ATTN_WIDTH:POOL_WIDTH + ATTN_WIDTH + KV_WIDTH]
    v = proj[:, POOL_WIDTH + ATTN_WIDTH + KV_WIDTH:]
    k = _head_norm_rope(k, kgain_ref[...], rope)
    kd_ref[0] = _expand_kv(k).astype(BF16)
    vd_ref[0] = _expand_kv(v).astype(BF16)


def _route(logits):
    lane = lax.broadcasted_iota(jnp.int32, logits.shape, 1).astype(F32)
    big = float(LANES)
    gmask = (lane >= N_EXPERTS) & (lane < N_EXPERTS + N_GROUPS)
    gl = jnp.where(gmask, logits, NEG_INF)
    gmax = jnp.max(gl, axis=-1, keepdims=True)
    gsum = jnp.sum(jnp.where(gmask, jnp.exp(gl - gmax), 0.0), axis=-1, keepdims=True)
    g_prob = 1.0 / gsum
    g_idx = jnp.min(jnp.where(gl == gmax, lane, big), axis=-1, keepdims=True) - N_EXPERTS
    e_lo = g_idx * EXPERTS_PER_GROUP
    emask = (lane >= e_lo) & (lane < e_lo + EXPERTS_PER_GROUP)
    el = jnp.where(emask, logits, NEG_INF)
    e1 = jnp.max(el, axis=-1, keepdims=True)
    i1 = jnp.min(jnp.where(el == e1, lane, big), axis=-1, keepdims=True)
    el2 = jnp.where(lane == i1, NEG_INF, el)
    e2 = jnp.max(el2, axis=-1, keepdims=True)
    i2 = jnp.min(jnp.where(el2 == e2, lane, big), axis=-1, keepdims=True)
    t = jnp.exp(e2 - e1)
    w1 = 1.0 / (1.0 + t)
    w2 = t * w1
    gates = jnp.where(lane == i1, w1 * g_prob, 0.0) + jnp.where(lane == i2, w2 * g_prob, 0.0)
    gates = jnp.where(lane == GID_LANE, g_idx, gates)
    counts = jnp.sum(jnp.where(lane == g_idx, 1.0, 0.0), axis=0, keepdims=True)
    return gates, counts.astype(jnp.int32)


def _attn_kernel(sink_ref, x_ref, yp_ref, q_ref, kd_ref, vd_ref, kmeta_ref, vmeta_ref, wout_ref,
                 fgain_ref, wr_ref, h2_ref, m_ref, gates_ref, cnt_ref, yattn_ref):
    j = pl.program_id(1)
    nt = (((1,), (1,)), ((), ()))
    for r in range(TQ_ATT // WINDOW):
        q0 = j * TQ_ATT + r * WINDOW
        start = pl.multiple_of(jnp.maximum(q0 - WINDOW, 0), WINDOW)
        qpos = q0 + lax.broadcasted_iota(jnp.int32, (WINDOW, 2 * WINDOW), 0)
        kpos = start + lax.broadcasted_iota(jnp.int32, (WINDOW, 2 * WINDOW), 1)
        d = qpos - kpos
        bias = jnp.where((d >= 0) & (d < WINDOW), 0.0, NEG_INF)
        for c in range(ATTN_WIDTH // LANES):
            g = c // (N_HEADS // N_KV_HEADS // 2)
            qc = q_ref[0, r * WINDOW:(r + 1) * WINDOW, c * LANES:(c + 1) * LANES]
            acc = jnp.zeros((WINDOW, LANES), F32)
            for hh in range(2):
                col = (2 * g + hh) * LANES
                kb = kd_ref[0, pl.ds(start, 2 * WINDOW), col:col + LANES]
                vb = vd_ref[0, pl.ds(start, 2 * WINDOW), col:col + LANES]
                km = kmeta_ref[:, col:col + LANES]
                vm = vmeta_ref[:, col:col + LANES]
                sink = sink_ref[2 * c + hh]
                s = lax.dot_general(qc, kb, nt, preferred_element_type=F32) + bias
                sm = lax.dot_general(qc, km, nt, preferred_element_type=F32)
                mx = jnp.maximum(jnp.max(s, axis=-1, keepdims=True),
                                 jnp.max(sm, axis=-1, keepdims=True))
                mx = jnp.maximum(mx, sink)
                p = jnp.exp(s - mx)
                pm = jnp.exp(sm - mx)
                den = (jnp.sum(p, axis=-1, keepdims=True) + jnp.sum(pm, axis=-1, keepdims=True)
                       + jnp.exp(sink - mx))
                o = (jnp.dot(p.astype(BF16), vb, preferred_element_type=F32)
                     + jnp.dot(pm.astype(BF16), vm, preferred_element_type=F32))
                acc = acc + o * (1.0 / den)
            yattn_ref[r * WINDOW:(r + 1) * WINDOW, c * LANES:(c + 1) * LANES] = acc.astype(BF16)

    mix = (jnp.dot(yp_ref[0], wout_ref[:POOL_WIDTH, :], preferred_element_type=F32)
           + jnp.dot(yattn_ref[...], wout_ref[POOL_WIDTH:, :], preferred_element_type=F32))
    h2 = x_ref[0] + mix
    h2_ref[0] = h2
    m = _rms_rows(h2, fgain_ref[...]).astype(BF16)
    m_ref[0] = m
    logits = jnp.dot(m, wr_ref[...], preferred_element_type=F32)
    gates, counts = _route(logits)
    gates_ref[0] = gates
    cnt_ref[0] = counts


def _moe_kernel(cnt_ref, m_ref, gates_ref, h2_ref, wg_ref, wu_ref, wd_ref, out_ref, ys_ref, hid_ref):
    i = pl.program_id(0)
    per = TM_MOE // TQ_ATT
    nsub, base = [], []
    total = 0
    for g in range(N_GROUPS):
        n_g = 0
        for s in range(per):
            n_g = n_g + cnt_ref[(i * per + s) * N_GROUPS + g]
        base.append(total * SUB)
        nsub.append(lax.div(n_g + (SUB - 1), SUB))
        total = total + nsub[-1]

    @pl.when(i == 0)
    def _():
        ys_ref[...] = jnp.zeros(ys_ref.shape, BF16)

    gates = gates_ref[...]
    lane = lax.broadcasted_iota(jnp.int32, gates.shape, 1).astype(F32)
    gid = gates[:, GID_LANE:GID_LANE + 1]
    onehot = jnp.where(lane == gid, 1.0, 0.0)
    tri = (lax.broadcasted_iota(jnp.int32, (TM_MOE, TM_MOE), 1)
           < lax.broadcasted_iota(jnp.int32, (TM_MOE, TM_MOE), 0))
    before = jnp.dot(jnp.where(tri, 1.0, 0.0).astype(BF16), onehot.astype(BF16),
                     preferred_element_type=F32)
    base_row = jnp.zeros((1, LANES), F32)
    for g in range(N_GROUPS):
        base_row = jnp.where(lane[:1] == g, _as_f32(base[g]), base_row)
    pos_col = jnp.sum(onehot * (before + base_row), axis=-1, keepdims=True)
    pos_row = jnp.transpose(jnp.broadcast_to(pos_col, (TM_MOE, LANES)))[0:1, :]

    g_hi = gates.astype(BF16)
    g_lo = (gates - g_hi.astype(F32)).astype(BF16)
    ghl = jnp.where(lane < N_EXPERTS, g_hi.astype(F32),
                    pltpu.roll(g_lo.astype(F32), LO_SHIFT, 1)).astype(BF16)

    for g in range(N_GROUPS):
        def body(c, carry, g=g):
            r0 = pl.multiple_of(base[g] + c * SUB, BF16_ROWS)
            rows = (r0 + lax.broadcasted_iota(jnp.int32, (SUB, 1), 0)).astype(F32)
            perm = jnp.where(pos_row == rows, 1.0, 0.0).astype(BF16)
            xs = jnp.dot(perm, m_ref[...], preferred_element_type=F32).astype(BF16)
            gs = jnp.dot(perm, ghl, preferred_element_type=F32)
            for jj in range(EXPERTS_PER_GROUP):
                e = g * EXPERTS_PER_GROUP + jj
                gt = jnp.dot(xs, wg_ref[e], preferred_element_type=F32)
                up = jnp.dot(xs, wu_ref[e], preferred_element_type=F32)
                gate = gs[:, e:e + 1] + gs[:, LO_SHIFT + e:LO_SHIFT + e + 1]
                hid = gt * (1.0 / (1.0 + jnp.exp(-gt))) * up * gate
                hid_ref[:, jj * D_EXPERT:(jj + 1) * D_EXPERT] = hid.astype(BF16)
            y = jnp.dot(hid_ref[...], wd_ref[g * D_GROUP:(g + 1) * D_GROUP, :],
                        preferred_element_type=F32)
            ys_ref[pl.ds(r0, SUB), :] = y.astype(BF16)
            return carry
        lax.fori_loop(0, nsub[g], body, 0)

    def scatter(n_rows):
        cols = lax.broadcasted_iota(jnp.int32, (1, n_rows), 1).astype(F32)
        perm_t = jnp.where(pos_col == cols, 1.0, 0.0).astype(BF16)
        out_ref[...] = h2_ref[...] + jnp.dot(perm_t, ys_ref[:n_rows, :], preferred_element_type=F32)

    short = total * SUB <= YS_SHORT

    @pl.when(short)
    def _():
        scatter(YS_SHORT)

    @pl.when(jnp.logical_not(short))
    def _():
        scatter(YS_ROWS)


def _as_f32(v):
    return float(v) if isinstance(v, int) else v.astype(F32)


def _const_spec(shape):
    n = len(shape)
    return pl.BlockSpec(shape, lambda *_: (0,) * n)


def _resident_spec(shape):
    n = len(shape)
    return pl.BlockSpec(shape, lambda *_: (0,) * n, pipeline_mode=pl.Buffered(1))


def kernel(x, meta_tokens, attn_norm_gain, w_in, w_pool, pool_scale, q_norm_gain, k_norm_gain,
           attn_sinks, w_out, ffn_norm_gain, w_group_router, w_expert_router, w_gate, w_up, w_down):
    B, S, D = x.shape
    assert D == D_MODEL and S % TQ_IN == 0 and S % TQ_ATT == 0 and (B * S) % TM_MOE == 0
    assert w_in.shape[0] == 1, "single layer"
    T = B * S

    rope = _rope_tables(N_META + S)
    again = attn_norm_gain[0][None, :]
    fgain = ffn_norm_gain[0][None, :]
    qgain2 = jnp.tile(q_norm_gain[0], 2)[None, :]
    kgain2 = jnp.tile(k_norm_gain[0], 2)[None, :]
    pscale = pool_scale[0][None, :]
    w_in_b = w_in[0].astype(BF16)
    w_pool_b = w_pool[0].astype(BF16)
    w_out_b = w_out[0].astype(BF16)
    w_r = jnp.concatenate(
        [w_expert_router[0], w_group_router[0],
         jnp.zeros((D, LANES - N_EXPERTS - N_GROUPS), F32)], axis=1).astype(BF16)
    wg = w_gate[0].astype(BF16)
    wu = w_up[0].astype(BF16)
    wd = w_down[0].astype(BF16).reshape(N_EXPERTS * D_EXPERT, D)
    params = pltpu.CompilerParams(vmem_limit_bytes=VMEM_LIMIT)

    u_meta, kd_meta, vd_meta = pl.pallas_call(
        _meta_kernel,
        out_shape=(jax.ShapeDtypeStruct((N_META, POOL_WIDTH), F32),
                   jax.ShapeDtypeStruct((N_META, KV_EXP), BF16),
                   jax.ShapeDtypeStruct((N_META, KV_EXP), BF16)),
        compiler_params=params,
        name="meta_proj",
    )(meta_tokens, again, w_in_b, kgain2, rope[:N_META])

    yp, q, kd, vd = pl.pallas_call(
        _in_kernel,
        grid=(B, S // TQ_IN),
        in_specs=[
            pl.BlockSpec((1, TQ_IN, D), lambda b, j: (b, j, 0)),
            _const_spec((N_META, POOL_WIDTH)),
            _const_spec((1, D)),
            _const_spec((D, IN_WIDTH)),
            _const_spec((len(POOL_WINDOWS), POOL_GROUP, POOL_GROUP)),
            _const_spec((1, POOL_WIDTH)),
            _const_spec((1, LANES)),
            _const_spec((1, LANES)),
            pl.BlockSpec((TQ_IN, 3 * LANES), lambda b, j: (j, 0)),
        ],
        out_specs=(
            pl.BlockSpec((1, TQ_IN, POOL_WIDTH), lambda b, j: (b, j, 0)),
            pl.BlockSpec((1, TQ_IN, ATTN_WIDTH), lambda b, j: (b, j, 0)),
            pl.BlockSpec((1, TQ_IN, KV_EXP), lambda b, j: (b, j, 0)),
            pl.BlockSpec((1, TQ_IN, KV_EXP), lambda b, j: (b, j, 0)),
        ),
        out_shape=(jax.ShapeDtypeStruct((B, S, POOL_WIDTH), BF16),
                   jax.ShapeDtypeStruct((B, S, ATTN_WIDTH), BF16),
                   jax.ShapeDtypeStruct((B, S, KV_EXP), BF16),
                   jax.ShapeDtypeStruct((B, S, KV_EXP), BF16)),
        scratch_shapes=[pltpu.VMEM((N_META, POOL_WIDTH), F32)],
        compiler_params=pltpu.CompilerParams(
            dimension_semantics=("arbitrary", "arbitrary"), vmem_limit_bytes=VMEM_LIMIT),
        name="in_proj",
    )(x, u_meta, again, w_in_b, w_pool_b, pscale, qgain2, kgain2, rope[N_META:])

    h2, m, gates, cnt = pl.pallas_call(
        _attn_kernel,
        grid=(B, S // TQ_ATT),
        in_specs=[
            pl.BlockSpec(memory_space=pltpu.SMEM),
            pl.BlockSpec((1, TQ_ATT, D), lambda b, j: (b, j, 0)),
            pl.BlockSpec((1, TQ_ATT, POOL_WIDTH), lambda b, j: (b, j, 0)),
            pl.BlockSpec((1, TQ_ATT, ATTN_WIDTH), lambda b, j: (b, j, 0)),
            pl.BlockSpec((1, S, KV_EXP), lambda b, j: (b, 0, 0)),
            pl.BlockSpec((1, S, KV_EXP), lambda b, j: (b, 0, 0)),
            _const_spec((N_META, KV_EXP)),
            _const_spec((N_META, KV_EXP)),
            _const_spec((D, D)),
            _const_spec((1, D)),
            _const_spec((D, LANES)),
        ],
        out_specs=(
            pl.BlockSpec((1, TQ_ATT, D), lambda b, j: (b, j, 0)),
            pl.BlockSpec((1, TQ_ATT, D), lambda b, j: (b, j, 0)),
            pl.BlockSpec((1, TQ_ATT, LANES), lambda b, j: (b, j, 0)),
            pl.BlockSpec((1, 1, LANES), lambda b, j: (b * (S // TQ_ATT) + j, 0, 0)),
        ),
        out_shape=(jax.ShapeDtypeStruct((B, S, D), F32),
                   jax.ShapeDtypeStruct((B, S, D), BF16),
                   jax.ShapeDtypeStruct((B, S, LANES), F32),
                   jax.ShapeDtypeStruct((T // TQ_ATT, 1, LANES), jnp.int32)),
        scratch_shapes=[pltpu.VMEM((TQ_ATT, ATTN_WIDTH), BF16)],
        compiler_params=pltpu.CompilerParams(
            dimension_semantics=("arbitrary", "arbitrary"), vmem_limit_bytes=VMEM_LIMIT),
        name="attn_out",
    )(attn_sinks[0], x, yp, q, kd, vd, kd_meta, vd_meta, w_out_b, fgain, w_r)

    cnt_flat = cnt[:, 0, :N_GROUPS].reshape(-1)
    out = pl.pallas_call(
        _moe_kernel,
        grid_spec=pltpu.PrefetchScalarGridSpec(
            num_scalar_prefetch=1,
            grid=(T // TM_MOE,),
            in_specs=[
                pl.BlockSpec((TM_MOE, D), lambda i, c: (i, 0)),
                pl.BlockSpec((TM_MOE, LANES), lambda i, c: (i, 0)),
                pl.BlockSpec((TM_MOE, D), lambda i, c: (i, 0)),
                _resident_spec((N_EXPERTS, D, D_EXPERT)),
                _resident_spec((N_EXPERTS, D, D_EXPERT)),
                _resident_spec((N_EXPERTS * D_EXPERT, D)),
            ],
            out_specs=pl.BlockSpec((TM_MOE, D), lambda i, c: (i, 0)),
            scratch_shapes=[pltpu.VMEM((YS_ROWS, D), BF16),
                            pltpu.VMEM((SUB, D_GROUP), BF16)],
        ),
        out_shape=jax.ShapeDtypeStruct((T, D), F32),
        compiler_params=pltpu.CompilerParams(
            dimension_semantics=("arbitrary",), vmem_limit_bytes=VMEM_LIMIT),
        name="moe",
    )(cnt_flat, m.reshape(T, D), gates.reshape(T, LANES), h2.reshape(T, D), wg, wu, wd)
    return out.reshape(B, S, D)
```

```python
import functools

import numpy as np
import jax
import jax.numpy as jnp
from jax import lax
from jax.experimental import pallas as pl
from jax.experimental.pallas import tpu as pltpu

D_MODEL = 1024
N_META = 16
POOL_WIDTH = 512
POOL_WINDOWS = (2, 4, 8, 16)
POOL_GROUP = 128
HEAD_DIM = 64
N_HEADS = 8
N_KV_HEADS = 2
ATTN_WIDTH = N_HEADS * HEAD_DIM
KV_WIDTH = N_KV_HEADS * HEAD_DIM
WINDOW = 128
ROT_DIM = HEAD_DIM // 4
ROPE_THETA = 500000.0
IN_WIDTH = POOL_WIDTH + ATTN_WIDTH + 2 * KV_WIDTH
N_GROUPS = 4
EXPERTS_PER_GROUP = 4
N_EXPERTS = 16
D_EXPERT = 256
EPS = 1e-6
NEG_INF = -1e30

LANES = 128
KV_EXP = 4 * LANES
TQ_IN = 512
TQ_ATT = 256
KEY_ROWS = 2 * (2 * WINDOW + N_META)
KEY_PAD = -(-KEY_ROWS // LANES) * LANES
N_ITEMS = (TQ_ATT // WINDOW) * (ATTN_WIDTH // LANES)
TM_MOE = 512
SUB = 144
BF16_ROWS = 16
MXU_DEPTH = 256
YS_ROWS = -(-((TM_MOE + N_GROUPS * (SUB - 1)) // SUB * SUB) // MXU_DEPTH) * MXU_DEPTH
YS_SHORT = YS_ROWS - MXU_DEPTH
GID_LANE = N_EXPERTS
LO_SHIFT = 32
D_GROUP = EXPERTS_PER_GROUP * D_EXPERT
VMEM_LIMIT = 56 * 1024 * 1024

BF16 = jnp.bfloat16
F32 = jnp.float32


def _rope_tables(n_pos):
    half = ROT_DIM // 2
    inv_freq = 1.0 / (ROPE_THETA ** (np.arange(half, dtype=np.float64) / half))
    ang = np.arange(n_pos, dtype=np.float64)[:, None] * inv_freq[None, :]
    cos, sin = np.cos(ang), np.sin(ang)
    c = np.ones((n_pos, HEAD_DIM)); c[:, :half] = cos; c[:, half:ROT_DIM] = cos
    sa = np.zeros((n_pos, HEAD_DIM)); sa[:, :half] = -sin
    sb = np.zeros((n_pos, HEAD_DIM)); sb[:, half:ROT_DIM] = sin
    tab = np.concatenate([np.tile(c, (1, 2)), np.tile(sa, (1, 2)), np.tile(sb, (1, 2))], axis=1)
    return jnp.asarray(tab, dtype=F32)


def _rms_rows(x, gain):
    ms = jnp.mean(x * x, axis=-1, keepdims=True)
    return x * lax.rsqrt(ms + EPS) * gain


def _head_norm_rope(xc, gain2, rope):
    lo = lax.broadcasted_iota(jnp.int32, xc.shape, 1) < HEAD_DIM
    sq = xc * xc
    s_lo = jnp.sum(jnp.where(lo, sq, 0.0), axis=-1, keepdims=True)
    s_hi = jnp.sum(jnp.where(lo, 0.0, sq), axis=-1, keepdims=True)
    ms = jnp.where(lo, s_lo, s_hi) * (1.0 / HEAD_DIM)
    y = xc * lax.rsqrt(ms + EPS) * gain2
    c, sa, sb = rope[:, 0:LANES], rope[:, LANES:2 * LANES], rope[:, 2 * LANES:3 * LANES]
    half = ROT_DIM // 2
    return y * c + pltpu.roll(y, LANES - half, 1) * sa + pltpu.roll(y, half, 1) * sb


def _expand_kv(t):
    lo = lax.broadcasted_iota(jnp.int32, t.shape, 1) < HEAD_DIM
    sw = pltpu.roll(t, HEAD_DIM, 1)
    z = jnp.zeros_like(t)
    return jnp.concatenate([jnp.where(lo, t, z), jnp.where(lo, z, sw),
                            jnp.where(lo, sw, z), jnp.where(lo, z, t)], axis=1)


def _project(x, gain, w_in):
    a = _rms_rows(x, gain).astype(BF16)
    return jnp.dot(a, w_in, preferred_element_type=F32)


def _meta_kernel(meta_ref, gain_ref, win_ref, kgain_ref, rope_ref, u_ref, kd_ref, v_ref):
    proj = _project(meta_ref[...], gain_ref[...], win_ref[...])
    u_ref[...] = proj[:, :POOL_WIDTH]
    k = proj[:, POOL_WIDTH + ATTN_WIDTH:POOL_WIDTH + ATTN_WIDTH + KV_WIDTH]
    v = proj[:, POOL_WIDTH + ATTN_WIDTH + KV_WIDTH:]
    k = _head_norm_rope(k, kgain_ref[...], rope_ref[...])
    kd_ref[...] = _expand_kv(k).astype(BF16)
    v_ref[...] = v.astype(BF16)


def _in_kernel(x_ref, umeta_ref, gain_ref, win_ref, wpool_ref, pscale_ref, qgain_ref, kgain_ref,
               rope_ref, yp_ref, q_ref, kd_ref, vt_ref, carry_ref):
    j = pl.program_id(1)

    @pl.when(j == 0)
    def _():
        carry_ref[...] = umeta_ref[...]

    proj = _project(x_ref[0], gain_ref[...], win_ref[...])
    rope = rope_ref[...]

    u = proj[:, :POOL_WIDTH]
    ext = jnp.concatenate([carry_ref[...], u], axis=0)
    acc = ext
    for gi, w in enumerate(POOL_WINDOWS):
        lo = gi * POOL_GROUP
        acc = acc[:, POOL_GROUP * (1 if gi else 0):]
        acc = acc + pltpu.roll(acc, w // 2, 0)
        mixed = acc[N_META:, :POOL_GROUP] * (1.0 / w) - u[:, lo:lo + POOL_GROUP]
        y = jnp.dot(mixed.astype(BF16), wpool_ref[gi], preferred_element_type=F32)
        yp_ref[0, :, lo:lo + POOL_GROUP] = (y * pscale_ref[:, lo:lo + POOL_GROUP]).astype(BF16)
    carry_ref[...] = u[TQ_IN - N_META:, :]

    scale = HEAD_DIM ** -0.5
    for c in range(ATTN_WIDTH // LANES):
        lo = POOL_WIDTH + c * LANES
        qc = _head_norm_rope(proj[:, lo:lo + LANES], qgain_ref[...], rope)
        q_ref[0, :, c * LANES:(c + 1) * LANES] = (qc * scale).astype(BF16)
    k = proj[:, POOL_WIDTH + ATTN_WIDTH:POOL_WIDTH + ATTN_WIDTH + KV_WIDTH]
    v = proj[:, POOL_WIDTH + ATTN_WIDTH + KV_WIDTH:]
    k = _head_norm_rope(k, kgain_ref[...], rope)
    kd_ref[0] = _expand_kv(k).astype(BF16)
    vt_ref[0] = jnp.transpose(v).astype(BF16)


def _route(logits):
    lane = lax.broadcasted_iota(jnp.int32, logits.shape, 1).astype(F32)
    big = float(LANES)
    gmask = (lane >= N_EXPERTS) & (lane < N_EXPERTS + N_GROUPS)
    gl = jnp.where(gmask, logits, NEG_INF)
    gmax = jnp.max(gl, axis=-1, keepdims=True)
    gsum = jnp.sum(jnp.where(gmask, jnp.exp(gl - gmax), 0.0), axis=-1, keepdims=True)
    g_prob = 1.0 / gsum
    g_idx = jnp.min(jnp.where(gl == gmax, lane, big), axis=-1, keepdims=True) - N_EXPERTS
    e_lo = g_idx * EXPERTS_PER_GROUP
    emask = (lane >= e_lo) & (lane < e_lo + EXPERTS_PER_GROUP)
    el = jnp.where(emask, logits, NEG_INF)
    e1 = jnp.max(el, axis=-1, keepdims=True)
    i1 = jnp.min(jnp.where(el == e1, lane, big), axis=-1, keepdims=True)
    el2 = jnp.where(lane == i1, NEG_INF, el)
    e2 = jnp.max(el2, axis=-1, keepdims=True)
    i2 = jnp.min(jnp.where(el2 == e2, lane, big), axis=-1, keepdims=True)
    t = jnp.exp(e2 - e1)
    w1 = 1.0 / (1.0 + t)
    w2 = t * w1
    gates = jnp.where(lane == i1, w1 * g_prob, 0.0) + jnp.where(lane == i2, w2 * g_prob, 0.0)
    gates = jnp.where(lane == GID_LANE, g_idx, gates)
    counts = jnp.sum(jnp.where(lane == g_idx, 1.0, 0.0), axis=0, keepdims=True)
    return gates, counts.astype(jnp.int32)


def _attn_kernel(sink_ref, x_ref, yp_ref, q_ref, kd_ref, vt_ref, kmeta_ref, vtmeta_ref, wout_ref,
                 fgain_ref, wr_ref, h2_ref, m_ref, gates_ref, cnt_ref,
                 yattn_ref, kcat_ref, vbd_ref, s_ref, p_ref, inv_ref):
    j = pl.program_id(1)
    nt = (((1,), (1,)), ((), ()))
    n_r = TQ_ATT // WINDOW
    n_c = ATTN_WIDTH // LANES
    band = 2 * WINDOW
    m0 = 2 * band

    @pl.when((pl.program_id(0) == 0) & (j == 0))
    def _():
        vbd_ref[...] = jnp.zeros(vbd_ref.shape, BF16)
        p_ref[...] = jnp.zeros(p_ref.shape, BF16)

    starts, biases = [], []
    for r in range(n_r):
        q0 = j * TQ_ATT + r * WINDOW
        start = pl.multiple_of(jnp.maximum(q0 - WINDOW, 0), WINDOW)
        starts.append(start)
        kpos = start + lax.broadcasted_iota(jnp.int32, (band, WINDOW), 0)
        qpos = q0 + lax.broadcasted_iota(jnp.int32, (band, WINDOW), 1)
        d = qpos - kpos
        biases.append(jnp.where((d >= 0) & (d < WINDOW), 0.0, NEG_INF))
        for g in range(N_KV_HEADS):
            vb = vt_ref[0, g * HEAD_DIM:(g + 1) * HEAD_DIM, pl.ds(start, band)]
            vm = vtmeta_ref[g * HEAD_DIM:(g + 1) * HEAD_DIM, :]
            for hh in range(2):
                col = (2 * g + hh) * LANES
                kcat_ref[r, g, hh * band:(hh + 1) * band, :] = kd_ref[0, pl.ds(start, band), col:col + LANES]
                kcat_ref[r, g, m0 + hh * N_META:m0 + (hh + 1) * N_META, :] = kmeta_ref[:, col:col + LANES]
                rows = slice(hh * HEAD_DIM, (hh + 1) * HEAD_DIM)
                vbd_ref[r, g, rows, hh * band:(hh + 1) * band] = vb
                vbd_ref[r, g, rows, m0 + hh * N_META:m0 + (hh + 1) * N_META] = vm

    for r in range(n_r):
        for c in range(n_c):
            qc = q_ref[0, r * WINDOW:(r + 1) * WINDOW, c * LANES:(c + 1) * LANES]
            s_ref[r * n_c + c] = lax.dot_general(kcat_ref[r, c // 2], qc, nt, preferred_element_type=F32)

    for r in range(n_r):
        for c in range(n_c):
            it = r * n_c + c
            for hh in range(2):
                sink = sink_ref[2 * c + hh]
                s = s_ref[it, hh * band:(hh + 1) * band, :] + biases[r]
                sm = s_ref[it, m0 + hh * N_META:m0 + (hh + 1) * N_META, :]
                mx = jnp.maximum(jnp.max(s, axis=0, keepdims=True), jnp.max(sm, axis=0, keepdims=True))
                mx = jnp.maximum(mx, sink)
                p = jnp.exp(s - mx)
                pm = jnp.exp(sm - mx)
                den = (jnp.sum(p, axis=0, keepdims=True) + jnp.sum(pm, axis=0, keepdims=True)
                       + jnp.exp(sink - mx))
                p_ref[it, hh * band:(hh + 1) * band, :] = p.astype(BF16)
                p_ref[it, m0 + hh * N_META:m0 + (hh + 1) * N_META, :] = pm.astype(BF16)
                inv_ref[it, hh * HEAD_DIM:(hh + 1) * HEAD_DIM, :] = jnp.broadcast_to(
                    1.0 / den, (HEAD_DIM, WINDOW))

    for r in range(n_r):
        for c in range(n_c):
            it = r * n_c + c
            o_t = jnp.dot(vbd_ref[r, c // 2], p_ref[it], preferred_element_type=F32) * inv_ref[it]
            yattn_ref[r * WINDOW:(r + 1) * WINDOW, c * LANES:(c + 1) * LANES] = (
                jnp.transpose(o_t).astype(BF16))

    mix = (jnp.dot(yp_ref[0], wout_ref[:POOL_WIDTH, :], preferred_element_type=F32)
           + jnp.dot(yattn_ref[...], wout_ref[POOL_WIDTH:, :], preferred_element_type=F32))
    h2 = x_ref[0] + mix
    h2_ref[0] = h2
    m = _rms_rows(h2, fgain_ref[...]).astype(BF16)
    m_ref[0] = m
    logits = jnp.dot(m, wr_ref[...], preferred_element_type=F32)
    gates, counts = _route(logits)
    gates_ref[0] = gates
    cnt_ref[0] = counts


def _moe_kernel(cnt_ref, m_ref, gates_ref, h2_ref, wg_ref, wu_ref, wd_ref, out_ref, ys_ref, hid_ref):
    i = pl.program_id(0)
    per = TM_MOE // TQ_ATT
    nsub, base = [], []
    total = 0
    for g in range(N_GROUPS):
        n_g = 0
        for s in range(per):
            n_g = n_g + cnt_ref[(i * per + s) * N_GROUPS + g]
        base.append(total * SUB)
        nsub.append(lax.div(n_g + (SUB - 1), SUB))
        total = total + nsub[-1]

    @pl.when(i == 0)
    def _():
        ys_ref[...] = jnp.zeros(ys_ref.shape, BF16)

    gates = gates_ref[...]
    lane = lax.broadcasted_iota(jnp.int32, gates.shape, 1).astype(F32)
    gid = gates[:, GID_LANE:GID_LANE + 1]
    onehot = jnp.where(lane == gid, 1.0, 0.0)
    tri = (lax.broadcasted_iota(jnp.int32, (TM_MOE, TM_MOE), 1)
           < lax.broadcasted_iota(jnp.int32, (TM_MOE, TM_MOE), 0))
    before = jnp.dot(jnp.where(tri, 1.0, 0.0).astype(BF16), onehot.astype(BF16),
                     preferred_element_type=F32)
    base_row = jnp.zeros((1, LANES), F32)
    for g in range(N_GROUPS):
        base_row = jnp.where(lane[:1] == g, _as_f32(base[g]), base_row)
    pos_col = jnp.sum(onehot * (before + base_row), axis=-1, keepdims=True)
    pos_row = jnp.transpose(jnp.broadcast_to(pos_col, (TM_MOE, LANES)))[0:1, :]

    g_hi = gates.astype(BF16)
    g_lo = (gates - g_hi.astype(F32)).astype(BF16)
    ghl = jnp.where(lane < N_EXPERTS, g_hi.astype(F32),
                    pltpu.roll(g_lo.astype(F32), LO_SHIFT, 1)).astype(BF16)

    for g in range(N_GROUPS):
        def body(c, carry, g=g):
            r0 = pl.multiple_of(base[g] + c * SUB, BF16_ROWS)
            rows = (r0 + lax.broadcasted_iota(jnp.int32, (SUB, 1), 0)).astype(F32)
            perm = jnp.where(pos_row == rows, 1.0, 0.0).astype(BF16)
            xs = jnp.dot(perm, m_ref[...], preferred_element_type=F32).astype(BF16)
            gs = jnp.dot(perm, ghl, preferred_element_type=F32)
            for jj in range(EXPERTS_PER_GROUP):
                e = g * EXPERTS_PER_GROUP + jj
                gt = jnp.dot(xs, wg_ref[e], preferred_element_type=F32)
                up = jnp.dot(xs, wu_ref[e], preferred_element_type=F32)
                gate = gs[:, e:e + 1] + gs[:, LO_SHIFT + e:LO_SHIFT + e + 1]
                hid = gt * (1.0 / (1.0 + jnp.exp(-gt))) * up * gate
                hid_ref[:, jj * D_EXPERT:(jj + 1) * D_EXPERT] = hid.astype(BF16)
            y = jnp.dot(hid_ref[...], wd_ref[g * D_GROUP:(g + 1) * D_GROUP, :],
                        preferred_element_type=F32)
            ys_ref[pl.ds(r0, SUB), :] = y.astype(BF16)
            return carry
        lax.fori_loop(0, nsub[g], body, 0)

    def scatter(n_rows):
        cols = lax.broadcasted_iota(jnp.int32, (1, n_rows), 1).astype(F32)
        perm_t = jnp.where(pos_col == cols, 1.0, 0.0).astype(BF16)
        out_ref[...] = h2_ref[...] + jnp.dot(perm_t, ys_ref[:n_rows, :], preferred_element_type=F32)

    short = total * SUB <= YS_SHORT

    @pl.when(short)
    def _():
        scatter(YS_SHORT)

    @pl.when(jnp.logical_not(short))
    def _():
        scatter(YS_ROWS)


def _as_f32(v):
    return float(v) if isinstance(v, int) else v.astype(F32)


def _const_spec(shape):
    n = len(shape)
    return pl.BlockSpec(shape, lambda *_: (0,) * n)


def _resident_spec(shape):
    n = len(shape)
    return pl.BlockSpec(shape, lambda *_: (0,) * n, pipeline_mode=pl.Buffered(1))


def kernel(x, meta_tokens, attn_norm_gain, w_in, w_pool, pool_scale, q_norm_gain, k_norm_gain,
           attn_sinks, w_out, ffn_norm_gain, w_group_router, w_expert_router, w_gate, w_up, w_down):
    B, S, D = x.shape
    assert D == D_MODEL and S % TQ_IN == 0 and S % TQ_ATT == 0 and (B * S) % TM_MOE == 0
    assert w_in.shape[0] == 1, "single layer"
    T = B * S

    rope = _rope_tables(N_META + S)
    again = attn_norm_gain[0][None, :]
    fgain = ffn_norm_gain[0][None, :]
    qgain2 = jnp.tile(q_norm_gain[0], 2)[None, :]
    kgain2 = jnp.tile(k_norm_gain[0], 2)[None, :]
    pscale = pool_scale[0][None, :]
    w_in_b = w_in[0].astype(BF16)
    w_pool_b = w_pool[0].astype(BF16)
    w_out_b = w_out[0].astype(BF16)
    w_r = jnp.concatenate(
        [w_expert_router[0], w_group_router[0],
         jnp.zeros((D, LANES - N_EXPERTS - N_GROUPS), F32)], axis=1).astype(BF16)
    wg = w_gate[0].astype(BF16)
    wu = w_up[0].astype(BF16)
    wd = w_down[0].astype(BF16).reshape(N_EXPERTS * D_EXPERT, D)
    params = pltpu.CompilerParams(vmem_limit_bytes=VMEM_LIMIT)

    u_meta, kd_meta, v_meta = pl.pallas_call(
        _meta_kernel,
        out_shape=(jax.ShapeDtypeStruct((N_META, POOL_WIDTH), F32),
                   jax.ShapeDtypeStruct((N_META, KV_EXP), BF16),
                   jax.ShapeDtypeStruct((N_META, KV_WIDTH), BF16)),
        compiler_params=params,
        name="meta_proj",
    )(meta_tokens, again, w_in_b, kgain2, rope[:N_META])
    vt_meta = v_meta.T

    yp, q, kd, vt = pl.pallas_call(
        _in_kernel,
        grid=(B, S // TQ_IN),
        in_specs=[
            pl.BlockSpec((1, TQ_IN, D), lambda b, j: (b, j, 0)),
            _const_spec((N_META, POOL_WIDTH)),
            _const_spec((1, D)),
            _const_spec((D, IN_WIDTH)),
            _const_spec((len(POOL_WINDOWS), POOL_GROUP, POOL_GROUP)),
            _const_spec((1, POOL_WIDTH)),
            _const_spec((1, LANES)),
            _const_spec((1, LANES)),
            pl.BlockSpec((TQ_IN, 3 * LANES), lambda b, j: (j, 0)),
        ],
        out_specs=(
            pl.BlockSpec((1, TQ_IN, POOL_WIDTH), lambda b, j: (b, j, 0)),
            pl.BlockSpec((1, TQ_IN, ATTN_WIDTH), lambda b, j: (b, j, 0)),
            pl.BlockSpec((1, TQ_IN, KV_EXP), lambda b, j: (b, j, 0)),
            pl.BlockSpec((1, KV_WIDTH, TQ_IN), lambda b, j: (b, 0, j)),
        ),
        out_shape=(jax.ShapeDtypeStruct((B, S, POOL_WIDTH), BF16),
                   jax.ShapeDtypeStruct((B, S, ATTN_WIDTH), BF16),
                   jax.ShapeDtypeStruct((B, S, KV_EXP), BF16),
                   jax.ShapeDtypeStruct((B, KV_WIDTH, S), BF16)),
        scratch_shapes=[pltpu.VMEM((N_META, POOL_WIDTH), F32)],
        compiler_params=pltpu.CompilerParams(
            dimension_semantics=("arbitrary", "arbitrary"), vmem_limit_bytes=VMEM_LIMIT),
        name="in_proj",
    )(x, u_meta, again, w_in_b, w_pool_b, pscale, qgain2, kgain2, rope[N_META:])

    h2, m, gates, cnt = pl.pallas_call(
        _attn_kernel,
        grid=(B, S // TQ_ATT),
        in_specs=[
            pl.BlockSpec(memory_space=pltpu.SMEM),
            pl.BlockSpec((1, TQ_ATT, D), lambda b, j: (b, j, 0)),
            pl.BlockSpec((1, TQ_ATT, POOL_WIDTH), lambda b, j: (b, j, 0)),
            pl.BlockSpec((1, TQ_ATT, ATTN_WIDTH), lambda b, j: (b, j, 0)),
            pl.BlockSpec((1, S, KV_EXP), lambda b, j: (b, 0, 0)),
            pl.BlockSpec((1, KV_WIDTH, S), lambda b, j: (b, 0, 0)),
            _const_spec((N_META, KV_EXP)),
            _const_spec((KV_WIDTH, N_META)),
            _const_spec((D, D)),
            _const_spec((1, D)),
            _const_spec((D, LANES)),
        ],
        out_specs=(
            pl.BlockSpec((1, TQ_ATT, D), lambda b, j: (b, j, 0)),
            pl.BlockSpec((1, TQ_ATT, D), lambda b, j: (b, j, 0)),
            pl.BlockSpec((1, TQ_ATT, LANES), lambda b, j: (b, j, 0)),
            pl.BlockSpec((1, 1, LANES), lambda b, j: (b * (S // TQ_ATT) + j, 0, 0)),
        ),
        out_shape=(jax.ShapeDtypeStruct((B, S, D), F32),
                   jax.ShapeDtypeStruct((B, S, D), BF16),
                   jax.ShapeDtypeStruct((B, S, LANES), F32),
                   jax.ShapeDtypeStruct((T // TQ_ATT, 1, LANES), jnp.int32)),
        scratch_shapes=[
            pltpu.VMEM((TQ_ATT, ATTN_WIDTH), BF16),
            pltpu.VMEM((TQ_ATT // WINDOW, N_KV_HEADS, KEY_ROWS, LANES), BF16),
            pltpu.VMEM((TQ_ATT // WINDOW, N_KV_HEADS, LANES, KEY_PAD), BF16),
            pltpu.VMEM((N_ITEMS, KEY_ROWS, LANES), F32),
            pltpu.VMEM((N_ITEMS, KEY_PAD, LANES), BF16),
            pltpu.VMEM((N_ITEMS, LANES, LANES), F32),
        ],
        compiler_params=pltpu.CompilerParams(
            dimension_semantics=("arbitrary", "arbitrary"), vmem_limit_bytes=VMEM_LIMIT),
        name="attn_out",
    )(attn_sinks[0], x, yp, q, kd, vt, kd_meta, vt_meta, w_out_b, fgain, w_r)

    cnt_flat = cnt[:, 0, :N_GROUPS].reshape(-1)
    out = pl.pallas_call(
        _moe_kernel,
        grid_spec=pltpu.PrefetchScalarGridSpec(
            num_scalar_prefetch=1,
            grid=(T // TM_MOE,),
            in_specs=[
                pl.BlockSpec((TM_MOE, D), lambda i, c: (i, 0)),
                pl.BlockSpec((TM_MOE, LANES), lambda i, c: (i, 0)),
                pl.BlockSpec((TM_MOE, D), lambda i, c: (i, 0)),
                _resident_spec((N_EXPERTS, D, D_EXPERT)),
                _resident_spec((N_EXPERTS, D, D_EXPERT)),
                _resident_spec((N_EXPERTS * D_EXPERT, D)),
            ],
            out_specs=pl.BlockSpec((TM_MOE, D), lambda i, c: (i, 0)),
            scratch_shapes=[pltpu.VMEM((YS_ROWS, D), BF16),
                            pltpu.VMEM((SUB, D_GROUP), BF16)],
        ),
        out_shape=jax.ShapeDtypeStruct((T, D), F32),
        compiler_params=pltpu.CompilerParams(
            dimension_semantics=("arbitrary",), vmem_limit_bytes=VMEM_LIMIT),
        name="moe",
    )(cnt_flat, m.reshape(T, D), gates.reshape(T, LANES), h2.reshape(T, D), wg, wu, wd)
    return out.reshape(B, S, D)
```

```python
import functools

import numpy as np
import jax
import jax.numpy as jnp
from jax import lax
from jax.experimental import pallas as pl
from jax.experimental.pallas import tpu as pltpu

D_MODEL = 1024
N_META = 16
POOL_WIDTH = 512
POOL_WINDOWS = (2, 4, 8, 16)
POOL_GROUP = 128
HEAD_DIM = 64
N_HEADS = 8
N_KV_HEADS = 2
ATTN_WIDTH = N_HEADS * HEAD_DIM
KV_WIDTH = N_KV_HEADS * HEAD_DIM
WINDOW = 128
ROT_DIM = HEAD_DIM // 4
ROPE_THETA = 500000.0
IN_WIDTH = POOL_WIDTH + ATTN_WIDTH + 2 * KV_WIDTH
N_GROUPS = 4
EXPERTS_PER_GROUP = 4
N_EXPERTS = 16
D_EXPERT = 256
EPS = 1e-6
NEG_INF = -1e30

LANES = 128
KV_EXP = 4 * LANES
TQ_IN = 1024
RB_IN = 128
TQ_ATT = 256
KEY_ROWS = 2 * (2 * WINDOW + N_META)
KEY_PAD = -(-KEY_ROWS // LANES) * LANES
N_ITEMS = (TQ_ATT // WINDOW) * (ATTN_WIDTH // LANES)
TM_MOE = 512
SUB = 144
BF16_ROWS = 16
MXU_DEPTH = 256
YS_ROWS = -(-((TM_MOE + N_GROUPS * (SUB - 1)) // SUB * SUB) // MXU_DEPTH) * MXU_DEPTH
YS_SHORT = YS_ROWS - MXU_DEPTH
GID_LANE = N_EXPERTS
LO_SHIFT = 32
D_GROUP = EXPERTS_PER_GROUP * D_EXPERT
VMEM_LIMIT = 56 * 1024 * 1024

BF16 = jnp.bfloat16
F32 = jnp.float32


def _rope_tables(n_pos):
    half = ROT_DIM // 2
    inv_freq = 1.0 / (ROPE_THETA ** (np.arange(half, dtype=np.float64) / half))
    ang = np.arange(n_pos, dtype=np.float64)[:, None] * inv_freq[None, :]
    cos, sin = np.cos(ang), np.sin(ang)
    c = np.ones((n_pos, HEAD_DIM)); c[:, :half] = cos; c[:, half:ROT_DIM] = cos
    sa = np.zeros((n_pos, HEAD_DIM)); sa[:, :half] = -sin
    sb = np.zeros((n_pos, HEAD_DIM)); sb[:, half:ROT_DIM] = sin
    tab = np.concatenate([np.tile(c, (1, 2)), np.tile(sa, (1, 2)), np.tile(sb, (1, 2))], axis=1)
    return jnp.asarray(tab, dtype=F32)


def _rms_unit(x):
    ms = jnp.mean(x * x, axis=-1, keepdims=True)
    return x * lax.rsqrt(ms + EPS)


def _rms_rows(x, gain):
    return _rms_unit(x) * gain


def _head_norm_rope_many(xs, gains, rope):
    lo = lax.broadcasted_iota(jnp.int32, xs[0].shape, 1) < HEAD_DIM
    sums = []
    for x in xs:
        sq = x * x
        sums.append((jnp.sum(jnp.where(lo, sq, 0.0), axis=-1, keepdims=True),
                     jnp.sum(jnp.where(lo, 0.0, sq), axis=-1, keepdims=True)))
    ys = [x * lax.rsqrt(jnp.where(lo, s_lo, s_hi) + HEAD_DIM * EPS) * g
          for x, g, (s_lo, s_hi) in zip(xs, gains, sums)]
    half = ROT_DIM // 2
    rolled = [(pltpu.roll(y, LANES - half, 1), pltpu.roll(y, half, 1)) for y in ys]
    c, sa, sb = rope[:, 0:LANES], rope[:, LANES:2 * LANES], rope[:, 2 * LANES:3 * LANES]
    return [y * c + ra * sa + rb * sb for y, (ra, rb) in zip(ys, rolled)]


def _head_norm_rope(xc, gain2, rope):
    return _head_norm_rope_many([xc], [gain2], rope)[0]


def _expand_kv(t):
    lo = lax.broadcasted_iota(jnp.int32, t.shape, 1) < HEAD_DIM
    sw = pltpu.roll(t, HEAD_DIM, 1)
    z = jnp.zeros_like(t)
    return jnp.concatenate([jnp.where(lo, t, z), jnp.where(lo, z, sw),
                            jnp.where(lo, sw, z), jnp.where(lo, z, t)], axis=1)


def _project(x, w_in):
    return jnp.dot(_rms_unit(x).astype(BF16), w_in, preferred_element_type=F32)


def _meta_kernel(meta_ref, win_ref, kgain_ref, rope_ref, u_ref, kd_ref, v_ref):
    proj = _project(meta_ref[...], win_ref[...])
    u_ref[...] = proj[:, :POOL_WIDTH]
    k = proj[:, POOL_WIDTH + ATTN_WIDTH:POOL_WIDTH + ATTN_WIDTH + KV_WIDTH]
    v = proj[:, POOL_WIDTH + ATTN_WIDTH + KV_WIDTH:]
    k = _head_norm_rope(k, kgain_ref[...], rope_ref[...])
    kd_ref[...] = _expand_kv(k).astype(BF16)
    v_ref[...] = v.astype(BF16)


def _in_kernel(x_ref, umeta_ref, win_ref, wpool_ref, pscale_ref, qgain_ref, kgain_ref,
               rope_ref, yp_ref, q_ref, kd_ref, vt_ref, carry_ref, proj_ref):
    j = pl.program_id(1)

    @pl.when(j == 0)
    def _():
        carry_ref[...] = umeta_ref[...]

    n_sub = TQ_IN // RB_IN

    def normalise(i):
        return _rms_unit(x_ref[0, i * RB_IN:(i + 1) * RB_IN, :]).astype(BF16)

    def project(i, a):
        proj_ref[i] = jnp.dot(a, win_ref[...], preferred_element_type=F32)

    def finish(i):
        rows = slice(i * RB_IN, (i + 1) * RB_IN)
        rope = rope_ref[rows, :]
        u = proj_ref[i, :, :POOL_WIDTH]
        acc = jnp.concatenate([carry_ref[...], u], axis=0)
        for gi, w in enumerate(POOL_WINDOWS):
            lo = gi * POOL_GROUP
            acc = acc[:, POOL_GROUP * (1 if gi else 0):]
            acc = acc + pltpu.roll(acc, w // 2, 0)
            mixed = acc[N_META:, :POOL_GROUP] * (1.0 / w) - u[:, lo:lo + POOL_GROUP]
            y = jnp.dot(mixed.astype(BF16), wpool_ref[gi], preferred_element_type=F32)
            yp_ref[0, rows, lo:lo + POOL_GROUP] = (y * pscale_ref[:, lo:lo + POOL_GROUP]).astype(BF16)
        carry_ref[...] = u[RB_IN - N_META:, :]

        n_qc = ATTN_WIDTH // LANES
        xs = [proj_ref[i, :, POOL_WIDTH + c * LANES:POOL_WIDTH + (c + 1) * LANES] for c in range(n_qc + 1)]
        outs = _head_norm_rope_many(xs, [qgain_ref[...]] * n_qc + [kgain_ref[...]], rope)
        for c in range(n_qc):
            q_ref[0, rows, c * LANES:(c + 1) * LANES] = outs[c].astype(BF16)
        kd_ref[0, rows, :] = _expand_kv(outs[n_qc]).astype(BF16)
        v = proj_ref[i, :, POOL_WIDTH + ATTN_WIDTH + KV_WIDTH:]
        vt_ref[0, :, rows] = jnp.transpose(v).astype(BF16)

    a_next = normalise(0)
    for i in range(n_sub):
        project(i, a_next)
        if i + 1 < n_sub:
            a_next = normalise(i + 1)
        if i > 0:
            finish(i - 1)
    finish(n_sub - 1)


def _route(logits):
    lane = lax.broadcasted_iota(jnp.int32, logits.shape, 1).astype(F32)
    big = float(LANES)
    gmask = (lane >= N_EXPERTS) & (lane < N_EXPERTS + N_GROUPS)
    gl = jnp.where(gmask, logits, NEG_INF)
    gmax = jnp.max(gl, axis=-1, keepdims=True)
    gsum = jnp.sum(jnp.where(gmask, jnp.exp(gl - gmax), 0.0), axis=-1, keepdims=True)
    g_prob = 1.0 / gsum
    g_idx = jnp.min(jnp.where(gl == gmax, lane, big), axis=-1, keepdims=True) - N_EXPERTS
    e_lo = g_idx * EXPERTS_PER_GROUP
    emask = (lane >= e_lo) & (lane < e_lo + EXPERTS_PER_GROUP)
    el = jnp.where(emask, logits, NEG_INF)
    e1 = jnp.max(el, axis=-1, keepdims=True)
    i1 = jnp.min(jnp.where(el == e1, lane, big), axis=-1, keepdims=True)
    el2 = jnp.where(lane == i1, NEG_INF, el)
    e2 = jnp.max(el2, axis=-1, keepdims=True)
    i2 = jnp.min(jnp.where(el2 == e2, lane, big), axis=-1, keepdims=True)
    t = jnp.exp(e2 - e1)
    w1 = 1.0 / (1.0 + t)
    w2 = t * w1
    gates = jnp.where(lane == i1, w1 * g_prob, 0.0) + jnp.where(lane == i2, w2 * g_prob, 0.0)
    gates = jnp.where(lane == GID_LANE, g_idx, gates)
    counts = jnp.sum(jnp.where(lane == g_idx, 1.0, 0.0), axis=0, keepdims=True)
    return gates, counts.astype(jnp.int32)


def _attn_kernel(sink_ref, x_ref, yp_ref, q_ref, kd_ref, vt_ref, kmeta_ref, vtmeta_ref, wout_ref,
                 fgain_ref, wr_ref, h2_ref, m_ref, gates_ref, cnt_ref,
                 yattn_ref, kcat_ref, vbd_ref, s_ref, p_ref, inv_ref):
    j = pl.program_id(1)
    nt = (((1,), (1,)), ((), ()))
    n_r = TQ_ATT // WINDOW
    n_c = ATTN_WIDTH // LANES
    band = 2 * WINDOW
    m0 = 2 * band

    @pl.when((pl.program_id(0) == 0) & (j == 0))
    def _():
        vbd_ref[...] = jnp.zeros(vbd_ref.shape, BF16)
        p_ref[...] = jnp.zeros(p_ref.shape, BF16)

    starts, biases = [], []
    for r in range(n_r):
        q0 = j * TQ_ATT + r * WINDOW
        start = pl.multiple_of(jnp.maximum(q0 - WINDOW, 0), WINDOW)
        starts.append(start)
        kpos = start + lax.broadcasted_iota(jnp.int32, (band, WINDOW), 0)
        qpos = q0 + lax.broadcasted_iota(jnp.int32, (band, WINDOW), 1)
        d = qpos - kpos
        biases.append(jnp.where((d >= 0) & (d < WINDOW), 0.0, NEG_INF))
        for g in range(N_KV_HEADS):
            vb = vt_ref[0, g * HEAD_DIM:(g + 1) * HEAD_DIM, pl.ds(start, band)]
            vm = vtmeta_ref[g * HEAD_DIM:(g + 1) * HEAD_DIM, :]
            for hh in range(2):
                col = (2 * g + hh) * LANES
                kcat_ref[r, g, hh * band:(hh + 1) * band, :] = kd_ref[0, pl.ds(start, band), col:col + LANES]
                kcat_ref[r, g, m0 + hh * N_META:m0 + (hh + 1) * N_META, :] = kmeta_ref[:, col:col + LANES]
                rows = slice(hh * HEAD_DIM, (hh + 1) * HEAD_DIM)
                vbd_ref[r, g, rows, hh * band:(hh + 1) * band] = vb
                vbd_ref[r, g, rows, m0 + hh * N_META:m0 + (hh + 1) * N_META] = vm

    for r in range(n_r):
        for c in range(n_c):
            qc = q_ref[0, r * WINDOW:(r + 1) * WINDOW, c * LANES:(c + 1) * LANES]
            s_ref[r * n_c + c] = lax.dot_general(kcat_ref[r, c // 2], qc, nt, preferred_element_type=F32)

    for r in range(n_r):
        for c in range(n_c):
            it = r * n_c + c
            for hh in range(2):
                sink = sink_ref[2 * c + hh]
                s = s_ref[it, hh * band:(hh + 1) * band, :] + biases[r]
                sm = s_ref[it, m0 + hh * N_META:m0 + (hh + 1) * N_META, :]
                mx = jnp.maximum(jnp.max(s, axis=0, keepdims=True), jnp.max(sm, axis=0, keepdims=True))
                mx = jnp.maximum(mx, sink)
                p = jnp.exp(s - mx)
                pm = jnp.exp(sm - mx)
                den = (jnp.sum(p, axis=0, keepdims=True) + jnp.sum(pm, axis=0, keepdims=True)
                       + jnp.exp(sink - mx))
                p_ref[it, hh * band:(hh + 1) * band, :] = p.astype(BF16)
                p_ref[it, m0 + hh * N_META:m0 + (hh + 1) * N_META, :] = pm.astype(BF16)
                inv_ref[it, hh * HEAD_DIM:(hh + 1) * HEAD_DIM, :] = jnp.broadcast_to(
                    1.0 / den, (HEAD_DIM, WINDOW))

    for r in range(n_r):
        for c in range(n_c):
            it = r * n_c + c
            o_t = jnp.dot(vbd_ref[r, c // 2], p_ref[it], preferred_element_type=F32) * inv_ref[it]
            yattn_ref[r * WINDOW:(r + 1) * WINDOW, c * LANES:(c + 1) * LANES] = (
                jnp.transpose(o_t).astype(BF16))

    mix = (jnp.dot(yp_ref[0], wout_ref[:POOL_WIDTH, :], preferred_element_type=F32)
           + jnp.dot(yattn_ref[...], wout_ref[POOL_WIDTH:, :], preferred_element_type=F32))
    h2 = x_ref[0] + mix
    h2_ref[0] = h2
    m = _rms_rows(h2, fgain_ref[...]).astype(BF16)
    m_ref[0] = m
    logits = jnp.dot(m, wr_ref[...], preferred_element_type=F32)
    gates, counts = _route(logits)
    gates_ref[0] = gates
    cnt_ref[0] = counts


def _moe_kernel(cnt_ref, m_ref, gates_ref, h2_ref, wg_ref, wu_ref, wd_ref, out_ref, ys_ref, hid_ref):
    i = pl.program_id(0)
    per = TM_MOE // TQ_ATT
    nsub, base = [], []
    total = 0
    for g in range(N_GROUPS):
        n_g = 0
        for s in range(per):
            n_g = n_g + cnt_ref[(i * per + s) * N_GROUPS + g]
        base.append(total * SUB)
        nsub.append(lax.div(n_g + (SUB - 1), SUB))
        total = total + nsub[-1]

    @pl.when(i == 0)
    def _():
        ys_ref[...] = jnp.zeros(ys_ref.shape, BF16)

    gates = gates_ref[...]
    lane = lax.broadcasted_iota(jnp.int32, gates.shape, 1).astype(F32)
    gid = gates[:, GID_LANE:GID_LANE + 1]
    onehot = jnp.where(lane == gid, 1.0, 0.0)
    tri = (lax.broadcasted_iota(jnp.int32, (TM_MOE, TM_MOE), 1)
           < lax.broadcasted_iota(jnp.int32, (TM_MOE, TM_MOE), 0))
    before = jnp.dot(jnp.where(tri, 1.0, 0.0).astype(BF16), onehot.astype(BF16),
                     preferred_element_type=F32)
    base_row = jnp.zeros((1, LANES), F32)
    for g in range(N_GROUPS):
        base_row = jnp.where(lane[:1] == g, _as_f32(base[g]), base_row)
    pos_col = jnp.sum(onehot * (before + base_row), axis=-1, keepdims=True)
    pos_row = jnp.transpose(jnp.broadcast_to(pos_col, (TM_MOE, LANES)))[0:1, :]

    g_hi = gates.astype(BF16)
    g_lo = (gates - g_hi.astype(F32)).astype(BF16)
    ghl = jnp.where(lane < N_EXPERTS, g_hi.astype(F32),
                    pltpu.roll(g_lo.astype(F32), LO_SHIFT, 1)).astype(BF16)

    for g in range(N_GROUPS):
        def body(c, carry, g=g):
            r0 = pl.multiple_of(base[g] + c * SUB, BF16_ROWS)
            rows = (r0 + lax.broadcasted_iota(jnp.int32, (SUB, 1), 0)).astype(F32)
            perm = jnp.where(pos_row == rows, 1.0, 0.0).astype(BF16)
            xs = jnp.dot(perm, m_ref[...], preferred_element_type=F32).astype(BF16)
            gs = jnp.dot(perm, ghl, preferred_element_type=F32)
            for jj in range(EXPERTS_PER_GROUP):
                e = g * EXPERTS_PER_GROUP + jj
                gt = jnp.dot(xs, wg_ref[e], preferred_element_type=F32)
                up = jnp.dot(xs, wu_ref[e], preferred_element_type=F32)
                gate = gs[:, e:e + 1] + gs[:, LO_SHIFT + e:LO_SHIFT + e + 1]
                hid = gt * (1.0 / (1.0 + jnp.exp(-gt))) * up * gate
                hid_ref[:, jj * D_EXPERT:(jj + 1) * D_EXPERT] = hid.astype(BF16)
            y = jnp.dot(hid_ref[...], wd_ref[g * D_GROUP:(g + 1) * D_GROUP, :],
                        preferred_element_type=F32)
            ys_ref[pl.ds(r0, SUB), :] = y.astype(BF16)
            return carry
        lax.fori_loop(0, nsub[g], body, 0)

    def scatter(n_rows):
        cols = lax.broadcasted_iota(jnp.int32, (1, n_rows), 1).astype(F32)
        perm_t = jnp.where(pos_col == cols, 1.0, 0.0).astype(BF16)
        out_ref[...] = h2_ref[...] + jnp.dot(perm_t, ys_ref[:n_rows, :], preferred_element_type=F32)

    short = total * SUB <= YS_SHORT

    @pl.when(short)
    def _():
        scatter(YS_SHORT)

    @pl.when(jnp.logical_not(short))
    def _():
        scatter(YS_ROWS)


def _as_f32(v):
    return float(v) if isinstance(v, int) else v.astype(F32)


def _const_spec(shape):
    n = len(shape)
    return pl.BlockSpec(shape, lambda *_: (0,) * n)


def _resident_spec(shape):
    n = len(shape)
    return pl.BlockSpec(shape, lambda *_: (0,) * n, pipeline_mode=pl.Buffered(1))


def kernel(x, meta_tokens, attn_norm_gain, w_in, w_pool, pool_scale, q_norm_gain, k_norm_gain,
           attn_sinks, w_out, ffn_norm_gain, w_group_router, w_expert_router, w_gate, w_up, w_down):
    B, S, D = x.shape
    assert D == D_MODEL and S % TQ_IN == 0 and S % TQ_ATT == 0 and (B * S) % TM_MOE == 0
    assert w_in.shape[0] == 1, "single layer"
    T = B * S

    rope = _rope_tables(N_META + S)
    fgain = ffn_norm_gain[0][None, :]
    qgain2 = jnp.tile(q_norm_gain[0], 2)[None, :]
    kgain2 = jnp.tile(k_norm_gain[0] * (HEAD_DIM ** 0.5), 2)[None, :]
    pscale = pool_scale[0][None, :]
    w_in_b = (attn_norm_gain[0][:, None] * w_in[0]).astype(BF16)
    w_pool_b = w_pool[0].astype(BF16)
    w_out_b = w_out[0].astype(BF16)
    w_r = jnp.concatenate(
        [w_expert_router[0], w_group_router[0],
         jnp.zeros((D, LANES - N_EXPERTS - N_GROUPS), F32)], axis=1).astype(BF16)
    wg = w_gate[0].astype(BF16)
    wu = w_up[0].astype(BF16)
    wd = w_down[0].astype(BF16).reshape(N_EXPERTS * D_EXPERT, D)
    params = pltpu.CompilerParams(vmem_limit_bytes=VMEM_LIMIT)

    u_meta, kd_meta, v_meta = pl.pallas_call(
        _meta_kernel,
        out_shape=(jax.ShapeDtypeStruct((N_META, POOL_WIDTH), F32),
                   jax.ShapeDtypeStruct((N_META, KV_EXP), BF16),
                   jax.ShapeDtypeStruct((N_META, KV_WIDTH), BF16)),
        compiler_params=params,
        name="meta_proj",
    )(meta_tokens, w_in_b, kgain2, rope[:N_META])
    vt_meta = v_meta.T

    yp, q, kd, vt = pl.pallas_call(
        _in_kernel,
        grid=(B, S // TQ_IN),
        in_specs=[
            pl.BlockSpec((1, TQ_IN, D), lambda b, j: (b, j, 0)),
            _const_spec((N_META, POOL_WIDTH)),
            _const_spec((D, IN_WIDTH)),
            _const_spec((len(POOL_WINDOWS), POOL_GROUP, POOL_GROUP)),
            _const_spec((1, POOL_WIDTH)),
            _const_spec((1, LANES)),
            _const_spec((1, LANES)),
            pl.BlockSpec((TQ_IN, 3 * LANES), lambda b, j: (j, 0)),
        ],
        out_specs=(
            pl.BlockSpec((1, TQ_IN, POOL_WIDTH), lambda b, j: (b, j, 0)),
            pl.BlockSpec((1, TQ_IN, ATTN_WIDTH), lambda b, j: (b, j, 0)),
            pl.BlockSpec((1, TQ_IN, KV_EXP), lambda b, j: (b, j, 0)),
            pl.BlockSpec((1, KV_WIDTH, TQ_IN), lambda b, j: (b, 0, j)),
        ),
        out_shape=(jax.ShapeDtypeStruct((B, S, POOL_WIDTH), BF16),
                   jax.ShapeDtypeStruct((B, S, ATTN_WIDTH), BF16),
                   jax.ShapeDtypeStruct((B, S, KV_EXP), BF16),
                   jax.ShapeDtypeStruct((B, KV_WIDTH, S), BF16)),
        scratch_shapes=[pltpu.VMEM((N_META, POOL_WIDTH), F32),
                        pltpu.VMEM((TQ_IN // RB_IN, RB_IN, IN_WIDTH), F32)],
        compiler_params=pltpu.CompilerParams(
            dimension_semantics=("arbitrary", "arbitrary"), vmem_limit_bytes=VMEM_LIMIT),
        name="in_proj",
    )(x, u_meta, w_in_b, w_pool_b, pscale, qgain2, kgain2, rope[N_META:])

    h2, m, gates, cnt = pl.pallas_call(
        _attn_kernel,
        grid=(B, S // TQ_ATT),
        in_specs=[
            pl.BlockSpec(memory_space=pltpu.SMEM),
            pl.BlockSpec((1, TQ_ATT, D), lambda b, j: (b, j, 0)),
            pl.BlockSpec((1, TQ_ATT, POOL_WIDTH), lambda b, j: (b, j, 0)),
            pl.BlockSpec((1, TQ_ATT, ATTN_WIDTH), lambda b, j: (b, j, 0)),
            pl.BlockSpec((1, S, KV_EXP), lambda b, j: (b, 0, 0)),
            pl.BlockSpec((1, KV_WIDTH, S), lambda b, j: (b, 0, 0)),
            _const_spec((N_META, KV_EXP)),
            _const_spec((KV_WIDTH, N_META)),
            _const_spec((D, D)),
            _const_spec((1, D)),
            _const_spec((D, LANES)),
        ],
        out_specs=(
            pl.BlockSpec((1, TQ_ATT, D), lambda b, j: (b, j, 0)),
            pl.BlockSpec((1, TQ_ATT, D), lambda b, j: (b, j, 0)),
            pl.BlockSpec((1, TQ_ATT, LANES), lambda b, j: (b, j, 0)),
            pl.BlockSpec((1, 1, LANES), lambda b, j: (b * (S // TQ_ATT) + j, 0, 0)),
        ),
        out_shape=(jax.ShapeDtypeStruct((B, S, D), F32),
                   jax.ShapeDtypeStruct((B, S, D), BF16),
                   jax.ShapeDtypeStruct((B, S, LANES), F32),
                   jax.ShapeDtypeStruct((T // TQ_ATT, 1, LANES), jnp.int32)),
        scratch_shapes=[
            pltpu.VMEM((TQ_ATT, ATTN_WIDTH), BF16),
            pltpu.VMEM((TQ_ATT // WINDOW, N_KV_HEADS, KEY_ROWS, LANES), BF16),
            pltpu.VMEM((TQ_ATT // WINDOW, N_KV_HEADS, LANES, KEY_PAD), BF16),
            pltpu.VMEM((N_ITEMS, KEY_ROWS, LANES), F32),
            pltpu.VMEM((N_ITEMS, KEY_PAD, LANES), BF16),
            pltpu.VMEM((N_ITEMS, LANES, LANES), F32),
        ],
        compiler_params=pltpu.CompilerParams(
            dimension_semantics=("arbitrary", "arbitrary"), vmem_limit_bytes=VMEM_LIMIT),
        name="attn_out",
    )(attn_sinks[0], x, yp, q, kd, vt, kd_meta, vt_meta, w_out_b, fgain, w_r)

    cnt_flat = cnt[:, 0, :N_GROUPS].reshape(-1)
    out = pl.pallas_call(
        _moe_kernel,
        grid_spec=pltpu.PrefetchScalarGridSpec(
            num_scalar_prefetch=1,
            grid=(T // TM_MOE,),
            in_specs=[
                pl.BlockSpec((TM_MOE, D), lambda i, c: (i, 0)),
                pl.BlockSpec((TM_MOE, LANES), lambda i, c: (i, 0)),
                pl.BlockSpec((TM_MOE, D), lambda i, c: (i, 0)),
                _resident_spec((N_EXPERTS, D, D_EXPERT)),
                _resident_spec((N_EXPERTS, D, D_EXPERT)),
                _resident_spec((N_EXPERTS * D_EXPERT, D)),
            ],
            out_specs=pl.BlockSpec((TM_MOE, D), lambda i, c: (i, 0)),
            scratch_shapes=[pltpu.VMEM((YS_ROWS, D), BF16),
                            pltpu.VMEM((SUB, D_GROUP), BF16)],
        ),
        out_shape=jax.ShapeDtypeStruct((T, D), F32),
        compiler_params=pltpu.CompilerParams(
            dimension_semantics=("arbitrary",), vmem_limit_bytes=VMEM_LIMIT),
        name="moe",
    )(cnt_flat, m.reshape(T, D), gates.reshape(T, LANES), h2.reshape(T, D), wg, wu, wd)
    return out.reshape(B, S, D)
```

```python
import functools

import numpy as np
import jax
import jax.numpy as jnp
from jax import lax
from jax.experimental import pallas as pl
from jax.experimental.pallas import tpu as pltpu

D_MODEL = 1024
N_META = 16
POOL_WIDTH = 512
POOL_WINDOWS = (2, 4, 8, 16)
POOL_GROUP = 128
HEAD_DIM = 64
N_HEADS = 8
N_KV_HEADS = 2
ATTN_WIDTH = N_HEADS * HEAD_DIM
KV_WIDTH = N_KV_HEADS * HEAD_DIM
WINDOW = 128
ROT_DIM = HEAD_DIM // 4
ROPE_THETA = 500000.0
IN_WIDTH = POOL_WIDTH + ATTN_WIDTH + 2 * KV_WIDTH
N_GROUPS = 4
EXPERTS_PER_GROUP = 4
N_EXPERTS = 16
D_EXPERT = 256
EPS = 1e-6
NEG_INF = -1e30
LOG2_E = 1.4426950408889634

LANES = 128
SUBLANES = 8
KV_EXP = 4 * LANES
TQ_IN = 1024
RB_IN = 128
TQ_ATT = 512
KEY_ROWS = 2 * (2 * WINDOW + N_META)
KEY_PAD = -(-KEY_ROWS // LANES) * LANES
OUT_BLOCKS = 2
ROUTE_ROWS = OUT_BLOCKS * WINDOW
N_ITEMS = (TQ_ATT // WINDOW) * N_KV_HEADS
TM_MOE = 512
SUB = 144
BF16_ROWS = 16
MXU_DEPTH = 256
YS_ROWS = -(-((TM_MOE + N_GROUPS * (SUB - 1)) // SUB * SUB) // MXU_DEPTH) * MXU_DEPTH
YS_SHORT = YS_ROWS - MXU_DEPTH
GID_LANE = N_EXPERTS
LO_SHIFT = 32
D_GROUP = EXPERTS_PER_GROUP * D_EXPERT
VMEM_LIMIT = 56 * 1024 * 1024

BF16 = jnp.bfloat16
F32 = jnp.float32


def _rope_tables(n_pos):
    half = ROT_DIM // 2
    inv_freq = 1.0 / (ROPE_THETA ** (np.arange(half, dtype=np.float64) / half))
    ang = np.arange(n_pos, dtype=np.float64)[:, None] * inv_freq[None, :]
    cos, sin = np.cos(ang), np.sin(ang)
    c = np.ones((n_pos, HEAD_DIM)); c[:, :half] = cos; c[:, half:ROT_DIM] = cos
    sa = np.zeros((n_pos, HEAD_DIM)); sa[:, :half] = -sin
    sb = np.zeros((n_pos, HEAD_DIM)); sb[:, half:ROT_DIM] = sin
    tab = np.concatenate([np.tile(c, (1, 2)), np.tile(sa, (1, 2)), np.tile(sb, (1, 2))], axis=1)
    return jnp.asarray(tab, dtype=F32)


def _rms_unit(x):
    ms = jnp.mean(x * x, axis=-1, keepdims=True)
    return x * lax.rsqrt(ms + EPS)


def _rms_rows(x, gain):
    return _rms_unit(x) * gain


def _head_norm_rope_many(xs, gains, rope):
    lo = lax.broadcasted_iota(jnp.int32, xs[0].shape, 1) < HEAD_DIM
    sums = []
    for x in xs:
        sq = x * x
        sums.append((jnp.sum(jnp.where(lo, sq, 0.0), axis=-1, keepdims=True),
                     jnp.sum(jnp.where(lo, 0.0, sq), axis=-1, keepdims=True)))
    ys = [x * lax.rsqrt(jnp.where(lo, s_lo, s_hi) + HEAD_DIM * EPS) * g
          for x, g, (s_lo, s_hi) in zip(xs, gains, sums)]
    half = ROT_DIM // 2
    rolled = [(pltpu.roll(y, LANES - half, 1), pltpu.roll(y, half, 1)) for y in ys]
    c, sa, sb = rope[:, 0:LANES], rope[:, LANES:2 * LANES], rope[:, 2 * LANES:3 * LANES]
    return [y * c + ra * sa + rb * sb for y, (ra, rb) in zip(ys, rolled)]


def _head_norm_rope(xc, gain2, rope):
    return _head_norm_rope_many([xc], [gain2], rope)[0]


def _expand_kv(t):
    lo = lax.broadcasted_iota(jnp.int32, t.shape, 1) < HEAD_DIM
    sw = pltpu.roll(t, HEAD_DIM, 1)
    z = jnp.zeros_like(t)
    return jnp.concatenate([jnp.where(lo, t, z), jnp.where(lo, z, sw),
                            jnp.where(lo, sw, z), jnp.where(lo, z, t)], axis=1)


def _project(x, w_in):
    return jnp.dot(_rms_unit(x).astype(BF16), w_in, preferred_element_type=F32)


def _meta_kernel(meta_ref, win_ref, kgain_ref, rope_ref, u_ref, kd_ref, v_ref):
    proj = _project(meta_ref[...], win_ref[...])
    u_ref[...] = proj[:, :POOL_WIDTH]
    k = proj[:, POOL_WIDTH + ATTN_WIDTH:POOL_WIDTH + ATTN_WIDTH + KV_WIDTH]
    v = proj[:, POOL_WIDTH + ATTN_WIDTH + KV_WIDTH:]
    k = _head_norm_rope(k, kgain_ref[...], rope_ref[...])
    kd_ref[...] = _expand_kv(k).astype(BF16)
    v_ref[...] = v.astype(BF16)


def _in_kernel(x_ref, umeta_ref, win_ref, wpool_ref, pscale_ref, qgain_ref, kgain_ref,
               rope_ref, yp_ref, q_ref, kd_ref, vt_ref, carry_ref, proj_ref):
    j = pl.program_id(1)

    @pl.when(j == 0)
    def _():
        carry_ref[...] = umeta_ref[...]

    n_sub = TQ_IN // RB_IN

    def normalise(i):
        return _rms_unit(x_ref[0, i * RB_IN:(i + 1) * RB_IN, :]).astype(BF16)

    def project(i, a):
        proj_ref[i] = jnp.dot(a, win_ref[...], preferred_element_type=F32)

    def finish(i):
        rows = slice(i * RB_IN, (i + 1) * RB_IN)
        rope = rope_ref[rows, :]
        u = proj_ref[i, :, :POOL_WIDTH]
        acc = jnp.concatenate([carry_ref[...], u], axis=0)
        for gi, w in enumerate(POOL_WINDOWS):
            lo = gi * POOL_GROUP
            acc = acc[:, POOL_GROUP * (1 if gi else 0):]
            acc = acc + pltpu.roll(acc, w // 2, 0)
            mixed = acc[N_META:, :POOL_GROUP] * (1.0 / w) - u[:, lo:lo + POOL_GROUP]
            y = jnp.dot(mixed.astype(BF16), wpool_ref[gi], preferred_element_type=F32)
            yp_ref[0, rows, lo:lo + POOL_GROUP] = (y * pscale_ref[:, lo:lo + POOL_GROUP]).astype(BF16)
        carry_ref[...] = u[RB_IN - N_META:, :]

        n_qc = ATTN_WIDTH // LANES
        xs = [proj_ref[i, :, POOL_WIDTH + c * LANES:POOL_WIDTH + (c + 1) * LANES] for c in range(n_qc + 1)]
        outs = _head_norm_rope_many(xs, [qgain_ref[...]] * n_qc + [kgain_ref[...]], rope)
        for c in range(n_qc):
            q_ref[0, rows, c * LANES:(c + 1) * LANES] = outs[c].astype(BF16)
        kd_ref[0, rows, :] = _expand_kv(outs[n_qc]).astype(BF16)
        v = proj_ref[i, :, POOL_WIDTH + ATTN_WIDTH + KV_WIDTH:]
        vt_ref[0, :, rows] = jnp.transpose(v).astype(BF16)

    a_next = normalise(0)
    for i in range(n_sub):
        project(i, a_next)
        if i + 1 < n_sub:
            a_next = normalise(i + 1)
        if i > 0:
            finish(i - 1)
    finish(n_sub - 1)


def _route(logits):
    n = logits.shape[0]
    lt = jnp.transpose(logits)
    sub = SUBLANES
    row8 = lax.broadcasted_iota(jnp.int32, (sub, n), 0).astype(F32)
    row16 = lax.broadcasted_iota(jnp.int32, (N_EXPERTS, n), 0).astype(F32)
    g_ok = row8 < N_GROUPS
    gl = jnp.where(g_ok, lt[N_EXPERTS:N_EXPERTS + sub, :], NEG_INF)
    gmax = jnp.max(gl, axis=0, keepdims=True)
    gsum = jnp.sum(jnp.where(g_ok, jnp.exp(gl - gmax), 0.0), axis=0, keepdims=True)
    g_prob = 1.0 / gsum
    g_idx = jnp.min(jnp.where(gl == gmax, row8, float(sub)), axis=0, keepdims=True)
    e_lo = g_idx * EXPERTS_PER_GROUP
    emask = (row16 >= e_lo) & (row16 < e_lo + EXPERTS_PER_GROUP)
    el = jnp.where(emask, lt[:N_EXPERTS, :], NEG_INF)
    big = float(N_EXPERTS)
    e1 = jnp.max(el, axis=0, keepdims=True)
    i1 = jnp.min(jnp.where(el == e1, row16, big), axis=0, keepdims=True)
    el2 = jnp.where(row16 == i1, NEG_INF, el)
    e2 = jnp.max(el2, axis=0, keepdims=True)
    i2 = jnp.min(jnp.where(el2 == e2, row16, big), axis=0, keepdims=True)
    t = jnp.exp(e2 - e1)
    w1 = 1.0 / (1.0 + t)
    w2 = t * w1
    gates_t = jnp.where(row16 == i1, w1 * g_prob, 0.0) + jnp.where(row16 == i2, w2 * g_prob, 0.0)
    full_t = jnp.concatenate(
        [gates_t, jnp.broadcast_to(g_idx, (sub, n)),
         jnp.zeros((LANES - N_EXPERTS - sub, n), F32)], axis=0)
    counts = jnp.sum(jnp.where(row8 == g_idx, 1.0, 0.0), axis=1, keepdims=True)
    return jnp.transpose(full_t), jnp.broadcast_to(counts, (sub, LANES)).astype(jnp.int32)


def _attn_kernel(sink_ref, x_ref, yp_ref, q_ref, kd_ref, vt_ref, kmeta_ref, vtmeta_ref, wout_ref,
                 fgain_ref, wr_ref, h2_ref, m_ref, gates_ref, cnt_ref,
                 yattn_ref, kcat_ref, vbd_ref, s_ref, p_ref, inv_ref, logit_ref):
    j = pl.program_id(1)
    nt = (((1,), (1,)), ((), ()))
    n_r = TQ_ATT // WINDOW
    n_c = ATTN_WIDTH // LANES
    band = 2 * WINDOW
    m0 = 2 * band

    @pl.when((pl.program_id(0) == 0) & (j == 0))
    def _():
        vbd_ref[...] = jnp.zeros(vbd_ref.shape, BF16)
        p_ref[...] = jnp.zeros(p_ref.shape, BF16)

    def window_start(r):
        return pl.multiple_of(jnp.maximum(j * TQ_ATT + (r - 1) * WINDOW, 0), WINDOW)

    def stage_scores(r):
        start = window_start(r)
        for g in range(N_KV_HEADS):
            vb = vt_ref[0, g * HEAD_DIM:(g + 1) * HEAD_DIM, pl.ds(start, band)]
            vm = vtmeta_ref[g * HEAD_DIM:(g + 1) * HEAD_DIM, :]
            for hh in range(2):
                col = (2 * g + hh) * LANES
                kcat_ref[r, g, hh * band:(hh + 1) * band, :] = kd_ref[0, pl.ds(start, band), col:col + LANES]
                kcat_ref[r, g, m0 + hh * N_META:m0 + (hh + 1) * N_META, :] = kmeta_ref[:, col:col + LANES]
                rows = slice(hh * HEAD_DIM, (hh + 1) * HEAD_DIM)
                vbd_ref[r, g, rows, hh * band:(hh + 1) * band] = vb
                vbd_ref[r, g, rows, m0 + hh * N_META:m0 + (hh + 1) * N_META] = vm
            rq = slice(r * WINDOW, (r + 1) * WINDOW)
            qq = jnp.concatenate([q_ref[0, rq, (2 * g + ch) * LANES:(2 * g + ch + 1) * LANES]
                                  for ch in range(2)], axis=0)
            s_ref[r * N_KV_HEADS + g] = lax.dot_general(kcat_ref[r, g], qq, nt, preferred_element_type=F32)

    def stage_softmax(r):
        kpos = window_start(r) + lax.broadcasted_iota(jnp.int32, (band, WINDOW), 0)
        qpos = j * TQ_ATT + r * WINDOW + lax.broadcasted_iota(jnp.int32, (band, WINDOW), 1)
        d = qpos - kpos
        bias = jnp.where((d >= 0) & (d < WINDOW), 0.0, NEG_INF)
        bias = jnp.concatenate([bias, bias], axis=1)
        left = lax.broadcasted_iota(jnp.int32, (1, 2 * WINDOW), 1) < WINDOW
        for g in range(N_KV_HEADS):
            it = r * N_KV_HEADS + g
            for hh in range(2):
                sink = jnp.where(left, sink_ref[4 * g + hh], sink_ref[4 * g + 2 + hh])
                s = s_ref[it, hh * band:(hh + 1) * band, :] + bias
                sm = s_ref[it, m0 + hh * N_META:m0 + (hh + 1) * N_META, :]
                mx = jnp.maximum(jnp.max(s, axis=0, keepdims=True), jnp.max(sm, axis=0, keepdims=True))
                mx = jnp.maximum(mx, sink)
                p = jnp.exp2(s - mx)
                pm = jnp.exp2(sm - mx)
                den = (jnp.sum(p, axis=0, keepdims=True) + jnp.sum(pm, axis=0, keepdims=True)
                       + jnp.exp2(sink - mx))
                p_ref[it, hh * band:(hh + 1) * band, :] = p.astype(BF16)
                p_ref[it, m0 + hh * N_META:m0 + (hh + 1) * N_META, :] = pm.astype(BF16)
                inv_ref[it, hh * HEAD_DIM:(hh + 1) * HEAD_DIM, :] = jnp.broadcast_to(
                    1.0 / den, (HEAD_DIM, 2 * WINDOW))

    def stage_values(r):
        for g in range(N_KV_HEADS):
            it = r * N_KV_HEADS + g
            o_t = jnp.dot(vbd_ref[r, g], p_ref[it], preferred_element_type=F32) * inv_ref[it]
            for ch in range(2):
                c = 2 * g + ch
                yattn_ref[r * WINDOW:(r + 1) * WINDOW, c * LANES:(c + 1) * LANES] = (
                    jnp.transpose(o_t[:, ch * WINDOW:(ch + 1) * WINDOW]).astype(BF16))

    def stage_out(r):
        if (r + 1) % OUT_BLOCKS:
            return
        rows = slice((r + 1 - OUT_BLOCKS) * WINDOW, (r + 1) * WINDOW)
        mix = (jnp.dot(yp_ref[0, rows, :], wout_ref[:POOL_WIDTH, :], preferred_element_type=F32)
               + jnp.dot(yattn_ref[rows, :], wout_ref[POOL_WIDTH:, :], preferred_element_type=F32))
        h2 = x_ref[0, rows, :] + mix
        h2_ref[0, rows, :] = h2
        m = _rms_rows(h2, fgain_ref[...]).astype(BF16)
        m_ref[0, rows, :] = m
        logit_ref[r // OUT_BLOCKS] = jnp.dot(m, wr_ref[...], preferred_element_type=F32)

    def stage_route(r):
        if (r + 1) % OUT_BLOCKS:
            return
        gates, counts = _route(logit_ref[r // OUT_BLOCKS])
        gates_ref[0, (r + 1 - OUT_BLOCKS) * WINDOW:(r + 1) * WINDOW, :] = gates
        cnt_ref[r // OUT_BLOCKS] = counts

    stages = (stage_scores, stage_softmax, stage_values, stage_out, stage_route)
    for t in range(n_r + len(stages) - 1):
        for k, stage in enumerate(stages):
            if 0 <= t - k < n_r:
                stage(t - k)


def _moe_kernel(cnt_ref, m_ref, gates_ref, h2_ref, wg_ref, wu_ref, wd_ref, out_ref, ys_ref, hid_ref):
    i = pl.program_id(0)
    per = TM_MOE // ROUTE_ROWS
    nsub, base = [], []
    total = 0
    for g in range(N_GROUPS):
        n_g = 0
        for s in range(per):
            n_g = n_g + cnt_ref[(i * per + s) * N_GROUPS + g]
        base.append(total * SUB)
        nsub.append(lax.div(n_g + (SUB - 1), SUB))
        total = total + nsub[-1]

    @pl.when(i == 0)
    def _():
        ys_ref[...] = jnp.zeros(ys_ref.shape, BF16)

    gates = gates_ref[...]
    lane = lax.broadcasted_iota(jnp.int32, gates.shape, 1).astype(F32)
    gid = gates[:, GID_LANE:GID_LANE + 1]
    onehot = jnp.where(lane == gid, 1.0, 0.0)
    tri = (lax.broadcasted_iota(jnp.int32, (TM_MOE, TM_MOE), 1)
           < lax.broadcasted_iota(jnp.int32, (TM_MOE, TM_MOE), 0))
    before = jnp.dot(jnp.where(tri, 1.0, 0.0).astype(BF16), onehot.astype(BF16),
                     preferred_element_type=F32)
    base_row = jnp.zeros((1, LANES), F32)
    for g in range(N_GROUPS):
        base_row = jnp.where(lane[:1] == g, _as_f32(base[g]), base_row)
    pos_col = jnp.sum(onehot * (before + base_row), axis=-1, keepdims=True)
    pos_row = jnp.transpose(jnp.broadcast_to(pos_col, (TM_MOE, LANES)))[0:1, :]

    g_hi = gates.astype(BF16)
    g_lo = (gates - g_hi.astype(F32)).astype(BF16)
    ghl = jnp.where(lane < N_EXPERTS, g_hi.astype(F32),
                    pltpu.roll(g_lo.astype(F32), LO_SHIFT, 1)).astype(BF16)

    for g in range(N_GROUPS):
        def body(c, carry, g=g):
            r0 = pl.multiple_of(base[g] + c * SUB, BF16_ROWS)
            rows = (r0 + lax.broadcasted_iota(jnp.int32, (SUB, 1), 0)).astype(F32)
            perm = jnp.where(pos_row == rows, 1.0, 0.0).astype(BF16)
            xs = jnp.dot(perm, m_ref[...], preferred_element_type=F32).astype(BF16)
            gs = jnp.dot(perm, ghl, preferred_element_type=F32)
            for jj in range(EXPERTS_PER_GROUP):
                e = g * EXPERTS_PER_GROUP + jj
                gt = jnp.dot(xs, wg_ref[e], preferred_element_type=F32)
                up = jnp.dot(xs, wu_ref[e], preferred_element_type=F32)
                gate = gs[:, e:e + 1] + gs[:, LO_SHIFT + e:LO_SHIFT + e + 1]
                hid = gt * (1.0 / (1.0 + jnp.exp(-gt))) * up * gate
                hid_ref[:, jj * D_EXPERT:(jj + 1) * D_EXPERT] = hid.astype(BF16)
            y = jnp.dot(hid_ref[...], wd_ref[g * D_GROUP:(g + 1) * D_GROUP, :],
                        preferred_element_type=F32)
            ys_ref[pl.ds(r0, SUB), :] = y.astype(BF16)
            return carry
        lax.fori_loop(0, nsub[g], body, 0)

    def scatter(n_rows):
        cols = lax.broadcasted_iota(jnp.int32, (1, n_rows), 1).astype(F32)
        perm_t = jnp.where(pos_col == cols, 1.0, 0.0).astype(BF16)
        out_ref[...] = h2_ref[...] + jnp.dot(perm_t, ys_ref[:n_rows, :], preferred_element_type=F32)

    short = total * SUB <= YS_SHORT

    @pl.when(short)
    def _():
        scatter(YS_SHORT)

    @pl.when(jnp.logical_not(short))
    def _():
        scatter(YS_ROWS)


def _as_f32(v):
    return float(v) if isinstance(v, int) else v.astype(F32)


def _const_spec(shape):
    n = len(shape)
    return pl.BlockSpec(shape, lambda *_: (0,) * n)


def _resident_spec(shape):
    n = len(shape)
    return pl.BlockSpec(shape, lambda *_: (0,) * n, pipeline_mode=pl.Buffered(1))


def kernel(x, meta_tokens, attn_norm_gain, w_in, w_pool, pool_scale, q_norm_gain, k_norm_gain,
           attn_sinks, w_out, ffn_norm_gain, w_group_router, w_expert_router, w_gate, w_up, w_down):
    B, S, D = x.shape
    assert D == D_MODEL and S % TQ_IN == 0 and S % TQ_ATT == 0 and (B * S) % TM_MOE == 0
    assert w_in.shape[0] == 1, "single layer"
    T = B * S

    rope = _rope_tables(N_META + S)
    fgain = ffn_norm_gain[0][None, :]
    qgain2 = jnp.tile(q_norm_gain[0] * LOG2_E, 2)[None, :]
    kgain2 = jnp.tile(k_norm_gain[0] * (HEAD_DIM ** 0.5), 2)[None, :]
    pscale = pool_scale[0][None, :]
    w_in_b = (attn_norm_gain[0][:, None] * w_in[0]).astype(BF16)
    w_pool_b = w_pool[0].astype(BF16)
    w_out_b = w_out[0].astype(BF16)
    w_r = jnp.concatenate(
        [w_expert_router[0], w_group_router[0],
         jnp.zeros((D, LANES - N_EXPERTS - N_GROUPS), F32)], axis=1).astype(BF16)
    wg = w_gate[0].astype(BF16)
    wu = w_up[0].astype(BF16)
    wd = w_down[0].astype(BF16).reshape(N_EXPERTS * D_EXPERT, D)
    params = pltpu.CompilerParams(vmem_limit_bytes=VMEM_LIMIT)

    u_meta, kd_meta, v_meta = pl.pallas_call(
        _meta_kernel,
        out_shape=(jax.ShapeDtypeStruct((N_META, POOL_WIDTH), F32),
                   jax.ShapeDtypeStruct((N_META, KV_EXP), BF16),
                   jax.ShapeDtypeStruct((N_META, KV_WIDTH), BF16)),
        compiler_params=params,
        name="meta_proj",
    )(meta_tokens, w_in_b, kgain2, rope[:N_META])
    vt_meta = v_meta.T

    yp, q, kd, vt = pl.pallas_call(
        _in_kernel,
        grid=(B, S // TQ_IN),
        in_specs=[
            pl.BlockSpec((1, TQ_IN, D), lambda b, j: (b, j, 0)),
            _const_spec((N_META, POOL_WIDTH)),
            _const_spec((D, IN_WIDTH)),
            _const_spec((len(POOL_WINDOWS), POOL_GROUP, POOL_GROUP)),
            _const_spec((1, POOL_WIDTH)),
            _const_spec((1, LANES)),
            _const_spec((1, LANES)),
            pl.BlockSpec((TQ_IN, 3 * LANES), lambda b, j: (j, 0)),
        ],
        out_specs=(
            pl.BlockSpec((1, TQ_IN, POOL_WIDTH), lambda b, j: (b, j, 0)),
            pl.BlockSpec((1, TQ_IN, ATTN_WIDTH), lambda b, j: (b, j, 0)),
            pl.BlockSpec((1, TQ_IN, KV_EXP), lambda b, j: (b, j, 0)),
            pl.BlockSpec((1, KV_WIDTH, TQ_IN), lambda b, j: (b, 0, j)),
        ),
        out_shape=(jax.ShapeDtypeStruct((B, S, POOL_WIDTH), BF16),
                   jax.ShapeDtypeStruct((B, S, ATTN_WIDTH), BF16),
                   jax.ShapeDtypeStruct((B, S, KV_EXP), BF16),
                   jax.ShapeDtypeStruct((B, KV_WIDTH, S), BF16)),
        scratch_shapes=[pltpu.VMEM((N_META, POOL_WIDTH), F32),
                        pltpu.VMEM((TQ_IN // RB_IN, RB_IN, IN_WIDTH), F32)],
        compiler_params=pltpu.CompilerParams(
            dimension_semantics=("arbitrary", "arbitrary"), vmem_limit_bytes=VMEM_LIMIT),
        name="in_proj",
    )(x, u_meta, w_in_b, w_pool_b, pscale, qgain2, kgain2, rope[N_META:])

    h2, m, gates, cnt = pl.pallas_call(
        _attn_kernel,
        grid=(B, S // TQ_ATT),
        in_specs=[
            pl.BlockSpec(memory_space=pltpu.SMEM),
            pl.BlockSpec((1, TQ_ATT, D), lambda b, j: (b, j, 0)),
            pl.BlockSpec((1, TQ_ATT, POOL_WIDTH), lambda b, j: (b, j, 0)),
            pl.BlockSpec((1, TQ_ATT, ATTN_WIDTH), lambda b, j: (b, j, 0)),
            pl.BlockSpec((1, S, KV_EXP), lambda b, j: (b, 0, 0)),
            pl.BlockSpec((1, KV_WIDTH, S), lambda b, j: (b, 0, 0)),
            _const_spec((N_META, KV_EXP)),
            _const_spec((KV_WIDTH, N_META)),
            _const_spec((D, D)),
            _const_spec((1, D)),
            _const_spec((D, LANES)),
        ],
        out_specs=(
            pl.BlockSpec((1, TQ_ATT, D), lambda b, j: (b, j, 0)),
            pl.BlockSpec((1, TQ_ATT, D), lambda b, j: (b, j, 0)),
            pl.BlockSpec((1, TQ_ATT, LANES), lambda b, j: (b, j, 0)),
            pl.BlockSpec((TQ_ATT // ROUTE_ROWS, SUBLANES, LANES), lambda b, j: (b * (S // TQ_ATT) + j, 0, 0)),
        ),
        out_shape=(jax.ShapeDtypeStruct((B, S, D), F32),
                   jax.ShapeDtypeStruct((B, S, D), BF16),
                   jax.ShapeDtypeStruct((B, S, LANES), F32),
                   jax.ShapeDtypeStruct((T // ROUTE_ROWS, SUBLANES, LANES), jnp.int32)),
        scratch_shapes=[
            pltpu.VMEM((TQ_ATT, ATTN_WIDTH), BF16),
            pltpu.VMEM((TQ_ATT // WINDOW, N_KV_HEADS, KEY_ROWS, LANES), BF16),
            pltpu.VMEM((TQ_ATT // WINDOW, N_KV_HEADS, LANES, KEY_PAD), BF16),
            pltpu.VMEM((N_ITEMS, KEY_ROWS, 2 * LANES), F32),
            pltpu.VMEM((N_ITEMS, KEY_PAD, 2 * LANES), BF16),
            pltpu.VMEM((N_ITEMS, LANES, 2 * LANES), F32),
            pltpu.VMEM((TQ_ATT // ROUTE_ROWS, ROUTE_ROWS, LANES), F32),
        ],
        compiler_params=pltpu.CompilerParams(
            dimension_semantics=("arbitrary", "arbitrary"), vmem_limit_bytes=VMEM_LIMIT),
        name="attn_out",
    )(attn_sinks[0] * LOG2_E, x, yp, q, kd, vt, kd_meta, vt_meta, w_out_b, fgain, w_r)

    cnt_flat = cnt[:, :N_GROUPS, 0].reshape(-1)
    out = pl.pallas_call(
        _moe_kernel,
        grid_spec=pltpu.PrefetchScalarGridSpec(
            num_scalar_prefetch=1,
            grid=(T // TM_MOE,),
            in_specs=[
                pl.BlockSpec((TM_MOE, D), lambda i, c: (i, 0)),
                pl.BlockSpec((TM_MOE, LANES), lambda i, c: (i, 0)),
                pl.BlockSpec((TM_MOE, D), lambda i, c: (i, 0)),
                _resident_spec((N_EXPERTS, D, D_EXPERT)),
                _resident_spec((N_EXPERTS, D, D_EXPERT)),
                _resident_spec((N_EXPERTS * D_EXPERT, D)),
            ],
            out_specs=pl.BlockSpec((TM_MOE, D), lambda i, c: (i, 0)),
            scratch_shapes=[pltpu.VMEM((YS_ROWS, D), BF16),
                            pltpu.VMEM((SUB, D_GROUP), BF16)],
        ),
        out_shape=jax.ShapeDtypeStruct((T, D), F32),
        compiler_params=pltpu.CompilerParams(
            dimension_semantics=("arbitrary",), vmem_limit_bytes=VMEM_LIMIT),
        name="moe",
    )(cnt_flat, m.reshape(T, D), gates.reshape(T, LANES), h2.reshape(T, D), wg, wu, wd)
    return out.reshape(B, S, D)
```

```python
import functools

import numpy as np
import jax
import jax.numpy as jnp
from jax import lax
from jax.experimental import pallas as pl
from jax.experimental.pallas import tpu as pltpu

D_MODEL = 1024
N_META = 16
POOL_WIDTH = 512
POOL_WINDOWS = (2, 4, 8, 16)
POOL_GROUP = 128
HEAD_DIM = 64
N_HEADS = 8
N_KV_HEADS = 2
ATTN_WIDTH = N_HEADS * HEAD_DIM
KV_WIDTH = N_KV_HEADS * HEAD_DIM
WINDOW = 128
ROT_DIM = HEAD_DIM // 4
ROPE_THETA = 500000.0
IN_WIDTH = POOL_WIDTH + ATTN_WIDTH + 2 * KV_WIDTH
N_GROUPS = 4
EXPERTS_PER_GROUP = 4
N_EXPERTS = 16
D_EXPERT = 256
EPS = 1e-6
NEG_INF = -1e30
LOG2_E = 1.4426950408889634

LANES = 128
SUBLANES = 8
KV_EXP = 4 * LANES
TQ_IN = 1024
RB_IN = 128
TQ_ATT = 512
KEY_ROWS = 2 * (2 * WINDOW + N_META)
KEY_PAD = -(-KEY_ROWS // LANES) * LANES
OUT_BLOCKS = 2
ROUTE_ROWS = OUT_BLOCKS * WINDOW
N_ITEMS = (TQ_ATT // WINDOW) * N_KV_HEADS
TM_MOE = 512
SUB = 144
BF16_ROWS = 16
MXU_DEPTH = 256
YS_ROWS = -(-((TM_MOE + N_GROUPS * (SUB - 1)) // SUB * SUB) // MXU_DEPTH) * MXU_DEPTH
YS_SHORT = YS_ROWS - MXU_DEPTH
GID_LANE = N_EXPERTS
RANK_LANE = 24
LO_SHIFT = 32
D_GROUP = EXPERTS_PER_GROUP * D_EXPERT
VMEM_LIMIT = 56 * 1024 * 1024

BF16 = jnp.bfloat16
F32 = jnp.float32


def _rope_tables(n_pos):
    half = ROT_DIM // 2
    inv_freq = 1.0 / (ROPE_THETA ** (np.arange(half, dtype=np.float64) / half))
    ang = np.arange(n_pos, dtype=np.float64)[:, None] * inv_freq[None, :]
    cos, sin = np.cos(ang), np.sin(ang)
    c = np.ones((n_pos, HEAD_DIM)); c[:, :half] = cos; c[:, half:ROT_DIM] = cos
    sa = np.zeros((n_pos, HEAD_DIM)); sa[:, :half] = -sin
    sb = np.zeros((n_pos, HEAD_DIM)); sb[:, half:ROT_DIM] = sin
    tab = np.concatenate([np.tile(c, (1, 2)), np.tile(sa, (1, 2)), np.tile(sb, (1, 2))], axis=1)
    return jnp.asarray(tab, dtype=F32)


def _rms_unit(x):
    ms = jnp.mean(x * x, axis=-1, keepdims=True)
    return x * lax.rsqrt(ms + EPS)


def _rms_rows(x, gain):
    return _rms_unit(x) * gain


def _head_norm_rope_many(xs, gains, rope):
    lo = lax.broadcasted_iota(jnp.int32, xs[0].shape, 1) < HEAD_DIM
    sums = []
    for x in xs:
        sq = x * x
        sums.append((jnp.sum(jnp.where(lo, sq, 0.0), axis=-1, keepdims=True),
                     jnp.sum(jnp.where(lo, 0.0, sq), axis=-1, keepdims=True)))
    ys = [x * lax.rsqrt(jnp.where(lo, s_lo, s_hi) + HEAD_DIM * EPS) * g
          for x, g, (s_lo, s_hi) in zip(xs, gains, sums)]
    half = ROT_DIM // 2
    rolled = [(pltpu.roll(y, LANES - half, 1), pltpu.roll(y, half, 1)) for y in ys]
    c, sa, sb = rope[:, 0:LANES], rope[:, LANES:2 * LANES], rope[:, 2 * LANES:3 * LANES]
    return [y * c + ra * sa + rb * sb for y, (ra, rb) in zip(ys, rolled)]


def _head_norm_rope(xc, gain2, rope):
    return _head_norm_rope_many([xc], [gain2], rope)[0]


def _expand_kv(t):
    lo = lax.broadcasted_iota(jnp.int32, t.shape, 1) < HEAD_DIM
    sw = pltpu.roll(t, HEAD_DIM, 1)
    z = jnp.zeros_like(t)
    return jnp.concatenate([jnp.where(lo, t, z), jnp.where(lo, z, sw),
                            jnp.where(lo, sw, z), jnp.where(lo, z, t)], axis=1)


def _project(x, w_in):
    return jnp.dot(_rms_unit(x).astype(BF16), w_in, preferred_element_type=F32)


def _meta_kernel(meta_ref, win_ref, kgain_ref, rope_ref, u_ref, kd_ref, v_ref):
    proj = _project(meta_ref[...], win_ref[...])
    u_ref[...] = proj[:, :POOL_WIDTH]
    k = proj[:, POOL_WIDTH + ATTN_WIDTH:POOL_WIDTH + ATTN_WIDTH + KV_WIDTH]
    v = proj[:, POOL_WIDTH + ATTN_WIDTH + KV_WIDTH:]
    k = _head_norm_rope(k, kgain_ref[...], rope_ref[...])
    kd_ref[...] = _expand_kv(k).astype(BF16)
    v_ref[...] = v.astype(BF16)


def _in_kernel(x_ref, umeta_ref, win_ref, wpool_ref, pscale_ref, qgain_ref, kgain_ref,
               rope_ref, yp_ref, q_ref, kd_ref, vt_ref, carry_ref, proj_ref):
    j = pl.program_id(1)

    @pl.when(j == 0)
    def _():
        carry_ref[...] = umeta_ref[...]

    n_sub = TQ_IN // RB_IN

    def normalise(i):
        return _rms_unit(x_ref[0, i * RB_IN:(i + 1) * RB_IN, :]).astype(BF16)

    def project(i, a):
        proj_ref[i] = jnp.dot(a, win_ref[...], preferred_element_type=F32)

    def finish(i):
        rows = slice(i * RB_IN, (i + 1) * RB_IN)
        rope = rope_ref[rows, :]
        u = proj_ref[i, :, :POOL_WIDTH]
        acc = jnp.concatenate([carry_ref[...], u], axis=0)
        for gi, w in enumerate(POOL_WINDOWS):
            lo = gi * POOL_GROUP
            acc = acc[:, POOL_GROUP * (1 if gi else 0):]
            acc = acc + pltpu.roll(acc, w // 2, 0)
            mixed = acc[N_META:, :POOL_GROUP] * (1.0 / w) - u[:, lo:lo + POOL_GROUP]
            y = jnp.dot(mixed.astype(BF16), wpool_ref[gi], preferred_element_type=F32)
            yp_ref[0, rows, lo:lo + POOL_GROUP] = (y * pscale_ref[:, lo:lo + POOL_GROUP]).astype(BF16)
        carry_ref[...] = u[RB_IN - N_META:, :]

        n_qc = ATTN_WIDTH // LANES
        xs = [proj_ref[i, :, POOL_WIDTH + c * LANES:POOL_WIDTH + (c + 1) * LANES] for c in range(n_qc + 1)]
        outs = _head_norm_rope_many(xs, [qgain_ref[...]] * n_qc + [kgain_ref[...]], rope)
        for c in range(n_qc):
            q_ref[0, rows, c * LANES:(c + 1) * LANES] = outs[c].astype(BF16)
        kd_ref[0, rows, :] = _expand_kv(outs[n_qc]).astype(BF16)
        v = proj_ref[i, :, POOL_WIDTH + ATTN_WIDTH + KV_WIDTH:]
        vt_ref[0, :, rows] = jnp.transpose(v).astype(BF16)

    a_next = normalise(0)
    for i in range(n_sub):
        project(i, a_next)
        if i + 1 < n_sub:
            a_next = normalise(i + 1)
        if i > 0:
            finish(i - 1)
    finish(n_sub - 1)


def _route(logits, upper):
    n = logits.shape[0]
    lt = jnp.transpose(logits)
    sub = SUBLANES
    row8 = lax.broadcasted_iota(jnp.int32, (sub, n), 0).astype(F32)
    row16 = lax.broadcasted_iota(jnp.int32, (N_EXPERTS, n), 0).astype(F32)
    g_ok = row8 < N_GROUPS
    gl = jnp.where(g_ok, lt[N_EXPERTS:N_EXPERTS + sub, :], NEG_INF)
    gmax = jnp.max(gl, axis=0, keepdims=True)
    gsum = jnp.sum(jnp.where(g_ok, jnp.exp(gl - gmax), 0.0), axis=0, keepdims=True)
    g_prob = 1.0 / gsum
    g_idx = jnp.min(jnp.where(gl == gmax, row8, float(sub)), axis=0, keepdims=True)
    e_lo = g_idx * EXPERTS_PER_GROUP
    emask = (row16 >= e_lo) & (row16 < e_lo + EXPERTS_PER_GROUP)
    el = jnp.where(emask, lt[:N_EXPERTS, :], NEG_INF)
    big = float(N_EXPERTS)
    e1 = jnp.max(el, axis=0, keepdims=True)
    i1 = jnp.min(jnp.where(el == e1, row16, big), axis=0, keepdims=True)
    el2 = jnp.where(row16 == i1, NEG_INF, el)
    e2 = jnp.max(el2, axis=0, keepdims=True)
    i2 = jnp.min(jnp.where(el2 == e2, row16, big), axis=0, keepdims=True)
    t = jnp.exp(e2 - e1)
    w1 = 1.0 / (1.0 + t)
    w2 = t * w1
    gates_t = jnp.where(row16 == i1, w1 * g_prob, 0.0) + jnp.where(row16 == i2, w2 * g_prob, 0.0)
    onehot = jnp.where(row8 == g_idx, 1.0, 0.0)
    earlier = jnp.dot(onehot.astype(BF16), upper, preferred_element_type=F32)
    rank = jnp.sum(onehot * earlier, axis=0, keepdims=True)
    hi = gates_t.astype(BF16).astype(F32)
    lo = (gates_t - hi).astype(BF16).astype(F32)
    gid8 = jnp.broadcast_to(g_idx, (sub, n))
    rank8 = jnp.broadcast_to(rank, (sub, n))
    assert GID_LANE == N_EXPERTS and RANK_LANE == GID_LANE + sub and LO_SHIFT == RANK_LANE + sub
    full_t = jnp.concatenate(
        [hi, gid8, rank8, lo, jnp.zeros((LANES - LO_SHIFT - N_EXPERTS, n), F32)], axis=0)
    rows = jnp.concatenate([g_idx, rank, jnp.zeros((sub - 2, n), F32)], axis=0)
    counts = jnp.sum(onehot, axis=1, keepdims=True)
    return (jnp.transpose(full_t).astype(BF16), rows,
            jnp.broadcast_to(counts, (sub, LANES)).astype(jnp.int32))


def _attn_kernel(sink_ref, x_ref, yp_ref, q_ref, kd_ref, vt_ref, kmeta_ref, vtmeta_ref, wout_ref,
                 fgain_ref, wr_ref, upper_ref, h2_ref, m_ref, table_ref, rows_ref, cnt_ref,
                 yattn_ref, kcat_ref, vbd_ref, s_ref, p_ref, inv_ref, logit_ref):
    j = pl.program_id(1)
    nt = (((1,), (1,)), ((), ()))
    n_r = TQ_ATT // WINDOW
    band = 2 * WINDOW
    m0 = 2 * band

    @pl.when((pl.program_id(0) == 0) & (j == 0))
    def _():
        vbd_ref[...] = jnp.zeros(vbd_ref.shape, BF16)
        p_ref[...] = jnp.zeros(p_ref.shape, BF16)

    def window_start(r):
        return pl.multiple_of(jnp.maximum(j * TQ_ATT + (r - 1) * WINDOW, 0), WINDOW)

    def stage_scores(r):
        start = window_start(r)
        for g in range(N_KV_HEADS):
            vb = vt_ref[0, g * HEAD_DIM:(g + 1) * HEAD_DIM, pl.ds(start, band)]
            vm = vtmeta_ref[g * HEAD_DIM:(g + 1) * HEAD_DIM, :]
            for hh in range(2):
                col = (2 * g + hh) * LANES
                kcat_ref[r, g, hh * band:(hh + 1) * band, :] = kd_ref[0, pl.ds(start, band), col:col + LANES]
                kcat_ref[r, g, m0 + hh * N_META:m0 + (hh + 1) * N_META, :] = kmeta_ref[:, col:col + LANES]
                rows = slice(hh * HEAD_DIM, (hh + 1) * HEAD_DIM)
                vbd_ref[r, g, rows, hh * band:(hh + 1) * band] = vb
                vbd_ref[r, g, rows, m0 + hh * N_META:m0 + (hh + 1) * N_META] = vm
            rq = slice(r * WINDOW, (r + 1) * WINDOW)
            qq = jnp.concatenate([q_ref[0, rq, (2 * g + ch) * LANES:(2 * g + ch + 1) * LANES]
                                  for ch in range(2)], axis=0)
            s_ref[r * N_KV_HEADS + g] = lax.dot_general(kcat_ref[r, g], qq, nt, preferred_element_type=F32)

    def stage_softmax(r):
        kpos = window_start(r) + lax.broadcasted_iota(jnp.int32, (band, WINDOW), 0)
        qpos = j * TQ_ATT + r * WINDOW + lax.broadcasted_iota(jnp.int32, (band, WINDOW), 1)
        d = qpos - kpos
        bias = jnp.where((d >= 0) & (d < WINDOW), 0.0, NEG_INF)
        bias = jnp.concatenate([bias, bias], axis=1)
        left = lax.broadcasted_iota(jnp.int32, (1, 2 * WINDOW), 1) < WINDOW
        for g in range(N_KV_HEADS):
            it = r * N_KV_HEADS + g
            for hh in range(2):
                sink = jnp.where(left, sink_ref[4 * g + hh], sink_ref[4 * g + 2 + hh])
                s = s_ref[it, hh * band:(hh + 1) * band, :] + bias
                sm = s_ref[it, m0 + hh * N_META:m0 + (hh + 1) * N_META, :]
                mx = jnp.maximum(jnp.max(s, axis=0, keepdims=True), jnp.max(sm, axis=0, keepdims=True))
                mx = jnp.maximum(mx, sink)
                p = jnp.exp2(s - mx)
                pm = jnp.exp2(sm - mx)
                den = (jnp.sum(p, axis=0, keepdims=True) + jnp.sum(pm, axis=0, keepdims=True)
                       + jnp.exp2(sink - mx))
                p_ref[it, hh * band:(hh + 1) * band, :] = p.astype(BF16)
                p_ref[it, m0 + hh * N_META:m0 + (hh + 1) * N_META, :] = pm.astype(BF16)
                inv_ref[it, hh * HEAD_DIM:(hh + 1) * HEAD_DIM, :] = jnp.broadcast_to(
                    1.0 / den, (HEAD_DIM, 2 * WINDOW))

    def stage_values(r):
        for g in range(N_KV_HEADS):
            it = r * N_KV_HEADS + g
            o_t = jnp.dot(vbd_ref[r, g], p_ref[it], preferred_element_type=F32) * inv_ref[it]
            for ch in range(2):
                c = 2 * g + ch
                yattn_ref[r * WINDOW:(r + 1) * WINDOW, c * LANES:(c + 1) * LANES] = (
                    jnp.transpose(o_t[:, ch * WINDOW:(ch + 1) * WINDOW]).astype(BF16))

    def stage_out(r):
        if (r + 1) % OUT_BLOCKS:
            return
        rows = slice((r + 1 - OUT_BLOCKS) * WINDOW, (r + 1) * WINDOW)
        mix = (jnp.dot(yp_ref[0, rows, :], wout_ref[:POOL_WIDTH, :], preferred_element_type=F32)
               + jnp.dot(yattn_ref[rows, :], wout_ref[POOL_WIDTH:, :], preferred_element_type=F32))
        h2 = x_ref[0, rows, :] + mix
        h2_ref[0, rows, :] = h2
        m = _rms_rows(h2, fgain_ref[...]).astype(BF16)
        m_ref[0, rows, :] = m
        logit_ref[r // OUT_BLOCKS] = jnp.dot(m, wr_ref[...], preferred_element_type=F32)

    def stage_route(r):
        if (r + 1) % OUT_BLOCKS:
            return
        table, rows, counts = _route(logit_ref[r // OUT_BLOCKS], upper_ref[...])
        table_ref[0, (r + 1 - OUT_BLOCKS) * WINDOW:(r + 1) * WINDOW, :] = table
        rows_ref[r // OUT_BLOCKS] = rows
        cnt_ref[r // OUT_BLOCKS] = counts

    stages = (stage_scores, stage_softmax, stage_values, stage_out, stage_route)
    for t in range(n_r + len(stages) - 1):
        for k, stage in enumerate(stages):
            if 0 <= t - k < n_r:
                stage(t - k)


def _moe_kernel(cnt_ref, m_ref, table_ref, rows_ref, h2_ref, wg_ref, wu_ref, wd_ref, out_ref,
                ys_ref, hid_ref):
    i = pl.program_id(0)
    per = TM_MOE // ROUTE_ROWS
    counts = [[cnt_ref[(i * per + h) * N_GROUPS + g] for g in range(N_GROUPS)] for h in range(per)]
    nsub, base = [], []
    total = 0
    for g in range(N_GROUPS):
        n_g = sum(counts[h][g] for h in range(per))
        base.append(total * SUB)
        nsub.append(lax.div(n_g + (SUB - 1), SUB))
        total = total + nsub[-1]
    first = [[base[g] + sum(counts[hh][g] for hh in range(h)) for g in range(N_GROUPS)] for h in range(per)]

    @pl.when(i == 0)
    def _():
        ys_ref[...] = jnp.zeros(ys_ref.shape, BF16)

    def pick(gid, offsets):
        out = _as_f32(offsets[N_GROUPS - 1])
        for g in range(N_GROUPS - 2, -1, -1):
            out = jnp.where(gid == g, _as_f32(offsets[g]), out)
        return out

    table = table_ref[...]
    table_f = table.astype(F32)
    gid_col = table_f[:, GID_LANE:GID_LANE + 1]
    row_id = lax.broadcasted_iota(jnp.int32, (TM_MOE, 1), 0)
    off_col = pick(gid_col, first[per - 1])
    for h in range(per - 2, -1, -1):
        off_col = jnp.where(row_id < (h + 1) * ROUTE_ROWS, pick(gid_col, first[h]), off_col)
    pos_col = table_f[:, RANK_LANE:RANK_LANE + 1] + off_col
    pos_row = jnp.concatenate(
        [rows_ref[h, 1:2, :] + pick(rows_ref[h, 0:1, :], first[h]) for h in range(per)], axis=1)

    for g in range(N_GROUPS):
        def body(c, carry, g=g):
            r0 = pl.multiple_of(base[g] + c * SUB, BF16_ROWS)
            rows = (r0 + lax.broadcasted_iota(jnp.int32, (SUB, 1), 0)).astype(F32)
            perm = jnp.where(pos_row == rows, 1.0, 0.0).astype(BF16)
            xs = jnp.dot(perm, m_ref[...], preferred_element_type=F32).astype(BF16)
            gs = jnp.dot(perm, table, preferred_element_type=F32)
            for jj in range(EXPERTS_PER_GROUP):
                e = g * EXPERTS_PER_GROUP + jj
                gt = jnp.dot(xs, wg_ref[e], preferred_element_type=F32)
                up = jnp.dot(xs, wu_ref[e], preferred_element_type=F32)
                gate = gs[:, e:e + 1] + gs[:, LO_SHIFT + e:LO_SHIFT + e + 1]
                hid = gt * (1.0 / (1.0 + jnp.exp(-gt))) * up * gate
                hid_ref[:, jj * D_EXPERT:(jj + 1) * D_EXPERT] = hid.astype(BF16)
            y = jnp.dot(hid_ref[...], wd_ref[g * D_GROUP:(g + 1) * D_GROUP, :],
                        preferred_element_type=F32)
            ys_ref[pl.ds(r0, SUB), :] = y.astype(BF16)
            return carry
        lax.fori_loop(0, nsub[g], body, 0)

    def scatter(n_rows):
        cols = lax.broadcasted_iota(jnp.int32, (1, n_rows), 1).astype(F32)
        perm_t = jnp.where(pos_col == cols, 1.0, 0.0).astype(BF16)
        out_ref[...] = h2_ref[...] + jnp.dot(perm_t, ys_ref[:n_rows, :], preferred_element_type=F32)

    short = total * SUB <= YS_SHORT

    @pl.when(short)
    def _():
        scatter(YS_SHORT)

    @pl.when(jnp.logical_not(short))
    def _():
        scatter(YS_ROWS)


def _as_f32(v):
    return float(v) if isinstance(v, int) else v.astype(F32)


def _const_spec(shape):
    n = len(shape)
    return pl.BlockSpec(shape, lambda *_: (0,) * n)


def _resident_spec(shape):
    n = len(shape)
    return pl.BlockSpec(shape, lambda *_: (0,) * n, pipeline_mode=pl.Buffered(1))


def kernel(x, meta_tokens, attn_norm_gain, w_in, w_pool, pool_scale, q_norm_gain, k_norm_gain,
           attn_sinks, w_out, ffn_norm_gain, w_group_router, w_expert_router, w_gate, w_up, w_down):
    B, S, D = x.shape
    assert D == D_MODEL and S % TQ_IN == 0 and S % TQ_ATT == 0 and (B * S) % TM_MOE == 0
    assert w_in.shape[0] == 1, "single layer"
    T = B * S

    rope = _rope_tables(N_META + S)
    fgain = ffn_norm_gain[0][None, :]
    qgain2 = jnp.tile(q_norm_gain[0] * LOG2_E, 2)[None, :]
    kgain2 = jnp.tile(k_norm_gain[0] * (HEAD_DIM ** 0.5), 2)[None, :]
    pscale = pool_scale[0][None, :]
    w_in_b = (attn_norm_gain[0][:, None] * w_in[0]).astype(BF16)
    w_pool_b = w_pool[0].astype(BF16)
    w_out_b = w_out[0].astype(BF16)
    w_r = jnp.concatenate(
        [w_expert_router[0], w_group_router[0],
         jnp.zeros((D, LANES - N_EXPERTS - N_GROUPS), F32)], axis=1).astype(BF16)
    wg = w_gate[0].astype(BF16)
    wu = w_up[0].astype(BF16)
    wd = w_down[0].astype(BF16).reshape(N_EXPERTS * D_EXPERT, D)
    params = pltpu.CompilerParams(vmem_limit_bytes=VMEM_LIMIT)

    u_meta, kd_meta, v_meta = pl.pallas_call(
        _meta_kernel,
        out_shape=(jax.ShapeDtypeStruct((N_META, POOL_WIDTH), F32),
                   jax.ShapeDtypeStruct((N_META, KV_EXP), BF16),
                   jax.ShapeDtypeStruct((N_META, KV_WIDTH), BF16)),
        compiler_params=params,
        name="meta_proj",
    )(meta_tokens, w_in_b, kgain2, rope[:N_META])
    vt_meta = v_meta.T

    yp, q, kd, vt = pl.pallas_call(
        _in_kernel,
        grid=(B, S // TQ_IN),
        in_specs=[
            pl.BlockSpec((1, TQ_IN, D), lambda b, j: (b, j, 0)),
            _const_spec((N_META, POOL_WIDTH)),
            _const_spec((D, IN_WIDTH)),
            _const_spec((len(POOL_WINDOWS), POOL_GROUP, POOL_GROUP)),
            _const_spec((1, POOL_WIDTH)),
            _const_spec((1, LANES)),
            _const_spec((1, LANES)),
            pl.BlockSpec((TQ_IN, 3 * LANES), lambda b, j: (j, 0)),
        ],
        out_specs=(
            pl.BlockSpec((1, TQ_IN, POOL_WIDTH), lambda b, j: (b, j, 0)),
            pl.BlockSpec((1, TQ_IN, ATTN_WIDTH), lambda b, j: (b, j, 0)),
            pl.BlockSpec((1, TQ_IN, KV_EXP), lambda b, j: (b, j, 0)),
            pl.BlockSpec((1, KV_WIDTH, TQ_IN), lambda b, j: (b, 0, j)),
        ),
        out_shape=(jax.ShapeDtypeStruct((B, S, POOL_WIDTH), BF16),
                   jax.ShapeDtypeStruct((B, S, ATTN_WIDTH), BF16),
                   jax.ShapeDtypeStruct((B, S, KV_EXP), BF16),
                   jax.ShapeDtypeStruct((B, KV_WIDTH, S), BF16)),
        scratch_shapes=[pltpu.VMEM((N_META, POOL_WIDTH), F32),
                        pltpu.VMEM((TQ_IN // RB_IN, RB_IN, IN_WIDTH), F32)],
        compiler_params=pltpu.CompilerParams(
            dimension_semantics=("arbitrary", "arbitrary"), vmem_limit_bytes=VMEM_LIMIT),
        name="in_proj",
    )(x, u_meta, w_in_b, w_pool_b, pscale, qgain2, kgain2, rope[N_META:])

    upper = jnp.asarray(np.triu(np.ones((ROUTE_ROWS, ROUTE_ROWS), np.float32), 1), dtype=BF16)
    h2, m, table, rows, cnt = pl.pallas_call(
        _attn_kernel,
        grid=(B, S // TQ_ATT),
        in_specs=[
            pl.BlockSpec(memory_space=pltpu.SMEM),
            pl.BlockSpec((1, TQ_ATT, D), lambda b, j: (b, j, 0)),
            pl.BlockSpec((1, TQ_ATT, POOL_WIDTH), lambda b, j: (b, j, 0)),
            pl.BlockSpec((1, TQ_ATT, ATTN_WIDTH), lambda b, j: (b, j, 0)),
            pl.BlockSpec((1, S, KV_EXP), lambda b, j: (b, 0, 0)),
            pl.BlockSpec((1, KV_WIDTH, S), lambda b, j: (b, 0, 0)),
            _const_spec((N_META, KV_EXP)),
            _const_spec((KV_WIDTH, N_META)),
            _const_spec((D, D)),
            _const_spec((1, D)),
            _const_spec((D, LANES)),
            _const_spec((ROUTE_ROWS, ROUTE_ROWS)),
        ],
        out_specs=(
            pl.BlockSpec((1, TQ_ATT, D), lambda b, j: (b, j, 0)),
            pl.BlockSpec((1, TQ_ATT, D), lambda b, j: (b, j, 0)),
            pl.BlockSpec((1, TQ_ATT, LANES), lambda b, j: (b, j, 0)),
            pl.BlockSpec((TQ_ATT // ROUTE_ROWS, SUBLANES, ROUTE_ROWS), lambda b, j: (b * (S // TQ_ATT) + j, 0, 0)),
            pl.BlockSpec((TQ_ATT // ROUTE_ROWS, SUBLANES, LANES), lambda b, j: (b * (S // TQ_ATT) + j, 0, 0)),
        ),
        out_shape=(jax.ShapeDtypeStruct((B, S, D), F32),
                   jax.ShapeDtypeStruct((B, S, D), BF16),
                   jax.ShapeDtypeStruct((B, S, LANES), BF16),
                   jax.ShapeDtypeStruct((T // ROUTE_ROWS, SUBLANES, ROUTE_ROWS), F32),
                   jax.ShapeDtypeStruct((T // ROUTE_ROWS, SUBLANES, LANES), jnp.int32)),
        scratch_shapes=[
            pltpu.VMEM((TQ_ATT, ATTN_WIDTH), BF16),
            pltpu.VMEM((TQ_ATT // WINDOW, N_KV_HEADS, KEY_ROWS, LANES), BF16),
            pltpu.VMEM((TQ_ATT // WINDOW, N_KV_HEADS, LANES, KEY_PAD), BF16),
            pltpu.VMEM((N_ITEMS, KEY_ROWS, 2 * LANES), F32),
            pltpu.VMEM((N_ITEMS, KEY_PAD, 2 * LANES), BF16),
            pltpu.VMEM((N_ITEMS, LANES, 2 * LANES), F32),
            pltpu.VMEM((TQ_ATT // ROUTE_ROWS, ROUTE_ROWS, LANES), F32),
        ],
        compiler_params=pltpu.CompilerParams(
            dimension_semantics=("arbitrary", "arbitrary"), vmem_limit_bytes=VMEM_LIMIT),
        name="attn_out",
    )(attn_sinks[0] * LOG2_E, x, yp, q, kd, vt, kd_meta, vt_meta, w_out_b, fgain, w_r, upper)

    cnt_flat = cnt[:, :N_GROUPS, 0].reshape(-1)
    out = pl.pallas_call(
        _moe_kernel,
        grid_spec=pltpu.PrefetchScalarGridSpec(
            num_scalar_prefetch=1,
            grid=(T // TM_MOE,),
            in_specs=[
                pl.BlockSpec((TM_MOE, D), lambda i, c: (i, 0)),
                pl.BlockSpec((TM_MOE, LANES), lambda i, c: (i, 0)),
                pl.BlockSpec((TM_MOE // ROUTE_ROWS, SUBLANES, ROUTE_ROWS), lambda i, c: (i, 0, 0)),
                pl.BlockSpec((TM_MOE, D), lambda i, c: (i, 0)),
                _resident_spec((N_EXPERTS, D, D_EXPERT)),
                _resident_spec((N_EXPERTS, D, D_EXPERT)),
                _resident_spec((N_EXPERTS * D_EXPERT, D)),
            ],
            out_specs=pl.BlockSpec((TM_MOE, D), lambda i, c: (i, 0)),
            scratch_shapes=[pltpu.VMEM((YS_ROWS, D), BF16),
                            pltpu.VMEM((SUB, D_GROUP), BF16)],
        ),
        out_shape=jax.ShapeDtypeStruct((T, D), F32),
        compiler_params=pltpu.CompilerParams(
            dimension_semantics=("arbitrary",), vmem_limit_bytes=VMEM_LIMIT),
        name="moe",
    )(cnt_flat, m.reshape(T, D), table.reshape(T, LANES), rows, h2.reshape(T, D), wg, wu, wd)
    return out.reshape(B, S, D)
```

```python
import functools

import numpy as np
import jax
import jax.numpy as jnp
from jax import lax
from jax.experimental import pallas as pl
from jax.experimental.pallas import tpu as pltpu

D_MODEL = 1024
N_META = 16
POOL_WIDTH = 512
POOL_WINDOWS = (2, 4, 8, 16)
POOL_GROUP = 128
HEAD_DIM = 64
N_HEADS = 8
N_KV_HEADS = 2
ATTN_WIDTH = N_HEADS * HEAD_DIM
KV_WIDTH = N_KV_HEADS * HEAD_DIM
WINDOW = 128
ROT_DIM = HEAD_DIM // 4
ROPE_THETA = 500000.0
IN_WIDTH = POOL_WIDTH + ATTN_WIDTH + 2 * KV_WIDTH
N_GROUPS = 4
EXPERTS_PER_GROUP = 4
N_EXPERTS = 16
D_EXPERT = 256
EPS = 1e-6
NEG_INF = -1e30
LOG2_E = 1.4426950408889634

LANES = 128
SUBLANES = 8
KV_EXP = 4 * LANES
TQ_IN = 2048
RB_IN = 256
TQ_ATT = 512
KEY_ROWS = 2 * (2 * WINDOW + N_META)
KEY_PAD = -(-KEY_ROWS // LANES) * LANES
OUT_BLOCKS = 2
ROUTE_ROWS = OUT_BLOCKS * WINDOW
N_ITEMS = (TQ_ATT // WINDOW) * N_KV_HEADS
TM_MOE = 512
SUB = 144
BF16_ROWS = 16
MXU_DEPTH = 256
YS_ROWS = -(-((TM_MOE + N_GROUPS * (SUB - 1)) // SUB * SUB) // MXU_DEPTH) * MXU_DEPTH
YS_SHORT = YS_ROWS - MXU_DEPTH
GID_LANE = N_EXPERTS
RANK_LANE = 24
LO_SHIFT = 32
D_GROUP = EXPERTS_PER_GROUP * D_EXPERT
VMEM_LIMIT = 56 * 1024 * 1024

BF16 = jnp.bfloat16
F32 = jnp.float32


def _rope_tables(n_pos):
    half = ROT_DIM // 2
    inv_freq = 1.0 / (ROPE_THETA ** (np.arange(half, dtype=np.float64) / half))
    ang = np.arange(n_pos, dtype=np.float64)[:, None] * inv_freq[None, :]
    cos, sin = np.cos(ang), np.sin(ang)
    c = np.ones((n_pos, HEAD_DIM)); c[:, :half] = cos; c[:, half:ROT_DIM] = cos
    sa = np.zeros((n_pos, HEAD_DIM)); sa[:, :half] = -sin
    sb = np.zeros((n_pos, HEAD_DIM)); sb[:, half:ROT_DIM] = sin
    tab = np.concatenate([np.tile(c, (1, 2)), np.tile(sa, (1, 2)), np.tile(sb, (1, 2))], axis=1)
    return jnp.asarray(tab, dtype=F32)


def _rms_unit(x):
    ms = jnp.mean(x * x, axis=-1, keepdims=True)
    return x * lax.rsqrt(ms + EPS)


def _rms_rows(x, gain):
    return _rms_unit(x) * gain


def _head_norm_rope_many(xs, gains, rope):
    lo = lax.broadcasted_iota(jnp.int32, xs[0].shape, 1) < HEAD_DIM
    sums = []
    for x in xs:
        sq = x * x
        sums.append((jnp.sum(jnp.where(lo, sq, 0.0), axis=-1, keepdims=True),
                     jnp.sum(jnp.where(lo, 0.0, sq), axis=-1, keepdims=True)))
    ys = [x * lax.rsqrt(jnp.where(lo, s_lo, s_hi) + HEAD_DIM * EPS) * g
          for x, g, (s_lo, s_hi) in zip(xs, gains, sums)]
    half = ROT_DIM // 2
    rolled = [(pltpu.roll(y, LANES - half, 1), pltpu.roll(y, half, 1)) for y in ys]
    c, sa, sb = rope[:, 0:LANES], rope[:, LANES:2 * LANES], rope[:, 2 * LANES:3 * LANES]
    return [y * c + ra * sa + rb * sb for y, (ra, rb) in zip(ys, rolled)]


def _head_norm_rope(xc, gain2, rope):
    return _head_norm_rope_many([xc], [gain2], rope)[0]


def _expand_kv(t):
    lo = lax.broadcasted_iota(jnp.int32, t.shape, 1) < HEAD_DIM
    sw = pltpu.roll(t, HEAD_DIM, 1)
    z = jnp.zeros_like(t)
    return jnp.concatenate([jnp.where(lo, t, z), jnp.where(lo, z, sw),
                            jnp.where(lo, sw, z), jnp.where(lo, z, t)], axis=1)


def _project(x, w_in):
    return jnp.dot(_rms_unit(x).astype(BF16), w_in, preferred_element_type=F32)


def _meta_kernel(meta_ref, win_ref, kgain_ref, rope_ref, u_ref, kd_ref, v_ref):
    proj = _project(meta_ref[...], win_ref[...])
    u_ref[...] = proj[:, :POOL_WIDTH]
    k = proj[:, POOL_WIDTH + ATTN_WIDTH:POOL_WIDTH + ATTN_WIDTH + KV_WIDTH]
    v = proj[:, POOL_WIDTH + ATTN_WIDTH + KV_WIDTH:]
    k = _head_norm_rope(k, kgain_ref[...], rope_ref[...])
    kd_ref[...] = _expand_kv(k).astype(BF16)
    v_ref[...] = v.astype(BF16)


def _in_kernel(x_ref, umeta_ref, win_ref, wpool_ref, pscale_ref, qgain_ref, kgain_ref,
               rope_ref, yp_ref, q_ref, kd_ref, vt_ref, carry_ref, proj_ref):
    j = pl.program_id(1)

    @pl.when(j == 0)
    def _():
        carry_ref[...] = umeta_ref[...]

    sizes = [RB_IN] * (TQ_IN // RB_IN - 1) + [RB_IN // 2] * 2
    blocks = [slice(sum(sizes[:i]), sum(sizes[:i + 1])) for i in range(len(sizes))]
    n_sub = len(blocks)

    def normalise(i):
        return _rms_unit(x_ref[0, blocks[i], :]).astype(BF16)

    def project(i, a):
        proj_ref[blocks[i], :] = jnp.dot(a, win_ref[...], preferred_element_type=F32)

    def finish(i):
        rows = blocks[i]
        rope = rope_ref[rows, :]
        u = proj_ref[rows, :POOL_WIDTH]
        acc = jnp.concatenate([carry_ref[...], u], axis=0)
        for gi, w in enumerate(POOL_WINDOWS):
            lo = gi * POOL_GROUP
            acc = acc[:, POOL_GROUP * (1 if gi else 0):]
            acc = acc + pltpu.roll(acc, w // 2, 0)
            mixed = acc[N_META:, :POOL_GROUP] * (1.0 / w) - u[:, lo:lo + POOL_GROUP]
            y = jnp.dot(mixed.astype(BF16), wpool_ref[gi], preferred_element_type=F32)
            yp_ref[0, rows, lo:lo + POOL_GROUP] = (y * pscale_ref[:, lo:lo + POOL_GROUP]).astype(BF16)
        carry_ref[...] = u[sizes[i] - N_META:, :]

        n_qc = ATTN_WIDTH // LANES
        xs = [proj_ref[rows, POOL_WIDTH + c * LANES:POOL_WIDTH + (c + 1) * LANES] for c in range(n_qc + 1)]
        outs = _head_norm_rope_many(xs, [qgain_ref[...]] * n_qc + [kgain_ref[...]], rope)
        for c in range(n_qc):
            q_ref[0, rows, c * LANES:(c + 1) * LANES] = outs[c].astype(BF16)
        kd_ref[0, rows, :] = _expand_kv(outs[n_qc]).astype(BF16)
        v = proj_ref[rows, POOL_WIDTH + ATTN_WIDTH + KV_WIDTH:]
        vt_ref[0, :, rows] = jnp.transpose(v).astype(BF16)

    a_next = normalise(0)
    for i in range(n_sub):
        project(i, a_next)
        if i + 1 < n_sub:
            a_next = normalise(i + 1)
        if i > 0:
            finish(i - 1)
    finish(n_sub - 1)


def _route(logits, upper):
    n = logits.shape[0]
    lt = jnp.transpose(logits)
    sub = SUBLANES
    row8 = lax.broadcasted_iota(jnp.int32, (sub, n), 0).astype(F32)
    row16 = lax.broadcasted_iota(jnp.int32, (N_EXPERTS, n), 0).astype(F32)
    g_ok = row8 < N_GROUPS
    gl = jnp.where(g_ok, lt[N_EXPERTS:N_EXPERTS + sub, :], NEG_INF)
    gmax = jnp.max(gl, axis=0, keepdims=True)
    gsum = jnp.sum(jnp.where(g_ok, jnp.exp(gl - gmax), 0.0), axis=0, keepdims=True)
    g_prob = 1.0 / gsum
    g_idx = jnp.min(jnp.where(gl == gmax, row8, float(sub)), axis=0, keepdims=True)
    e_lo = g_idx * EXPERTS_PER_GROUP
    emask = (row16 >= e_lo) & (row16 < e_lo + EXPERTS_PER_GROUP)
    el = jnp.where(emask, lt[:N_EXPERTS, :], NEG_INF)
    big = float(N_EXPERTS)
    e1 = jnp.max(el, axis=0, keepdims=True)
    i1 = jnp.min(jnp.where(el == e1, row16, big), axis=0, keepdims=True)
    el2 = jnp.where(row16 == i1, NEG_INF, el)
    e2 = jnp.max(el2, axis=0, keepdims=True)
    i2 = jnp.min(jnp.where(el2 == e2, row16, big), axis=0, keepdims=True)
    t = jnp.exp(e2 - e1)
    w1 = 1.0 / (1.0 + t)
    w2 = t * w1
    gates_t = jnp.where(row16 == i1, w1 * g_prob, 0.0) + jnp.where(row16 == i2, w2 * g_prob, 0.0)
    onehot = jnp.where(row8 == g_idx, 1.0, 0.0)
    earlier = jnp.dot(onehot.astype(BF16), upper, preferred_element_type=F32)
    rank = jnp.sum(onehot * earlier, axis=0, keepdims=True)
    hi = gates_t.astype(BF16).astype(F32)
    lo = (gates_t - hi).astype(BF16).astype(F32)
    gid8 = jnp.broadcast_to(g_idx, (sub, n))
    rank8 = jnp.broadcast_to(rank, (sub, n))
    assert GID_LANE == N_EXPERTS and RANK_LANE == GID_LANE + sub and LO_SHIFT == RANK_LANE + sub
    full_t = jnp.concatenate(
        [hi, gid8, rank8, lo, jnp.zeros((LANES - LO_SHIFT - N_EXPERTS, n), F32)], axis=0)
    rows = jnp.concatenate([g_idx, rank, jnp.zeros((sub - 2, n), F32)], axis=0)
    counts = jnp.sum(onehot, axis=1, keepdims=True)
    return (jnp.transpose(full_t).astype(BF16), rows,
            jnp.broadcast_to(counts, (sub, LANES)).astype(jnp.int32))


def _attn_kernel(sink_ref, x_ref, yp_ref, q_ref, kd_ref, vt_ref, kmeta_ref, vtmeta_ref, wout_ref,
                 fgain_ref, wr_ref, upper_ref, h2_ref, m_ref, table_ref, rows_ref, cnt_ref,
                 yattn_ref, kcat_ref, vbd_ref, s_ref, p_ref, inv_ref, logit_ref):
    j = pl.program_id(1)
    nt = (((1,), (1,)), ((), ()))
    n_r = TQ_ATT // WINDOW
    band = 2 * WINDOW
    m0 = 2 * band

    @pl.when((pl.program_id(0) == 0) & (j == 0))
    def _():
        vbd_ref[...] = jnp.zeros(vbd_ref.shape, BF16)
        p_ref[...] = jnp.zeros(p_ref.shape, BF16)

    def window_start(r):
        return pl.multiple_of(jnp.maximum(j * TQ_ATT + (r - 1) * WINDOW, 0), WINDOW)

    def stage_scores(r):
        start = window_start(r)
        for g in range(N_KV_HEADS):
            vb = vt_ref[0, g * HEAD_DIM:(g + 1) * HEAD_DIM, pl.ds(start, band)]
            vm = vtmeta_ref[g * HEAD_DIM:(g + 1) * HEAD_DIM, :]
            for hh in range(2):
                col = (2 * g + hh) * LANES
                kcat_ref[r, g, hh * band:(hh + 1) * band, :] = kd_ref[0, pl.ds(start, band), col:col + LANES]
                kcat_ref[r, g, m0 + hh * N_META:m0 + (hh + 1) * N_META, :] = kmeta_ref[:, col:col + LANES]
                rows = slice(hh * HEAD_DIM, (hh + 1) * HEAD_DIM)
                vbd_ref[r, g, rows, hh * band:(hh + 1) * band] = vb
                vbd_ref[r, g, rows, m0 + hh * N_META:m0 + (hh + 1) * N_META] = vm
            rq = slice(r * WINDOW, (r + 1) * WINDOW)
            qq = jnp.concatenate([q_ref[0, rq, (2 * g + ch) * LANES:(2 * g + ch + 1) * LANES]
                                  for ch in range(2)], axis=0)
            s_ref[r * N_KV_HEADS + g] = lax.dot_general(kcat_ref[r, g], qq, nt, preferred_element_type=F32)

    def stage_softmax(r):
        kpos = window_start(r) + lax.broadcasted_iota(jnp.int32, (band, WINDOW), 0)
        qpos = j * TQ_ATT + r * WINDOW + lax.broadcasted_iota(jnp.int32, (band, WINDOW), 1)
        d = qpos - kpos
        bias = jnp.where((d >= 0) & (d < WINDOW), 0.0, NEG_INF)
        bias = jnp.concatenate([bias, bias], axis=1)
        left = lax.broadcasted_iota(jnp.int32, (1, 2 * WINDOW), 1) < WINDOW
        for g in range(N_KV_HEADS):
            it = r * N_KV_HEADS + g
            for hh in range(2):
                sink = jnp.where(left, sink_ref[4 * g + hh], sink_ref[4 * g + 2 + hh])
                s = s_ref[it, hh * band:(hh + 1) * band, :] + bias
                sm = s_ref[it, m0 + hh * N_META:m0 + (hh + 1) * N_META, :]
                mx = jnp.maximum(jnp.max(s, axis=0, keepdims=True), jnp.max(sm, axis=0, keepdims=True))
                mx = jnp.maximum(mx, sink)
                p = jnp.exp2(s - mx)
                pm = jnp.exp2(sm - mx)
                den = (jnp.sum(p, axis=0, keepdims=True) + jnp.sum(pm, axis=0, keepdims=True)
                       + jnp.exp2(sink - mx))
                p_ref[it, hh * band:(hh + 1) * band, :] = p.astype(BF16)
                p_ref[it, m0 + hh * N_META:m0 + (hh + 1) * N_META, :] = pm.astype(BF16)
                inv_ref[it, hh * HEAD_DIM:(hh + 1) * HEAD_DIM, :] = jnp.broadcast_to(
                    1.0 / den, (HEAD_DIM, 2 * WINDOW))

    def stage_values(r):
        for g in range(N_KV_HEADS):
            it = r * N_KV_HEADS + g
            o_t = jnp.dot(vbd_ref[r, g], p_ref[it], preferred_element_type=F32) * inv_ref[it]
            for ch in range(2):
                c = 2 * g + ch
                yattn_ref[r * WINDOW:(r + 1) * WINDOW, c * LANES:(c + 1) * LANES] = (
                    jnp.transpose(o_t[:, ch * WINDOW:(ch + 1) * WINDOW]).astype(BF16))

    def stage_out(r):
        if (r + 1) % OUT_BLOCKS:
            return
        rows = slice((r + 1 - OUT_BLOCKS) * WINDOW, (r + 1) * WINDOW)
        mix = (jnp.dot(yp_ref[0, rows, :], wout_ref[:POOL_WIDTH, :], preferred_element_type=F32)
               + jnp.dot(yattn_ref[rows, :], wout_ref[POOL_WIDTH:, :], preferred_element_type=F32))
        h2 = x_ref[0, rows, :] + mix
        h2_ref[0, rows, :] = h2
        m = _rms_rows(h2, fgain_ref[...]).astype(BF16)
        m_ref[0, rows, :] = m
        logit_ref[r // OUT_BLOCKS] = jnp.dot(m, wr_ref[...], preferred_element_type=F32)

    def stage_route(r):
        if (r + 1) % OUT_BLOCKS:
            return
        table, rows, counts = _route(logit_ref[r // OUT_BLOCKS], upper_ref[...])
        table_ref[0, (r + 1 - OUT_BLOCKS) * WINDOW:(r + 1) * WINDOW, :] = table
        rows_ref[r // OUT_BLOCKS] = rows
        cnt_ref[r // OUT_BLOCKS] = counts

    stages = (stage_scores, stage_softmax, stage_values, stage_out, stage_route)
    for t in range(n_r + len(stages) - 1):
        for k, stage in enumerate(stages):
            if 0 <= t - k < n_r:
                stage(t - k)


def _moe_kernel(cnt_ref, m_ref, table_ref, rows_ref, h2_ref, wg_ref, wu_ref, wd_ref, out_ref,
                ys_ref, hid_ref):
    i = pl.program_id(0)
    per = TM_MOE // ROUTE_ROWS
    counts = [[cnt_ref[(i * per + h) * N_GROUPS + g] for g in range(N_GROUPS)] for h in range(per)]
    nsub, base = [], []
    total = 0
    for g in range(N_GROUPS):
        n_g = sum(counts[h][g] for h in range(per))
        base.append(total * SUB)
        nsub.append(lax.div(n_g + (SUB - 1), SUB))
        total = total + nsub[-1]
    first = [[base[g] + sum(counts[hh][g] for hh in range(h)) for g in range(N_GROUPS)] for h in range(per)]

    @pl.when(i == 0)
    def _():
        ys_ref[...] = jnp.zeros(ys_ref.shape, BF16)

    def pick(gid, offsets):
        out = _as_f32(offsets[N_GROUPS - 1])
        for g in range(N_GROUPS - 2, -1, -1):
            out = jnp.where(gid == g, _as_f32(offsets[g]), out)
        return out

    table = table_ref[...]
    table_f = table.astype(F32)
    gid_col = table_f[:, GID_LANE:GID_LANE + 1]
    row_id = lax.broadcasted_iota(jnp.int32, (TM_MOE, 1), 0)
    off_col = pick(gid_col, first[per - 1])
    for h in range(per - 2, -1, -1):
        off_col = jnp.where(row_id < (h + 1) * ROUTE_ROWS, pick(gid_col, first[h]), off_col)
    pos_col = table_f[:, RANK_LANE:RANK_LANE + 1] + off_col
    pos_row = jnp.concatenate(
        [rows_ref[h, 1:2, :] + pick(rows_ref[h, 0:1, :], first[h]) for h in range(per)], axis=1)

    for g in range(N_GROUPS):
        def body(c, carry, g=g):
            r0 = pl.multiple_of(base[g] + c * SUB, BF16_ROWS)
            rows = (r0 + lax.broadcasted_iota(jnp.int32, (SUB, 1), 0)).astype(F32)
            perm = jnp.where(pos_row == rows, 1.0, 0.0).astype(BF16)
            xs = jnp.dot(perm, m_ref[...], preferred_element_type=F32).astype(BF16)
            gs = jnp.dot(perm, table, preferred_element_type=F32)
            for jj in range(EXPERTS_PER_GROUP):
                e = g * EXPERTS_PER_GROUP + jj
                gt = jnp.dot(xs, wg_ref[e], preferred_element_type=F32)
                up = jnp.dot(xs, wu_ref[e], preferred_element_type=F32)
                gate = gs[:, e:e + 1] + gs[:, LO_SHIFT + e:LO_SHIFT + e + 1]
                hid = gt * (1.0 / (1.0 + jnp.exp(-gt))) * up * gate
                hid_ref[:, jj * D_EXPERT:(jj + 1) * D_EXPERT] = hid.astype(BF16)
            y = jnp.dot(hid_ref[...], wd_ref[g * D_GROUP:(g + 1) * D_GROUP, :],
                        preferred_element_type=F32)
            ys_ref[pl.ds(r0, SUB), :] = y.astype(BF16)
            return carry
        lax.fori_loop(0, nsub[g], body, 0)

    def scatter(n_rows):
        cols = lax.broadcasted_iota(jnp.int32, (1, n_rows), 1).astype(F32)
        perm_t = jnp.where(pos_col == cols, 1.0, 0.0).astype(BF16)
        out_ref[...] = h2_ref[...] + jnp.dot(perm_t, ys_ref[:n_rows, :], preferred_element_type=F32)

    short = total * SUB <= YS_SHORT

    @pl.when(short)
    def _():
        scatter(YS_SHORT)

    @pl.when(jnp.logical_not(short))
    def _():
        scatter(YS_ROWS)


def _as_f32(v):
    return float(v) if isinstance(v, int) else v.astype(F32)


def _const_spec(shape):
    n = len(shape)
    return pl.BlockSpec(shape, lambda *_: (0,) * n)


def _resident_spec(shape):
    n = len(shape)
    return pl.BlockSpec(shape, lambda *_: (0,) * n, pipeline_mode=pl.Buffered(1))


def kernel(x, meta_tokens, attn_norm_gain, w_in, w_pool, pool_scale, q_norm_gain, k_norm_gain,
           attn_sinks, w_out, ffn_norm_gain, w_group_router, w_expert_router, w_gate, w_up, w_down):
    B, S, D = x.shape
    assert D == D_MODEL and S % TQ_IN == 0 and S % TQ_ATT == 0 and (B * S) % TM_MOE == 0
    assert w_in.shape[0] == 1, "single layer"
    T = B * S

    rope = _rope_tables(N_META + S)
    fgain = ffn_norm_gain[0][None, :]
    qgain2 = jnp.tile(q_norm_gain[0] * LOG2_E, 2)[None, :]
    kgain2 = jnp.tile(k_norm_gain[0] * (HEAD_DIM ** 0.5), 2)[None, :]
    pscale = pool_scale[0][None, :]
    w_in_b = (attn_norm_gain[0][:, None] * w_in[0]).astype(BF16)
    w_pool_b = w_pool[0].astype(BF16)
    w_out_b = w_out[0].astype(BF16)
    w_r = jnp.concatenate(
        [w_expert_router[0], w_group_router[0],
         jnp.zeros((D, LANES - N_EXPERTS - N_GROUPS), F32)], axis=1).astype(BF16)
    wg = w_gate[0].astype(BF16)
    wu = w_up[0].astype(BF16)
    wd = w_down[0].astype(BF16).reshape(N_EXPERTS * D_EXPERT, D)
    params = pltpu.CompilerParams(vmem_limit_bytes=VMEM_LIMIT)

    u_meta, kd_meta, v_meta = pl.pallas_call(
        _meta_kernel,
        out_shape=(jax.ShapeDtypeStruct((N_META, POOL_WIDTH), F32),
                   jax.ShapeDtypeStruct((N_META, KV_EXP), BF16),
                   jax.ShapeDtypeStruct((N_META, KV_WIDTH), BF16)),
        compiler_params=params,
        name="meta_proj",
    )(meta_tokens, w_in_b, kgain2, rope[:N_META])
    vt_meta = v_meta.T

    yp, q, kd, vt = pl.pallas_call(
        _in_kernel,
        grid=(B, S // TQ_IN),
        in_specs=[
            pl.BlockSpec((1, TQ_IN, D), lambda b, j: (b, j, 0)),
            _const_spec((N_META, POOL_WIDTH)),
            _const_spec((D, IN_WIDTH)),
            _const_spec((len(POOL_WINDOWS), POOL_GROUP, POOL_GROUP)),
            _const_spec((1, POOL_WIDTH)),
            _const_spec((1, LANES)),
            _const_spec((1, LANES)),
            pl.BlockSpec((TQ_IN, 3 * LANES), lambda b, j: (j, 0)),
        ],
        out_specs=(
            pl.BlockSpec((1, TQ_IN, POOL_WIDTH), lambda b, j: (b, j, 0)),
            pl.BlockSpec((1, TQ_IN, ATTN_WIDTH), lambda b, j: (b, j, 0)),
            pl.BlockSpec((1, TQ_IN, KV_EXP), lambda b, j: (b, j, 0)),
            pl.BlockSpec((1, KV_WIDTH, TQ_IN), lambda b, j: (b, 0, j)),
        ),
        out_shape=(jax.ShapeDtypeStruct((B, S, POOL_WIDTH), BF16),
                   jax.ShapeDtypeStruct((B, S, ATTN_WIDTH), BF16),
                   jax.ShapeDtypeStruct((B, S, KV_EXP), BF16),
                   jax.ShapeDtypeStruct((B, KV_WIDTH, S), BF16)),
        scratch_shapes=[pltpu.VMEM((N_META, POOL_WIDTH), F32),
                        pltpu.VMEM((TQ_IN, IN_WIDTH), F32)],
        compiler_params=pltpu.CompilerParams(
            dimension_semantics=("arbitrary", "arbitrary"), vmem_limit_bytes=VMEM_LIMIT),
        name="in_proj",
    )(x, u_meta, w_in_b, w_pool_b, pscale, qgain2, kgain2, rope[N_META:])

    upper = jnp.asarray(np.triu(np.ones((ROUTE_ROWS, ROUTE_ROWS), np.float32), 1), dtype=BF16)
    h2, m, table, rows, cnt = pl.pallas_call(
        _attn_kernel,
        grid=(B, S // TQ_ATT),
        in_specs=[
            pl.BlockSpec(memory_space=pltpu.SMEM),
            pl.BlockSpec((1, TQ_ATT, D), lambda b, j: (b, j, 0)),
            pl.BlockSpec((1, TQ_ATT, POOL_WIDTH), lambda b, j: (b, j, 0)),
            pl.BlockSpec((1, TQ_ATT, ATTN_WIDTH), lambda b, j: (b, j, 0)),
            pl.BlockSpec((1, S, KV_EXP), lambda b, j: (b, 0, 0)),
            pl.BlockSpec((1, KV_WIDTH, S), lambda b, j: (b, 0, 0)),
            _const_spec((N_META, KV_EXP)),
            _const_spec((KV_WIDTH, N_META)),
            _const_spec((D, D)),
            _const_spec((1, D)),
            _const_spec((D, LANES)),
            _const_spec((ROUTE_ROWS, ROUTE_ROWS)),
        ],
        out_specs=(
            pl.BlockSpec((1, TQ_ATT, D), lambda b, j: (b, j, 0)),
            pl.BlockSpec((1, TQ_ATT, D), lambda b, j: (b, j, 0)),
            pl.BlockSpec((1, TQ_ATT, LANES), lambda b, j: (b, j, 0)),
            pl.BlockSpec((TQ_ATT // ROUTE_ROWS, SUBLANES, ROUTE_ROWS), lambda b, j: (b * (S // TQ_ATT) + j, 0, 0)),
            pl.BlockSpec((TQ_ATT // ROUTE_ROWS, SUBLANES, LANES), lambda b, j: (b * (S // TQ_ATT) + j, 0, 0)),
        ),
        out_shape=(jax.ShapeDtypeStruct((B, S, D), F32),
                   jax.ShapeDtypeStruct((B, S, D), BF16),
                   jax.ShapeDtypeStruct((B, S, LANES), BF16),
                   jax.ShapeDtypeStruct((T // ROUTE_ROWS, SUBLANES, ROUTE_ROWS), F32),
                   jax.ShapeDtypeStruct((T // ROUTE_ROWS, SUBLANES, LANES), jnp.int32)),
        scratch_shapes=[
            pltpu.VMEM((TQ_ATT, ATTN_WIDTH), BF16),
            pltpu.VMEM((TQ_ATT // WINDOW, N_KV_HEADS, KEY_ROWS, LANES), BF16),
            pltpu.VMEM((TQ_ATT // WINDOW, N_KV_HEADS, LANES, KEY_PAD), BF16),
            pltpu.VMEM((N_ITEMS, KEY_ROWS, 2 * LANES), F32),
            pltpu.VMEM((N_ITEMS, KEY_PAD, 2 * LANES), BF16),
            pltpu.VMEM((N_ITEMS, LANES, 2 * LANES), F32),
            pltpu.VMEM((TQ_ATT // ROUTE_ROWS, ROUTE_ROWS, LANES), F32),
        ],
        compiler_params=pltpu.CompilerParams(
            dimension_semantics=("arbitrary", "arbitrary"), vmem_limit_bytes=VMEM_LIMIT),
        name="attn_out",
    )(attn_sinks[0] * LOG2_E, x, yp, q, kd, vt, kd_meta, vt_meta, w_out_b, fgain, w_r, upper)

    cnt_flat = cnt[:, :N_GROUPS, 0].reshape(-1)
    out = pl.pallas_call(
        _moe_kernel,
        grid_spec=pltpu.PrefetchScalarGridSpec(
            num_scalar_prefetch=1,
            grid=(T // TM_MOE,),
            in_specs=[
                pl.BlockSpec((TM_MOE, D), lambda i, c: (i, 0)),
                pl.BlockSpec((TM_MOE, LANES), lambda i, c: (i, 0)),
                pl.BlockSpec((TM_MOE // ROUTE_ROWS, SUBLANES, ROUTE_ROWS), lambda i, c: (i, 0, 0)),
                pl.BlockSpec((TM_MOE, D), lambda i, c: (i, 0)),
                _resident_spec((N_EXPERTS, D, D_EXPERT)),
                _resident_spec((N_EXPERTS, D, D_EXPERT)),
                _resident_spec((N_EXPERTS * D_EXPERT, D)),
            ],
            out_specs=pl.BlockSpec((TM_MOE, D), lambda i, c: (i, 0)),
            scratch_shapes=[pltpu.VMEM((YS_ROWS, D), BF16),
                            pltpu.VMEM((SUB, D_GROUP), BF16)],
        ),
        out_shape=jax.ShapeDtypeStruct((T, D), F32),
        compiler_params=pltpu.CompilerParams(
            dimension_semantics=("arbitrary",), vmem_limit_bytes=VMEM_LIMIT),
        name="moe",
    )(cnt_flat, m.reshape(T, D), table.reshape(T, LANES), rows, h2.reshape(T, D), wg, wu, wd)
    return out.reshape(B, S, D)
```

```python
import functools

import numpy as np
import jax
import jax.numpy as jnp
from jax import lax
from jax.experimental import pallas as pl
from jax.experimental.pallas import tpu as pltpu

D_MODEL = 1024
N_META = 16
POOL_WIDTH = 512
POOL_WINDOWS = (2, 4, 8, 16)
POOL_GROUP = 128
HEAD_DIM = 64
N_HEADS = 8
N_KV_HEADS = 2
ATTN_WIDTH = N_HEADS * HEAD_DIM
KV_WIDTH = N_KV_HEADS * HEAD_DIM
WINDOW = 128
ROT_DIM = HEAD_DIM // 4
ROPE_THETA = 500000.0
IN_WIDTH = POOL_WIDTH + ATTN_WIDTH + 2 * KV_WIDTH
N_GROUPS = 4
EXPERTS_PER_GROUP = 4
N_EXPERTS = 16
D_EXPERT = 256
EPS = 1e-6
NEG_INF = -1e30
LOG2_E = 1.4426950408889634

LANES = 128
SUBLANES = 8
KV_EXP = 4 * LANES
TQ_IN = 2048
RB_IN = 256
TQ_ATT = 512
KEY_ROWS = 2 * (2 * WINDOW + N_META)
KEY_PAD = -(-KEY_ROWS // LANES) * LANES
OUT_BLOCKS = 2
ROUTE_ROWS = OUT_BLOCKS * WINDOW
N_ITEMS = (TQ_ATT // WINDOW) * N_KV_HEADS
TM_MOE = 512
SUB = 144
BF16_ROWS = 16
MXU_DEPTH = 256
YS_ROWS = -(-((TM_MOE + N_GROUPS * (SUB - 1)) // SUB * SUB) // MXU_DEPTH) * MXU_DEPTH
YS_SHORT = YS_ROWS - MXU_DEPTH
GID_LANE = N_EXPERTS
RANK_LANE = 24
LO_SHIFT = 32
D_GROUP = EXPERTS_PER_GROUP * D_EXPERT
VMEM_LIMIT = 56 * 1024 * 1024

BF16 = jnp.bfloat16
F32 = jnp.float32


def _rope_tables(n_pos):
    half = ROT_DIM // 2
    inv_freq = 1.0 / (ROPE_THETA ** (np.arange(half, dtype=np.float64) / half))
    ang = np.arange(n_pos, dtype=np.float64)[:, None] * inv_freq[None, :]
    cos, sin = np.cos(ang), np.sin(ang)
    c = np.ones((n_pos, HEAD_DIM)); c[:, :half] = cos; c[:, half:ROT_DIM] = cos
    sa = np.zeros((n_pos, HEAD_DIM)); sa[:, :half] = -sin
    sb = np.zeros((n_pos, HEAD_DIM)); sb[:, half:ROT_DIM] = sin
    tab = np.concatenate([np.tile(c, (1, 2)), np.tile(sa, (1, 2)), np.tile(sb, (1, 2))], axis=1)
    return jnp.asarray(tab, dtype=F32)


def _rms_unit(x):
    ms = jnp.mean(x * x, axis=-1, keepdims=True)
    return x * lax.rsqrt(ms + EPS)


def _rms_rows(x, gain):
    return _rms_unit(x) * gain


def _head_norm_rope_many(xs, gains, rope):
    lo = lax.broadcasted_iota(jnp.int32, xs[0].shape, 1) < HEAD_DIM
    sums = []
    for x in xs:
        sq = x * x
        sums.append((jnp.sum(jnp.where(lo, sq, 0.0), axis=-1, keepdims=True),
                     jnp.sum(jnp.where(lo, 0.0, sq), axis=-1, keepdims=True)))
    ys = [x * lax.rsqrt(jnp.where(lo, s_lo, s_hi) + HEAD_DIM * EPS) * g
          for x, g, (s_lo, s_hi) in zip(xs, gains, sums)]
    half = ROT_DIM // 2
    rolled = [(pltpu.roll(y, LANES - half, 1), pltpu.roll(y, half, 1)) for y in ys]
    c, sa, sb = rope[:, 0:LANES], rope[:, LANES:2 * LANES], rope[:, 2 * LANES:3 * LANES]
    return [y * c + ra * sa + rb * sb for y, (ra, rb) in zip(ys, rolled)]


def _head_norm_rope(xc, gain2, rope):
    return _head_norm_rope_many([xc], [gain2], rope)[0]


def _expand_kv(t):
    lo = lax.broadcasted_iota(jnp.int32, t.shape, 1) < HEAD_DIM
    sw = pltpu.roll(t, HEAD_DIM, 1)
    z = jnp.zeros_like(t)
    return jnp.concatenate([jnp.where(lo, t, z), jnp.where(lo, z, sw),
                            jnp.where(lo, sw, z), jnp.where(lo, z, t)], axis=1)


def _project(x, w_in):
    return jnp.dot(_rms_unit(x).astype(BF16), w_in, preferred_element_type=F32)


def _meta_kernel(meta_ref, win_ref, kgain_ref, rope_ref, u_ref, kd_ref, v_ref):
    proj = _project(meta_ref[...], win_ref[...])
    u_ref[...] = proj[:, :POOL_WIDTH]
    k = proj[:, POOL_WIDTH + ATTN_WIDTH:POOL_WIDTH + ATTN_WIDTH + KV_WIDTH]
    v = proj[:, POOL_WIDTH + ATTN_WIDTH + KV_WIDTH:]
    k = _head_norm_rope(k, kgain_ref[...], rope_ref[...])
    kd_ref[...] = _expand_kv(k).astype(BF16)
    v_ref[...] = v.astype(BF16)


def _in_kernel(x_ref, umeta_ref, win_ref, wpool_ref, pscale_ref, qgain_ref, kgain_ref,
               rope_ref, yp_ref, q_ref, kd_ref, vt_ref, carry_ref, proj_ref):
    j = pl.program_id(1)

    @pl.when(j == 0)
    def _():
        carry_ref[...] = umeta_ref[...]

    sizes = [RB_IN] * (TQ_IN // RB_IN - 1) + [RB_IN // 2] * 2
    blocks = [slice(sum(sizes[:i]), sum(sizes[:i + 1])) for i in range(len(sizes))]
    n_sub = len(blocks)

    def normalise(i):
        return _rms_unit(x_ref[0, blocks[i], :]).astype(BF16)

    def project(i, a):
        proj_ref[blocks[i], :] = jnp.dot(a, win_ref[...], preferred_element_type=F32)

    def finish(i):
        rows = blocks[i]
        rope = rope_ref[rows, :]
        u = proj_ref[rows, :POOL_WIDTH]
        acc = jnp.concatenate([carry_ref[...], u], axis=0)
        for gi, w in enumerate(POOL_WINDOWS):
            lo = gi * POOL_GROUP
            acc = acc[:, POOL_GROUP * (1 if gi else 0):]
            acc = acc + pltpu.roll(acc, w // 2, 0)
            mixed = acc[N_META:, :POOL_GROUP] * (1.0 / w) - u[:, lo:lo + POOL_GROUP]
            y = jnp.dot(mixed.astype(BF16), wpool_ref[gi], preferred_element_type=F32)
            yp_ref[0, rows, lo:lo + POOL_GROUP] = (y * pscale_ref[:, lo:lo + POOL_GROUP]).astype(BF16)
        carry_ref[...] = u[sizes[i] - N_META:, :]

        n_qc = ATTN_WIDTH // LANES
        xs = [proj_ref[rows, POOL_WIDTH + c * LANES:POOL_WIDTH + (c + 1) * LANES] for c in range(n_qc + 1)]
        outs = _head_norm_rope_many(xs, [qgain_ref[...]] * n_qc + [kgain_ref[...]], rope)
        for c in range(n_qc):
            q_ref[0, rows, c * LANES:(c + 1) * LANES] = outs[c].astype(BF16)
        kd_ref[0, rows, :] = _expand_kv(outs[n_qc]).astype(BF16)
        v = proj_ref[rows, POOL_WIDTH + ATTN_WIDTH + KV_WIDTH:]
        vt_ref[0, :, rows] = jnp.transpose(v).astype(BF16)

    a_next = normalise(0)
    for i in range(n_sub):
        project(i, a_next)
        if i + 1 < n_sub:
            a_next = normalise(i + 1)
        if i > 0:
            finish(i - 1)
    finish(n_sub - 1)


def _route(logits, upper):
    n = logits.shape[0]
    lt = jnp.transpose(logits)
    sub = SUBLANES
    row8 = lax.broadcasted_iota(jnp.int32, (sub, n), 0).astype(F32)
    row16 = lax.broadcasted_iota(jnp.int32, (N_EXPERTS, n), 0).astype(F32)
    g_ok = row8 < N_GROUPS
    gl = jnp.where(g_ok, lt[N_EXPERTS:N_EXPERTS + sub, :], NEG_INF)
    gmax = jnp.max(gl, axis=0, keepdims=True)
    gsum = jnp.sum(jnp.where(g_ok, jnp.exp(gl - gmax), 0.0), axis=0, keepdims=True)
    g_prob = 1.0 / gsum
    g_idx = jnp.min(jnp.where(gl == gmax, row8, float(sub)), axis=0, keepdims=True)
    e_lo = g_idx * EXPERTS_PER_GROUP
    emask = (row16 >= e_lo) & (row16 < e_lo + EXPERTS_PER_GROUP)
    el = jnp.where(emask, lt[:N_EXPERTS, :], NEG_INF)
    big = float(N_EXPERTS)
    e1 = jnp.max(el, axis=0, keepdims=True)
    i1 = jnp.min(jnp.where(el == e1, row16, big), axis=0, keepdims=True)
    el2 = jnp.where(row16 == i1, NEG_INF, el)
    e2 = jnp.max(el2, axis=0, keepdims=True)
    i2 = jnp.min(jnp.where(el2 == e2, row16, big), axis=0, keepdims=True)
    t = jnp.exp(e2 - e1)
    w1 = 1.0 / (1.0 + t)
    w2 = t * w1
    gates_t = jnp.where(row16 == i1, w1 * g_prob, 0.0) + jnp.where(row16 == i2, w2 * g_prob, 0.0)
    onehot = jnp.where(row8 == g_idx, 1.0, 0.0)
    earlier = jnp.dot(onehot.astype(BF16), upper, preferred_element_type=F32)
    rank = jnp.sum(onehot * earlier, axis=0, keepdims=True)
    hi = gates_t.astype(BF16).astype(F32)
    lo = (gates_t - hi).astype(BF16).astype(F32)
    gid8 = jnp.broadcast_to(g_idx, (sub, n))
    rank8 = jnp.broadcast_to(rank, (sub, n))
    assert GID_LANE == N_EXPERTS and RANK_LANE == GID_LANE + sub and LO_SHIFT == RANK_LANE + sub
    full_t = jnp.concatenate(
        [hi, gid8, rank8, lo, jnp.zeros((LANES - LO_SHIFT - N_EXPERTS, n), F32)], axis=0)
    rows = jnp.concatenate([g_idx, rank, jnp.zeros((sub - 2, n), F32)], axis=0)
    counts = jnp.sum(onehot, axis=1, keepdims=True)
    return (jnp.transpose(full_t).astype(BF16), rows,
            jnp.broadcast_to(counts, (sub, LANES)).astype(jnp.int32))


def _attn_kernel(n_tiles, tiles_per_row, sink_ref, x_ref, yp_ref, q_ref, kd_ref, vt_ref, kmeta_ref, vtmeta_ref, wout_ref,
                 fgain_ref, wr_ref, upper_ref, h2_ref, m_ref, table_ref, rows_ref, cnt_ref,
                 yattn_ref, kcat_ref, vbd_ref, s_ref, p_ref, inv_ref, logit_ref):
    step = pl.program_id(0)
    tile = jnp.minimum(step, n_tiles - 1)
    j = lax.rem(tile, tiles_per_row)
    slot = lax.rem(step, 2)
    nt = (((1,), (1,)), ((), ()))
    n_r = TQ_ATT // WINDOW
    band = 2 * WINDOW
    m0 = 2 * band

    @pl.when(step == 0)
    def _():
        vbd_ref[...] = jnp.zeros(vbd_ref.shape, BF16)
        p_ref[...] = jnp.zeros(p_ref.shape, BF16)
        yattn_ref[...] = jnp.zeros(yattn_ref.shape, BF16)

    def window_start(r):
        return pl.multiple_of(jnp.maximum(j * TQ_ATT + (r - 1) * WINDOW, 0), WINDOW)

    def stage_scores(r):
        start = window_start(r)
        for g in range(N_KV_HEADS):
            vb = vt_ref[0, g * HEAD_DIM:(g + 1) * HEAD_DIM, pl.ds(start, band)]
            vm = vtmeta_ref[g * HEAD_DIM:(g + 1) * HEAD_DIM, :]
            for hh in range(2):
                col = (2 * g + hh) * LANES
                kcat_ref[r, g, hh * band:(hh + 1) * band, :] = kd_ref[0, pl.ds(start, band), col:col + LANES]
                kcat_ref[r, g, m0 + hh * N_META:m0 + (hh + 1) * N_META, :] = kmeta_ref[:, col:col + LANES]
                rows = slice(hh * HEAD_DIM, (hh + 1) * HEAD_DIM)
                vbd_ref[r, g, rows, hh * band:(hh + 1) * band] = vb
                vbd_ref[r, g, rows, m0 + hh * N_META:m0 + (hh + 1) * N_META] = vm
            rq = slice(r * WINDOW, (r + 1) * WINDOW)
            qq = jnp.concatenate([q_ref[0, rq, (2 * g + ch) * LANES:(2 * g + ch + 1) * LANES]
                                  for ch in range(2)], axis=0)
            s_ref[r * N_KV_HEADS + g] = lax.dot_general(kcat_ref[r, g], qq, nt, preferred_element_type=F32)

    def stage_softmax(r):
        kpos = window_start(r) + lax.broadcasted_iota(jnp.int32, (band, WINDOW), 0)
        qpos = j * TQ_ATT + r * WINDOW + lax.broadcasted_iota(jnp.int32, (band, WINDOW), 1)
        d = qpos - kpos
        bias = jnp.where((d >= 0) & (d < WINDOW), 0.0, NEG_INF)
        bias = jnp.concatenate([bias, bias], axis=1)
        left = lax.broadcasted_iota(jnp.int32, (1, 2 * WINDOW), 1) < WINDOW
        for g in range(N_KV_HEADS):
            it = r * N_KV_HEADS + g
            for hh in range(2):
                sink = jnp.where(left, sink_ref[4 * g + hh], sink_ref[4 * g + 2 + hh])
                s = s_ref[it, hh * band:(hh + 1) * band, :] + bias
                sm = s_ref[it, m0 + hh * N_META:m0 + (hh + 1) * N_META, :]
                mx = jnp.maximum(jnp.max(s, axis=0, keepdims=True), jnp.max(sm, axis=0, keepdims=True))
                mx = jnp.maximum(mx, sink)
                p = jnp.exp2(s - mx)
                pm = jnp.exp2(sm - mx)
                den = (jnp.sum(p, axis=0, keepdims=True) + jnp.sum(pm, axis=0, keepdims=True)
                       + jnp.exp2(sink - mx))
                p_ref[it, hh * band:(hh + 1) * band, :] = p.astype(BF16)
                p_ref[it, m0 + hh * N_META:m0 + (hh + 1) * N_META, :] = pm.astype(BF16)
                inv_ref[it, hh * HEAD_DIM:(hh + 1) * HEAD_DIM, :] = jnp.broadcast_to(
                    1.0 / den, (HEAD_DIM, 2 * WINDOW))

    def stage_values(r):
        for g in range(N_KV_HEADS):
            it = r * N_KV_HEADS + g
            o_t = jnp.dot(vbd_ref[r, g], p_ref[it], preferred_element_type=F32) * inv_ref[it]
            for ch in range(2):
                c = 2 * g + ch
                yattn_ref[slot, r * WINDOW:(r + 1) * WINDOW, c * LANES:(c + 1) * LANES] = (
                    jnp.transpose(o_t[:, ch * WINDOW:(ch + 1) * WINDOW]).astype(BF16))

    def stage_out(p):
        rows = slice(p * ROUTE_ROWS, (p + 1) * ROUTE_ROWS)
        mix = (jnp.dot(yp_ref[0, rows, :], wout_ref[:POOL_WIDTH, :], preferred_element_type=F32)
               + jnp.dot(yattn_ref[1 - slot, rows, :], wout_ref[POOL_WIDTH:, :],
                         preferred_element_type=F32))
        h2 = x_ref[0, rows, :] + mix
        h2_ref[0, rows, :] = h2
        m = _rms_rows(h2, fgain_ref[...]).astype(BF16)
        m_ref[0, rows, :] = m
        logit_ref[p] = jnp.dot(m, wr_ref[...], preferred_element_type=F32)

    def stage_route(p):
        table, rows, counts = _route(logit_ref[p], upper_ref[...])
        table_ref[0, p * ROUTE_ROWS:(p + 1) * ROUTE_ROWS, :] = table
        rows_ref[p] = rows
        cnt_ref[p] = counts

    n_p = TQ_ATT // ROUTE_ROWS
    for t in range(max(n_r + 2, 2 * n_p)):
        for k, stage in enumerate((stage_scores, stage_softmax, stage_values)):
            if 0 <= t - k < n_r:
                stage(t - k)
        if t % 2 == 0 and t // 2 < n_p:
            stage_out(t // 2)
        if t % 2 == 1 and t // 2 < n_p:
            stage_route(t // 2)


def _moe_kernel(cnt_ref, m_ref, table_ref, rows_ref, h2_ref, wg_ref, wu_ref, wd_ref, out_ref,
                ys_ref, hid_ref):
    i = pl.program_id(0)
    per = TM_MOE // ROUTE_ROWS
    counts = [[cnt_ref[(i * per + h) * N_GROUPS + g] for g in range(N_GROUPS)] for h in range(per)]
    nsub, base = [], []
    total = 0
    for g in range(N_GROUPS):
        n_g = sum(counts[h][g] for h in range(per))
        base.append(total * SUB)
        nsub.append(lax.div(n_g + (SUB - 1), SUB))
        total = total + nsub[-1]
    first = [[base[g] + sum(counts[hh][g] for hh in range(h)) for g in range(N_GROUPS)] for h in range(per)]

    @pl.when(i == 0)
    def _():
        ys_ref[...] = jnp.zeros(ys_ref.shape, BF16)

    def pick(gid, offsets):
        out = _as_f32(offsets[N_GROUPS - 1])
        for g in range(N_GROUPS - 2, -1, -1):
            out = jnp.where(gid == g, _as_f32(offsets[g]), out)
        return out

    table = table_ref[...]
    table_f = table.astype(F32)
    gid_col = table_f[:, GID_LANE:GID_LANE + 1]
    row_id = lax.broadcasted_iota(jnp.int32, (TM_MOE, 1), 0)
    off_col = pick(gid_col, first[per - 1])
    for h in range(per - 2, -1, -1):
        off_col = jnp.where(row_id < (h + 1) * ROUTE_ROWS, pick(gid_col, first[h]), off_col)
    pos_col = table_f[:, RANK_LANE:RANK_LANE + 1] + off_col
    pos_row = jnp.concatenate(
        [rows_ref[h, 1:2, :] + pick(rows_ref[h, 0:1, :], first[h]) for h in range(per)], axis=1)

    for g in range(N_GROUPS):
        def body(c, carry, g=g):
            r0 = pl.multiple_of(base[g] + c * SUB, BF16_ROWS)
            rows = (r0 + lax.broadcasted_iota(jnp.int32, (SUB, 1), 0)).astype(F32)
            perm = jnp.where(pos_row == rows, 1.0, 0.0).astype(BF16)
            xs = jnp.dot(perm, m_ref[...], preferred_element_type=F32).astype(BF16)
            gs = jnp.dot(perm, table, preferred_element_type=F32)
            for jj in range(EXPERTS_PER_GROUP):
                e = g * EXPERTS_PER_GROUP + jj
                gt = jnp.dot(xs, wg_ref[e], preferred_element_type=F32)
                up = jnp.dot(xs, wu_ref[e], preferred_element_type=F32)
                gate = gs[:, e:e + 1] + gs[:, LO_SHIFT + e:LO_SHIFT + e + 1]
                hid = gt * (1.0 / (1.0 + jnp.exp(-gt))) * up * gate
                hid_ref[:, jj * D_EXPERT:(jj + 1) * D_EXPERT] = hid.astype(BF16)
            y = jnp.dot(hid_ref[...], wd_ref[g * D_GROUP:(g + 1) * D_GROUP, :],
                        preferred_element_type=F32)
            ys_ref[pl.ds(r0, SUB), :] = y.astype(BF16)
            return carry
        lax.fori_loop(0, nsub[g], body, 0)

    def scatter(n_rows):
        cols = lax.broadcasted_iota(jnp.int32, (1, n_rows), 1).astype(F32)
        perm_t = jnp.where(pos_col == cols, 1.0, 0.0).astype(BF16)
        out_ref[...] = h2_ref[...] + jnp.dot(perm_t, ys_ref[:n_rows, :], preferred_element_type=F32)

    short = total * SUB <= YS_SHORT

    @pl.when(short)
    def _():
        scatter(YS_SHORT)

    @pl.when(jnp.logical_not(short))
    def _():
        scatter(YS_ROWS)


def _as_f32(v):
    return float(v) if isinstance(v, int) else v.astype(F32)


def _const_spec(shape):
    n = len(shape)
    return pl.BlockSpec(shape, lambda *_: (0,) * n)


def _resident_spec(shape):
    n = len(shape)
    return pl.BlockSpec(shape, lambda *_: (0,) * n, pipeline_mode=pl.Buffered(1))


def kernel(x, meta_tokens, attn_norm_gain, w_in, w_pool, pool_scale, q_norm_gain, k_norm_gain,
           attn_sinks, w_out, ffn_norm_gain, w_group_router, w_expert_router, w_gate, w_up, w_down):
    B, S, D = x.shape
    assert D == D_MODEL and S % TQ_IN == 0 and S % TQ_ATT == 0 and (B * S) % TM_MOE == 0
    assert w_in.shape[0] == 1, "single layer"
    T = B * S

    rope = _rope_tables(N_META + S)
    fgain = ffn_norm_gain[0][None, :]
    qgain2 = jnp.tile(q_norm_gain[0] * LOG2_E, 2)[None, :]
    kgain2 = jnp.tile(k_norm_gain[0] * (HEAD_DIM ** 0.5), 2)[None, :]
    pscale = pool_scale[0][None, :]
    w_in_b = (attn_norm_gain[0][:, None] * w_in[0]).astype(BF16)
    w_pool_b = w_pool[0].astype(BF16)
    w_out_b = w_out[0].astype(BF16)
    w_r = jnp.concatenate(
        [w_expert_router[0], w_group_router[0],
         jnp.zeros((D, LANES - N_EXPERTS - N_GROUPS), F32)], axis=1).astype(BF16)
    wg = w_gate[0].astype(BF16)
    wu = w_up[0].astype(BF16)
    wd = w_down[0].astype(BF16).reshape(N_EXPERTS * D_EXPERT, D)
    params = pltpu.CompilerParams(vmem_limit_bytes=VMEM_LIMIT)

    u_meta, kd_meta, v_meta = pl.pallas_call(
        _meta_kernel,
        out_shape=(jax.ShapeDtypeStruct((N_META, POOL_WIDTH), F32),
                   jax.ShapeDtypeStruct((N_META, KV_EXP), BF16),
                   jax.ShapeDtypeStruct((N_META, KV_WIDTH), BF16)),
        compiler_params=params,
        name="meta_proj",
    )(meta_tokens, w_in_b, kgain2, rope[:N_META])
    vt_meta = v_meta.T

    yp, q, kd, vt = pl.pallas_call(
        _in_kernel,
        grid=(B, S // TQ_IN),
        in_specs=[
            pl.BlockSpec((1, TQ_IN, D), lambda b, j: (b, j, 0)),
            _const_spec((N_META, POOL_WIDTH)),
            _const_spec((D, IN_WIDTH)),
            _const_spec((len(POOL_WINDOWS), POOL_GROUP, POOL_GROUP)),
            _const_spec((1, POOL_WIDTH)),
            _const_spec((1, LANES)),
            _const_spec((1, LANES)),
            pl.BlockSpec((TQ_IN, 3 * LANES), lambda b, j: (j, 0)),
        ],
        out_specs=(
            pl.BlockSpec((1, TQ_IN, POOL_WIDTH), lambda b, j: (b, j, 0)),
            pl.BlockSpec((1, TQ_IN, ATTN_WIDTH), lambda b, j: (b, j, 0)),
            pl.BlockSpec((1, TQ_IN, KV_EXP), lambda b, j: (b, j, 0)),
            pl.BlockSpec((1, KV_WIDTH, TQ_IN), lambda b, j: (b, 0, j)),
        ),
        out_shape=(jax.ShapeDtypeStruct((B, S, POOL_WIDTH), BF16),
                   jax.ShapeDtypeStruct((B, S, ATTN_WIDTH), BF16),
                   jax.ShapeDtypeStruct((B, S, KV_EXP), BF16),
                   jax.ShapeDtypeStruct((B, KV_WIDTH, S), BF16)),
        scratch_shapes=[pltpu.VMEM((N_META, POOL_WIDTH), F32),
                        pltpu.VMEM((TQ_IN, IN_WIDTH), F32)],
        compiler_params=pltpu.CompilerParams(
            dimension_semantics=("arbitrary", "arbitrary"), vmem_limit_bytes=VMEM_LIMIT),
        name="in_proj",
    )(x, u_meta, w_in_b, w_pool_b, pscale, qgain2, kgain2, rope[N_META:])

    upper = jnp.asarray(np.triu(np.ones((ROUTE_ROWS, ROUTE_ROWS), np.float32), 1), dtype=BF16)
    per_row = S // TQ_ATT
    n_tiles = B * per_row

    def cur(s):
        t = jnp.minimum(s, n_tiles - 1)
        return t // per_row, t % per_row

    def lag(s):
        t = jnp.maximum(s - 1, 0)
        return t // per_row, t % per_row

    def lag_rows(s):
        return (*lag(s), 0)

    def lag_flat(s):
        return (jnp.maximum(s - 1, 0), 0, 0)

    h2, m, table, rows, cnt = pl.pallas_call(
        functools.partial(_attn_kernel, n_tiles, per_row),
        grid=(n_tiles + 1,),
        in_specs=[
            pl.BlockSpec(memory_space=pltpu.SMEM),
            pl.BlockSpec((1, TQ_ATT, D), lag_rows),
            pl.BlockSpec((1, TQ_ATT, POOL_WIDTH), lag_rows),
            pl.BlockSpec((1, TQ_ATT, ATTN_WIDTH), lambda s: (*cur(s), 0)),
            pl.BlockSpec((1, S, KV_EXP), lambda s: (cur(s)[0], 0, 0)),
            pl.BlockSpec((1, KV_WIDTH, S), lambda s: (cur(s)[0], 0, 0)),
            _const_spec((N_META, KV_EXP)),
            _const_spec((KV_WIDTH, N_META)),
            _const_spec((D, D)),
            _const_spec((1, D)),
            _const_spec((D, LANES)),
            _const_spec((ROUTE_ROWS, ROUTE_ROWS)),
        ],
        out_specs=(
            pl.BlockSpec((1, TQ_ATT, D), lag_rows),
            pl.BlockSpec((1, TQ_ATT, D), lag_rows),
            pl.BlockSpec((1, TQ_ATT, LANES), lag_rows),
            pl.BlockSpec((TQ_ATT // ROUTE_ROWS, SUBLANES, ROUTE_ROWS), lag_flat),
            pl.BlockSpec((TQ_ATT // ROUTE_ROWS, SUBLANES, LANES), lag_flat),
        ),
        out_shape=(jax.ShapeDtypeStruct((B, S, D), F32),
                   jax.ShapeDtypeStruct((B, S, D), BF16),
                   jax.ShapeDtypeStruct((B, S, LANES), BF16),
                   jax.ShapeDtypeStruct((T // ROUTE_ROWS, SUBLANES, ROUTE_ROWS), F32),
                   jax.ShapeDtypeStruct((T // ROUTE_ROWS, SUBLANES, LANES), jnp.int32)),
        scratch_shapes=[
            pltpu.VMEM((2, TQ_ATT, ATTN_WIDTH), BF16),
            pltpu.VMEM((TQ_ATT // WINDOW, N_KV_HEADS, KEY_ROWS, LANES), BF16),
            pltpu.VMEM((TQ_ATT // WINDOW, N_KV_HEADS, LANES, KEY_PAD), BF16),
            pltpu.VMEM((N_ITEMS, KEY_ROWS, 2 * LANES), F32),
            pltpu.VMEM((N_ITEMS, KEY_PAD, 2 * LANES), BF16),
            pltpu.VMEM((N_ITEMS, LANES, 2 * LANES), F32),
            pltpu.VMEM((TQ_ATT // ROUTE_ROWS, ROUTE_ROWS, LANES), F32),
        ],
        compiler_params=pltpu.CompilerParams(
            dimension_semantics=("arbitrary",), vmem_limit_bytes=VMEM_LIMIT),
        name="attn_out",
    )(attn_sinks[0] * LOG2_E, x, yp, q, kd, vt, kd_meta, vt_meta, w_out_b, fgain, w_r, upper)

    cnt_flat = cnt[:, :N_GROUPS, 0].reshape(-1)
    out = pl.pallas_call(
        _moe_kernel,
        grid_spec=pltpu.PrefetchScalarGridSpec(
            num_scalar_prefetch=1,
            grid=(T // TM_MOE,),
            in_specs=[
                pl.BlockSpec((TM_MOE, D), lambda i, c: (i, 0)),
                pl.BlockSpec((TM_MOE, LANES), lambda i, c: (i, 0)),
                pl.BlockSpec((TM_MOE // ROUTE_ROWS, SUBLANES, ROUTE_ROWS), lambda i, c: (i, 0, 0)),
                pl.BlockSpec((TM_MOE, D), lambda i, c: (i, 0)),
                _resident_spec((N_EXPERTS, D, D_EXPERT)),
                _resident_spec((N_EXPERTS, D, D_EXPERT)),
                _resident_spec((N_EXPERTS * D_EXPERT, D)),
            ],
            out_specs=pl.BlockSpec((TM_MOE, D), lambda i, c: (i, 0)),
            scratch_shapes=[pltpu.VMEM((YS_ROWS, D), BF16),
                            pltpu.VMEM((SUB, D_GROUP), BF16)],
        ),
        out_shape=jax.ShapeDtypeStruct((T, D), F32),
        compiler_params=pltpu.CompilerParams(
            dimension_semantics=("arbitrary",), vmem_limit_bytes=VMEM_LIMIT),
        name="moe",
    )(cnt_flat, m.reshape(T, D), table.reshape(T, LANES), rows, h2.reshape(T, D), wg, wu, wd)
    return out.reshape(B, S, D)
```

```python
import functools

import numpy as np
import jax
import jax.numpy as jnp
from jax import lax
from jax.experimental import pallas as pl
from jax.experimental.pallas import tpu as pltpu

D_MODEL = 1024
N_META = 16
POOL_WIDTH = 512
POOL_WINDOWS = (2, 4, 8, 16)
POOL_GROUP = 128
HEAD_DIM = 64
N_HEADS = 8
N_KV_HEADS = 2
ATTN_WIDTH = N_HEADS * HEAD_DIM
KV_WIDTH = N_KV_HEADS * HEAD_DIM
WINDOW = 128
ROT_DIM = HEAD_DIM // 4
ROPE_THETA = 500000.0
IN_WIDTH = POOL_WIDTH + ATTN_WIDTH + 2 * KV_WIDTH
N_GROUPS = 4
EXPERTS_PER_GROUP = 4
N_EXPERTS = 16
D_EXPERT = 256
EPS = 1e-6
NEG_INF = -1e30
LOG2_E = 1.4426950408889634

LANES = 128
SUBLANES = 8
KV_EXP = 4 * LANES
TQ_IN = 2048
RB_IN = 256
TQ_ATT = 512
KEY_ROWS = 2 * (2 * WINDOW + N_META)
KEY_PAD = -(-KEY_ROWS // LANES) * LANES
OUT_BLOCKS = 2
ROUTE_ROWS = OUT_BLOCKS * WINDOW
N_ITEMS = (TQ_ATT // WINDOW) * N_KV_HEADS
TM_MOE = 512
SUB = 144
BF16_ROWS = 16
MXU_DEPTH = 256
YS_ROWS = -(-((TM_MOE + N_GROUPS * (SUB - 1)) // SUB * SUB) // MXU_DEPTH) * MXU_DEPTH
YS_SHORT = YS_ROWS - MXU_DEPTH
GID_LANE = N_EXPERTS
RANK_LANE = 24
LO_SHIFT = 32
D_GROUP = EXPERTS_PER_GROUP * D_EXPERT
VMEM_LIMIT = 56 * 1024 * 1024

BF16 = jnp.bfloat16
F32 = jnp.float32


def _rope_tables(n_pos):
    half = ROT_DIM // 2
    inv_freq = 1.0 / (ROPE_THETA ** (np.arange(half, dtype=np.float64) / half))
    ang = np.arange(n_pos, dtype=np.float64)[:, None] * inv_freq[None, :]
    cos, sin = np.cos(ang), np.sin(ang)
    c = np.ones((n_pos, HEAD_DIM)); c[:, :half] = cos; c[:, half:ROT_DIM] = cos
    sa = np.zeros((n_pos, HEAD_DIM)); sa[:, :half] = -sin
    sb = np.zeros((n_pos, HEAD_DIM)); sb[:, half:ROT_DIM] = sin
    tab = np.concatenate([np.tile(c, (1, 2)), np.tile(sa, (1, 2)), np.tile(sb, (1, 2))], axis=1)
    return jnp.asarray(tab, dtype=F32)


def _rms_unit(x):
    ms = jnp.mean(x * x, axis=-1, keepdims=True)
    return x * lax.rsqrt(ms + EPS)


def _head_norm_rope_many(xs, gains, rope):
    lo = lax.broadcasted_iota(jnp.int32, xs[0].shape, 1) < HEAD_DIM
    sums = []
    for x in xs:
        sq = x * x
        sums.append((jnp.sum(jnp.where(lo, sq, 0.0), axis=-1, keepdims=True),
                     jnp.sum(jnp.where(lo, 0.0, sq), axis=-1, keepdims=True)))
    ys = [x * lax.rsqrt(jnp.where(lo, s_lo, s_hi) + HEAD_DIM * EPS) * g
          for x, g, (s_lo, s_hi) in zip(xs, gains, sums)]
    half = ROT_DIM // 2
    rolled = [(pltpu.roll(y, LANES - half, 1), pltpu.roll(y, half, 1)) for y in ys]
    c, sa, sb = rope[:, 0:LANES], rope[:, LANES:2 * LANES], rope[:, 2 * LANES:3 * LANES]
    return [y * c + ra * sa + rb * sb for y, (ra, rb) in zip(ys, rolled)]


def _head_norm_rope(xc, gain2, rope):
    return _head_norm_rope_many([xc], [gain2], rope)[0]


def _expand_kv(t):
    lo = lax.broadcasted_iota(jnp.int32, t.shape, 1) < HEAD_DIM
    sw = pltpu.roll(t, HEAD_DIM, 1)
    z = jnp.zeros_like(t)
    return jnp.concatenate([jnp.where(lo, t, z), jnp.where(lo, z, sw),
                            jnp.where(lo, sw, z), jnp.where(lo, z, t)], axis=1)


def _project(x, w_in):
    return jnp.dot(_rms_unit(x).astype(BF16), w_in, preferred_element_type=F32)


def _meta_kernel(meta_ref, win_ref, kgain_ref, rope_ref, u_ref, kd_ref, v_ref):
    proj = _project(meta_ref[...], win_ref[...])
    u_ref[...] = proj[:, :POOL_WIDTH]
    k = proj[:, POOL_WIDTH + ATTN_WIDTH:POOL_WIDTH + ATTN_WIDTH + KV_WIDTH]
    v = proj[:, POOL_WIDTH + ATTN_WIDTH + KV_WIDTH:]
    k = _head_norm_rope(k, kgain_ref[...], rope_ref[...])
    kd_ref[...] = _expand_kv(k).astype(BF16)
    v_ref[...] = v.astype(BF16)


def _in_kernel(x_ref, umeta_ref, win_ref, wpool_ref, pscale_ref, qgain_ref, kgain_ref,
               rope_ref, yp_ref, q_ref, kd_ref, vt_ref, carry_ref, proj_ref):
    j = pl.program_id(1)

    @pl.when(j == 0)
    def _():
        carry_ref[...] = umeta_ref[...]

    sizes = [RB_IN] * (TQ_IN // RB_IN - 1) + [RB_IN // 2] * 2
    blocks = [slice(sum(sizes[:i]), sum(sizes[:i + 1])) for i in range(len(sizes))]
    n_sub = len(blocks)

    def normalise(i):
        return _rms_unit(x_ref[0, blocks[i], :]).astype(BF16)

    def project(i, a):
        proj_ref[blocks[i], :] = jnp.dot(a, win_ref[...], preferred_element_type=F32)

    def finish(i):
        rows = blocks[i]
        rope = rope_ref[rows, :]
        u = proj_ref[rows, :POOL_WIDTH]
        acc = jnp.concatenate([carry_ref[...], u], axis=0)
        for gi, w in enumerate(POOL_WINDOWS):
            lo = gi * POOL_GROUP
            acc = acc[:, POOL_GROUP * (1 if gi else 0):]
            acc = acc + pltpu.roll(acc, w // 2, 0)
            mixed = acc[N_META:, :POOL_GROUP] * (1.0 / w) - u[:, lo:lo + POOL_GROUP]
            y = jnp.dot(mixed.astype(BF16), wpool_ref[gi], preferred_element_type=F32)
            yp_ref[0, rows, lo:lo + POOL_GROUP] = (y * pscale_ref[:, lo:lo + POOL_GROUP]).astype(BF16)
        carry_ref[...] = u[sizes[i] - N_META:, :]

        n_qc = ATTN_WIDTH // LANES
        xs = [proj_ref[rows, POOL_WIDTH + c * LANES:POOL_WIDTH + (c + 1) * LANES] for c in range(n_qc + 1)]
        outs = _head_norm_rope_many(xs, [qgain_ref[...]] * n_qc + [kgain_ref[...]], rope)
        for c in range(n_qc):
            q_ref[0, rows, c * LANES:(c + 1) * LANES] = outs[c].astype(BF16)
        kd_ref[0, rows, :] = _expand_kv(outs[n_qc]).astype(BF16)
        v = proj_ref[rows, POOL_WIDTH + ATTN_WIDTH + KV_WIDTH:]
        vt_ref[0, :, rows] = jnp.transpose(v).astype(BF16)

    a_next = normalise(0)
    for i in range(n_sub):
        project(i, a_next)
        if i + 1 < n_sub:
            a_next = normalise(i + 1)
        if i > 0:
            finish(i - 1)
    finish(n_sub - 1)


def _route(logits, upper):
    n = logits.shape[0]
    lt = jnp.transpose(logits)
    sub = SUBLANES
    row8 = lax.broadcasted_iota(jnp.int32, (sub, n), 0).astype(F32)
    row16 = lax.broadcasted_iota(jnp.int32, (N_EXPERTS, n), 0).astype(F32)
    g_ok = row8 < N_GROUPS
    gl = jnp.where(g_ok, lt[N_EXPERTS:N_EXPERTS + sub, :], NEG_INF)
    gmax = jnp.max(gl, axis=0, keepdims=True)
    gsum = jnp.sum(jnp.where(g_ok, jnp.exp(gl - gmax), 0.0), axis=0, keepdims=True)
    g_prob = 1.0 / gsum
    g_idx = jnp.min(jnp.where(gl == gmax, row8, float(sub)), axis=0, keepdims=True)
    e_lo = g_idx * EXPERTS_PER_GROUP
    emask = (row16 >= e_lo) & (row16 < e_lo + EXPERTS_PER_GROUP)
    el = jnp.where(emask, lt[:N_EXPERTS, :], NEG_INF)
    big = float(N_EXPERTS)
    e1 = jnp.max(el, axis=0, keepdims=True)
    i1 = jnp.min(jnp.where(el == e1, row16, big), axis=0, keepdims=True)
    el2 = jnp.where(row16 == i1, NEG_INF, el)
    e2 = jnp.max(el2, axis=0, keepdims=True)
    i2 = jnp.min(jnp.where(el2 == e2, row16, big), axis=0, keepdims=True)
    t = jnp.exp(e2 - e1)
    w1 = 1.0 / (1.0 + t)
    w2 = t * w1
    gates_t = jnp.where(row16 == i1, w1 * g_prob, 0.0) + jnp.where(row16 == i2, w2 * g_prob, 0.0)
    onehot = jnp.where(row8 == g_idx, 1.0, 0.0)
    earlier = jnp.dot(onehot.astype(BF16), upper, preferred_element_type=F32)
    rank = jnp.sum(onehot * earlier, axis=0, keepdims=True)
    hi = gates_t.astype(BF16).astype(F32)
    lo = (gates_t - hi).astype(BF16).astype(F32)
    gid8 = jnp.broadcast_to(g_idx, (sub, n))
    rank8 = jnp.broadcast_to(rank, (sub, n))
    assert GID_LANE == N_EXPERTS and RANK_LANE == GID_LANE + sub and LO_SHIFT == RANK_LANE + sub
    full_t = jnp.concatenate(
        [hi, gid8, rank8, lo, jnp.zeros((LANES - LO_SHIFT - N_EXPERTS, n), F32)], axis=0)
    rows = jnp.concatenate([g_idx, rank, jnp.zeros((sub - 2, n), F32)], axis=0)
    counts = jnp.sum(onehot, axis=1, keepdims=True)
    return (jnp.transpose(full_t).astype(BF16), rows,
            jnp.broadcast_to(counts, (sub, LANES)).astype(jnp.int32))


def _fold_rows(x, pair, final):
    rows = x.shape[0]
    while rows > SUBLANES:
        rows //= 2
        x = pair(x[:rows], x[rows:])
    return final(x, axis=0, keepdims=True)


def _attn_kernel(n_tiles, tiles_per_row, sink_ref, x_ref, yp_ref, q_ref, kd_ref, vt_ref, kmeta_ref, vtmeta_ref, wout_ref,
                 wr_ref, upper_ref, h2_ref, m_ref, table_ref, rows_ref, cnt_ref,
                 yattn_ref, kcat_ref, vbd_ref, s_ref, p_ref, inv_ref, logit_ref):
    step = pl.program_id(0)
    tile = jnp.minimum(step, n_tiles - 1)
    j = lax.rem(tile, tiles_per_row)
    slot = lax.rem(step, 2)
    nt = (((1,), (1,)), ((), ()))
    n_r = TQ_ATT // WINDOW
    band = 2 * WINDOW
    m0 = 2 * band

    @pl.when(step == 0)
    def _():
        vbd_ref[...] = jnp.zeros(vbd_ref.shape, BF16)
        p_ref[...] = jnp.zeros(p_ref.shape, BF16)
        yattn_ref[...] = jnp.zeros(yattn_ref.shape, BF16)

    def window_start(r):
        return pl.multiple_of(jnp.maximum(j * TQ_ATT + (r - 1) * WINDOW, 0), WINDOW)

    def stage_scores(r):
        start = window_start(r)
        for g in range(N_KV_HEADS):
            vb = vt_ref[0, g * HEAD_DIM:(g + 1) * HEAD_DIM, pl.ds(start, band)]
            vm = vtmeta_ref[g * HEAD_DIM:(g + 1) * HEAD_DIM, :]
            for hh in range(2):
                col = (2 * g + hh) * LANES
                kcat_ref[r, g, hh * band:(hh + 1) * band, :] = kd_ref[0, pl.ds(start, band), col:col + LANES]
                kcat_ref[r, g, m0 + hh * N_META:m0 + (hh + 1) * N_META, :] = kmeta_ref[:, col:col + LANES]
                rows = slice(hh * HEAD_DIM, (hh + 1) * HEAD_DIM)
                vbd_ref[r, g, rows, hh * band:(hh + 1) * band] = vb
                vbd_ref[r, g, rows, m0 + hh * N_META:m0 + (hh + 1) * N_META] = vm
            rq = slice(r * WINDOW, (r + 1) * WINDOW)
            qq = jnp.concatenate([q_ref[0, rq, (2 * g + ch) * LANES:(2 * g + ch + 1) * LANES]
                                  for ch in range(2)], axis=0)
            s_ref[r * N_KV_HEADS + g] = lax.dot_general(kcat_ref[r, g], qq, nt, preferred_element_type=F32)

    def stage_softmax(r):
        kpos = window_start(r) + lax.broadcasted_iota(jnp.int32, (band, WINDOW), 0)
        qpos = j * TQ_ATT + r * WINDOW + lax.broadcasted_iota(jnp.int32, (band, WINDOW), 1)
        d = qpos - kpos
        bias = jnp.where((d >= 0) & (d < WINDOW), 0.0, NEG_INF)
        bias = jnp.concatenate([bias, bias], axis=1)
        left = lax.broadcasted_iota(jnp.int32, (1, 2 * WINDOW), 1) < WINDOW
        for g in range(N_KV_HEADS):
            it = r * N_KV_HEADS + g
            for hh in range(2):
                sink = jnp.where(left, sink_ref[4 * g + hh], sink_ref[4 * g + 2 + hh])
                s = s_ref[it, hh * band:(hh + 1) * band, :] + bias
                sm = s_ref[it, m0 + hh * N_META:m0 + (hh + 1) * N_META, :]
                mx = jnp.maximum(_fold_rows(s, jnp.maximum, jnp.max), jnp.max(sm, axis=0, keepdims=True))
                mx = jnp.maximum(mx, sink)
                p = jnp.exp2(s - mx)
                pm = jnp.exp2(sm - mx)
                den = (_fold_rows(p, jnp.add, jnp.sum) + jnp.sum(pm, axis=0, keepdims=True)
                       + jnp.exp2(sink - mx))
                p_ref[it, hh * band:(hh + 1) * band, :] = p.astype(BF16)
                p_ref[it, m0 + hh * N_META:m0 + (hh + 1) * N_META, :] = pm.astype(BF16)
                inv_ref[it, hh * HEAD_DIM:(hh + 1) * HEAD_DIM, :] = jnp.broadcast_to(
                    1.0 / den, (HEAD_DIM, 2 * WINDOW))

    def stage_values(r):
        for g in range(N_KV_HEADS):
            it = r * N_KV_HEADS + g
            o_t = jnp.dot(vbd_ref[r, g], p_ref[it], preferred_element_type=F32) * inv_ref[it]
            for ch in range(2):
                c = 2 * g + ch
                yattn_ref[slot, r * WINDOW:(r + 1) * WINDOW, c * LANES:(c + 1) * LANES] = (
                    jnp.transpose(o_t[:, ch * WINDOW:(ch + 1) * WINDOW]).astype(BF16))

    def stage_out():
        mix = (jnp.dot(yp_ref[0], wout_ref[:POOL_WIDTH, :], preferred_element_type=F32)
               + jnp.dot(yattn_ref[1 - slot], wout_ref[POOL_WIDTH:, :], preferred_element_type=F32))
        h2 = x_ref[0] + mix
        h2_ref[0] = h2
        m = _rms_unit(h2).astype(BF16)
        m_ref[0] = m
        logits = jnp.dot(m, wr_ref[...], preferred_element_type=F32)
        for p in range(TQ_ATT // ROUTE_ROWS):
            logit_ref[p] = logits[p * ROUTE_ROWS:(p + 1) * ROUTE_ROWS, :]

    def stage_route(p):
        table, rows, counts = _route(logit_ref[p], upper_ref[...])
        table_ref[0, p * ROUTE_ROWS:(p + 1) * ROUTE_ROWS, :] = table
        rows_ref[p] = rows
        cnt_ref[p] = counts

    n_p = TQ_ATT // ROUTE_ROWS
    for t in range(max(n_r + 2, n_p + 1)):
        for k, stage in enumerate((stage_scores, stage_softmax, stage_values)):
            if 0 <= t - k < n_r:
                stage(t - k)
        if t == 0:
            stage_out()
        if 1 <= t <= n_p:
            stage_route(t - 1)


def _moe_kernel(cnt_ref, m_ref, table_ref, rows_ref, h2_ref, wg_ref, wu_ref, wd_ref, out_ref,
                ys_ref, hid_ref):
    i = pl.program_id(0)
    per = TM_MOE // ROUTE_ROWS
    counts = [[cnt_ref[(i * per + h) * N_GROUPS + g] for g in range(N_GROUPS)] for h in range(per)]
    nsub, base = [], []
    total = 0
    for g in range(N_GROUPS):
        n_g = sum(counts[h][g] for h in range(per))
        base.append(total * SUB)
        nsub.append(lax.div(n_g + (SUB - 1), SUB))
        total = total + nsub[-1]
    first = [[base[g] + sum(counts[hh][g] for hh in range(h)) for g in range(N_GROUPS)] for h in range(per)]

    @pl.when(i == 0)
    def _():
        ys_ref[...] = jnp.zeros(ys_ref.shape, BF16)

    def pick(gid, offsets):
        out = _as_f32(offsets[N_GROUPS - 1])
        for g in range(N_GROUPS - 2, -1, -1):
            out = jnp.where(gid == g, _as_f32(offsets[g]), out)
        return out

    table = table_ref[...]
    table_f = table.astype(F32)
    gid_col = table_f[:, GID_LANE:GID_LANE + 1]
    row_id = lax.broadcasted_iota(jnp.int32, (TM_MOE, 1), 0)
    off_col = pick(gid_col, first[per - 1])
    for h in range(per - 2, -1, -1):
        off_col = jnp.where(row_id < (h + 1) * ROUTE_ROWS, pick(gid_col, first[h]), off_col)
    pos_col = table_f[:, RANK_LANE:RANK_LANE + 1] + off_col
    pos_row = jnp.concatenate(
        [rows_ref[h, 1:2, :] + pick(rows_ref[h, 0:1, :], first[h]) for h in range(per)], axis=1)

    for g in range(N_GROUPS):
        def body(c, carry, g=g):
            r0 = pl.multiple_of(base[g] + c * SUB, BF16_ROWS)
            rows = (r0 + lax.broadcasted_iota(jnp.int32, (SUB, 1), 0)).astype(F32)
            perm = jnp.where(pos_row == rows, 1.0, 0.0).astype(BF16)
            xs = jnp.dot(perm, m_ref[...], preferred_element_type=F32).astype(BF16)
            gs = jnp.dot(perm, table, preferred_element_type=F32)
            for jj in range(EXPERTS_PER_GROUP):
                e = g * EXPERTS_PER_GROUP + jj
                gt = jnp.dot(xs, wg_ref[e], preferred_element_type=F32)
                up = jnp.dot(xs, wu_ref[e], preferred_element_type=F32)
                gate = gs[:, e:e + 1] + gs[:, LO_SHIFT + e:LO_SHIFT + e + 1]
                hid = gt * (1.0 / (1.0 + jnp.exp(-gt))) * up * gate
                hid_ref[:, jj * D_EXPERT:(jj + 1) * D_EXPERT] = hid.astype(BF16)
            y = jnp.dot(hid_ref[...], wd_ref[g * D_GROUP:(g + 1) * D_GROUP, :],
                        preferred_element_type=F32)
            ys_ref[pl.ds(r0, SUB), :] = y.astype(BF16)
            return carry
        lax.fori_loop(0, nsub[g], body, 0)

    def scatter(n_rows):
        cols = lax.broadcasted_iota(jnp.int32, (1, n_rows), 1).astype(F32)
        perm_t = jnp.where(pos_col == cols, 1.0, 0.0).astype(BF16)
        out_ref[...] = h2_ref[...] + jnp.dot(perm_t, ys_ref[:n_rows, :], preferred_element_type=F32)

    short = total * SUB <= YS_SHORT

    @pl.when(short)
    def _():
        scatter(YS_SHORT)

    @pl.when(jnp.logical_not(short))
    def _():
        scatter(YS_ROWS)


def _as_f32(v):
    return float(v) if isinstance(v, int) else v.astype(F32)


def _const_spec(shape):
    n = len(shape)
    return pl.BlockSpec(shape, lambda *_: (0,) * n)


def _resident_spec(shape):
    n = len(shape)
    return pl.BlockSpec(shape, lambda *_: (0,) * n, pipeline_mode=pl.Buffered(1))


def kernel(x, meta_tokens, attn_norm_gain, w_in, w_pool, pool_scale, q_norm_gain, k_norm_gain,
           attn_sinks, w_out, ffn_norm_gain, w_group_router, w_expert_router, w_gate, w_up, w_down):
    B, S, D = x.shape
    assert D == D_MODEL and S % TQ_IN == 0 and S % TQ_ATT == 0 and (B * S) % TM_MOE == 0
    assert w_in.shape[0] == 1, "single layer"
    T = B * S

    rope = _rope_tables(N_META + S)
    fgain = ffn_norm_gain[0][:, None]
    qgain2 = jnp.tile(q_norm_gain[0] * LOG2_E, 2)[None, :]
    kgain2 = jnp.tile(k_norm_gain[0] * (HEAD_DIM ** 0.5), 2)[None, :]
    pscale = pool_scale[0][None, :]
    w_in_b = (attn_norm_gain[0][:, None] * w_in[0]).astype(BF16)
    w_pool_b = w_pool[0].astype(BF16)
    w_out_b = w_out[0].astype(BF16)
    w_r = (fgain * jnp.concatenate(
        [w_expert_router[0], w_group_router[0],
         jnp.zeros((D, LANES - N_EXPERTS - N_GROUPS), F32)], axis=1)).astype(BF16)
    wg = (fgain[None] * w_gate[0]).astype(BF16)
    wu = (fgain[None] * w_up[0]).astype(BF16)
    wd = w_down[0].astype(BF16).reshape(N_EXPERTS * D_EXPERT, D)
    params = pltpu.CompilerParams(vmem_limit_bytes=VMEM_LIMIT)

    u_meta, kd_meta, v_meta = pl.pallas_call(
        _meta_kernel,
        out_shape=(jax.ShapeDtypeStruct((N_META, POOL_WIDTH), F32),
                   jax.ShapeDtypeStruct((N_META, KV_EXP), BF16),
                   jax.ShapeDtypeStruct((N_META, KV_WIDTH), BF16)),
        compiler_params=params,
        name="meta_proj",
    )(meta_tokens, w_in_b, kgain2, rope[:N_META])
    vt_meta = v_meta.T

    yp, q, kd, vt = pl.pallas_call(
        _in_kernel,
        grid=(B, S // TQ_IN),
        in_specs=[
            pl.BlockSpec((1, TQ_IN, D), lambda b, j: (b, j, 0)),
            _const_spec((N_META, POOL_WIDTH)),
            _const_spec((D, IN_WIDTH)),
            _const_spec((len(POOL_WINDOWS), POOL_GROUP, POOL_GROUP)),
            _const_spec((1, POOL_WIDTH)),
            _const_spec((1, LANES)),
            _const_spec((1, LANES)),
            pl.BlockSpec((TQ_IN, 3 * LANES), lambda b, j: (j, 0)),
        ],
        out_specs=(
            pl.BlockSpec((1, TQ_IN, POOL_WIDTH), lambda b, j: (b, j, 0)),
            pl.BlockSpec((1, TQ_IN, ATTN_WIDTH), lambda b, j: (b, j, 0)),
            pl.BlockSpec((1, TQ_IN, KV_EXP), lambda b, j: (b, j, 0)),
            pl.BlockSpec((1, KV_WIDTH, TQ_IN), lambda b, j: (b, 0, j)),
        ),
        out_shape=(jax.ShapeDtypeStruct((B, S, POOL_WIDTH), BF16),
                   jax.ShapeDtypeStruct((B, S, ATTN_WIDTH), BF16),
                   jax.ShapeDtypeStruct((B, S, KV_EXP), BF16),
                   jax.ShapeDtypeStruct((B, KV_WIDTH, S), BF16)),
        scratch_shapes=[pltpu.VMEM((N_META, POOL_WIDTH), F32),
                        pltpu.VMEM((TQ_IN, IN_WIDTH), F32)],
        compiler_params=pltpu.CompilerParams(
            dimension_semantics=("arbitrary", "arbitrary"), vmem_limit_bytes=VMEM_LIMIT),
        name="in_proj",
    )(x, u_meta, w_in_b, w_pool_b, pscale, qgain2, kgain2, rope[N_META:])

    upper = jnp.asarray(np.triu(np.ones((ROUTE_ROWS, ROUTE_ROWS), np.float32), 1), dtype=BF16)
    per_row = S // TQ_ATT
    n_tiles = B * per_row

    def cur(s):
        t = jnp.minimum(s, n_tiles - 1)
        return t // per_row, t % per_row

    def lag(s):
        t = jnp.maximum(s - 1, 0)
        return t // per_row, t % per_row

    def lag_rows(s):
        return (*lag(s), 0)

    def lag_flat(s):
        return (jnp.maximum(s - 1, 0), 0, 0)

    h2, m, table, rows, cnt = pl.pallas_call(
        functools.partial(_attn_kernel, n_tiles, per_row),
        grid=(n_tiles + 1,),
        in_specs=[
            pl.BlockSpec(memory_space=pltpu.SMEM),
            pl.BlockSpec((1, TQ_ATT, D), lag_rows),
            pl.BlockSpec((1, TQ_ATT, POOL_WIDTH), lag_rows),
            pl.BlockSpec((1, TQ_ATT, ATTN_WIDTH), lambda s: (*cur(s), 0)),
            pl.BlockSpec((1, S, KV_EXP), lambda s: (cur(s)[0], 0, 0)),
            pl.BlockSpec((1, KV_WIDTH, S), lambda s: (cur(s)[0], 0, 0)),
            _const_spec((N_META, KV_EXP)),
            _const_spec((KV_WIDTH, N_META)),
            _const_spec((D, D)),
            _const_spec((D, LANES)),
            _const_spec((ROUTE_ROWS, ROUTE_ROWS)),
        ],
        out_specs=(
            pl.BlockSpec((1, TQ_ATT, D), lag_rows),
            pl.BlockSpec((1, TQ_ATT, D), lag_rows),
            pl.BlockSpec((1, TQ_ATT, LANES), lag_rows),
            pl.BlockSpec((TQ_ATT // ROUTE_ROWS, SUBLANES, ROUTE_ROWS), lag_flat),
            pl.BlockSpec((TQ_ATT // ROUTE_ROWS, SUBLANES, LANES), lag_flat),
        ),
        out_shape=(jax.ShapeDtypeStruct((B, S, D), F32),
                   jax.ShapeDtypeStruct((B, S, D), BF16),
                   jax.ShapeDtypeStruct((B, S, LANES), BF16),
                   jax.ShapeDtypeStruct((T // ROUTE_ROWS, SUBLANES, ROUTE_ROWS), F32),
                   jax.ShapeDtypeStruct((T // ROUTE_ROWS, SUBLANES, LANES), jnp.int32)),
        scratch_shapes=[
            pltpu.VMEM((2, TQ_ATT, ATTN_WIDTH), BF16),
            pltpu.VMEM((TQ_ATT // WINDOW, N_KV_HEADS, KEY_ROWS, LANES), BF16),
            pltpu.VMEM((TQ_ATT // WINDOW, N_KV_HEADS, LANES, KEY_PAD), BF16),
            pltpu.VMEM((N_ITEMS, KEY_ROWS, 2 * LANES), F32),
            pltpu.VMEM((N_ITEMS, KEY_PAD, 2 * LANES), BF16),
            pltpu.VMEM((N_ITEMS, LANES, 2 * LANES), F32),
            pltpu.VMEM((TQ_ATT // ROUTE_ROWS, ROUTE_ROWS, LANES), F32),
        ],
        compiler_params=pltpu.CompilerParams(
            dimension_semantics=("arbitrary",), vmem_limit_bytes=VMEM_LIMIT),
        name="attn_out",
    )(attn_sinks[0] * LOG2_E, x, yp, q, kd, vt, kd_meta, vt_meta, w_out_b, w_r, upper)

    cnt_flat = cnt[:, :N_GROUPS, 0].reshape(-1)
    out = pl.pallas_call(
        _moe_kernel,
        grid_spec=pltpu.PrefetchScalarGridSpec(
            num_scalar_prefetch=1,
            grid=(T // TM_MOE,),
            in_specs=[
                pl.BlockSpec((TM_MOE, D), lambda i, c: (i, 0)),
                pl.BlockSpec((TM_MOE, LANES), lambda i, c: (i, 0)),
                pl.BlockSpec((TM_MOE // ROUTE_ROWS, SUBLANES, ROUTE_ROWS), lambda i, c: (i, 0, 0)),
                pl.BlockSpec((TM_MOE, D), lambda i, c: (i, 0)),
                _resident_spec((N_EXPERTS, D, D_EXPERT)),
                _resident_spec((N_EXPERTS, D, D_EXPERT)),
                _resident_spec((N_EXPERTS * D_EXPERT, D)),
            ],
            out_specs=pl.BlockSpec((TM_MOE, D), lambda i, c: (i, 0)),
            scratch_shapes=[pltpu.VMEM((YS_ROWS, D), BF16),
                            pltpu.VMEM((SUB, D_GROUP), BF16)],
        ),
        out_shape=jax.ShapeDtypeStruct((T, D), F32),
        compiler_params=pltpu.CompilerParams(
            dimension_semantics=("arbitrary",), vmem_limit_bytes=VMEM_LIMIT),
        name="moe",
    )(cnt_flat, m.reshape(T, D), table.reshape(T, LANES), rows, h2.reshape(T, D), wg, wu, wd)
    return out.reshape(B, S, D)
```

```python
import functools

import numpy as np
import jax
import jax.numpy as jnp
from jax import lax
from jax.experimental import pallas as pl
from jax.experimental.pallas import tpu as pltpu

D_MODEL = 1024
N_META = 16
POOL_WIDTH = 512
POOL_WINDOWS = (2, 4, 8, 16)
POOL_GROUP = 128
HEAD_DIM = 64
N_HEADS = 8
N_KV_HEADS = 2
ATTN_WIDTH = N_HEADS * HEAD_DIM
KV_WIDTH = N_KV_HEADS * HEAD_DIM
WINDOW = 128
ROT_DIM = HEAD_DIM // 4
ROPE_THETA = 500000.0
IN_WIDTH = POOL_WIDTH + ATTN_WIDTH + 2 * KV_WIDTH
N_GROUPS = 4
EXPERTS_PER_GROUP = 4
N_EXPERTS = 16
D_EXPERT = 256
EPS = 1e-6
NEG_INF = -1e30
LOG2_E = 1.4426950408889634

LANES = 128
SUBLANES = 8
KV_EXP = 4 * LANES
TQ_IN = 2048
RB_IN = 256
TQ_ATT = 512
KEY_ROWS = 2 * (2 * WINDOW + N_META)
KEY_PAD = -(-KEY_ROWS // LANES) * LANES
OUT_BLOCKS = 2
ROUTE_ROWS = OUT_BLOCKS * WINDOW
N_ITEMS = (TQ_ATT // WINDOW) * N_KV_HEADS
TM_MOE = 512
SUB = 144
PASS_SIZES = (128, 144, 160)
BF16_ROWS = 16
MXU_DEPTH = 256
GID_LANE = N_EXPERTS
RANK_LANE = 24
LO_SHIFT = 32
D_GROUP = EXPERTS_PER_GROUP * D_EXPERT
VMEM_LIMIT = 56 * 1024 * 1024

BF16 = jnp.bfloat16
F32 = jnp.float32


def _rope_tables(n_pos):
    half = ROT_DIM // 2
    inv_freq = 1.0 / (ROPE_THETA ** (np.arange(half, dtype=np.float64) / half))
    ang = np.arange(n_pos, dtype=np.float64)[:, None] * inv_freq[None, :]
    cos, sin = np.cos(ang), np.sin(ang)
    c = np.ones((n_pos, HEAD_DIM)); c[:, :half] = cos; c[:, half:ROT_DIM] = cos
    sa = np.zeros((n_pos, HEAD_DIM)); sa[:, :half] = -sin
    sb = np.zeros((n_pos, HEAD_DIM)); sb[:, half:ROT_DIM] = sin
    tab = np.concatenate([np.tile(c, (1, 2)), np.tile(sa, (1, 2)), np.tile(sb, (1, 2))], axis=1)
    return jnp.asarray(tab, dtype=F32)


def _rms_unit(x):
    ms = jnp.mean(x * x, axis=-1, keepdims=True)
    return x * lax.rsqrt(ms + EPS)


def _head_norm_rope_many(xs, gains, rope):
    lo = lax.broadcasted_iota(jnp.int32, xs[0].shape, 1) < HEAD_DIM
    sums = []
    for x in xs:
        sq = x * x
        sums.append((jnp.sum(jnp.where(lo, sq, 0.0), axis=-1, keepdims=True),
                     jnp.sum(jnp.where(lo, 0.0, sq), axis=-1, keepdims=True)))
    ys = [x * lax.rsqrt(jnp.where(lo, s_lo, s_hi) + HEAD_DIM * EPS) * g
          for x, g, (s_lo, s_hi) in zip(xs, gains, sums)]
    half = ROT_DIM // 2
    rolled = [(pltpu.roll(y, LANES - half, 1), pltpu.roll(y, half, 1)) for y in ys]
    c, sa, sb = rope[:, 0:LANES], rope[:, LANES:2 * LANES], rope[:, 2 * LANES:3 * LANES]
    return [y * c + ra * sa + rb * sb for y, (ra, rb) in zip(ys, rolled)]


def _head_norm_rope(xc, gain2, rope):
    return _head_norm_rope_many([xc], [gain2], rope)[0]


def _expand_kv(t):
    lo = lax.broadcasted_iota(jnp.int32, t.shape, 1) < HEAD_DIM
    sw = pltpu.roll(t, HEAD_DIM, 1)
    z = jnp.zeros_like(t)
    return jnp.concatenate([jnp.where(lo, t, z), jnp.where(lo, z, sw),
                            jnp.where(lo, sw, z), jnp.where(lo, z, t)], axis=1)


def _project(x, w_in):
    return jnp.dot(_rms_unit(x).astype(BF16), w_in, preferred_element_type=F32)


def _meta_kernel(meta_ref, win_ref, kgain_ref, rope_ref, u_ref, kd_ref, v_ref):
    proj = _project(meta_ref[...], win_ref[...])
    u_ref[...] = proj[:, :POOL_WIDTH]
    k = proj[:, POOL_WIDTH + ATTN_WIDTH:POOL_WIDTH + ATTN_WIDTH + KV_WIDTH]
    v = proj[:, POOL_WIDTH + ATTN_WIDTH + KV_WIDTH:]
    k = _head_norm_rope(k, kgain_ref[...], rope_ref[...])
    kd_ref[...] = _expand_kv(k).astype(BF16)
    v_ref[...] = v.astype(BF16)


def _in_kernel(x_ref, umeta_ref, win_ref, wpool_ref, pscale_ref, qgain_ref, kgain_ref,
               rope_ref, yp_ref, q_ref, kd_ref, vt_ref, carry_ref, proj_ref):
    j = pl.program_id(1)

    @pl.when(j == 0)
    def _():
        carry_ref[...] = umeta_ref[...]

    sizes = [RB_IN] * (TQ_IN // RB_IN - 1) + [RB_IN // 2] * 2
    blocks = [slice(sum(sizes[:i]), sum(sizes[:i + 1])) for i in range(len(sizes))]
    n_sub = len(blocks)

    def normalise(i):
        return _rms_unit(x_ref[0, blocks[i], :]).astype(BF16)

    def project(i, a):
        proj_ref[blocks[i], :] = jnp.dot(a, win_ref[...], preferred_element_type=F32)

    def finish(i):
        rows = blocks[i]
        rope = rope_ref[rows, :]
        u = proj_ref[rows, :POOL_WIDTH]
        acc = jnp.concatenate([carry_ref[...], u], axis=0)
        for gi, w in enumerate(POOL_WINDOWS):
            lo = gi * POOL_GROUP
            acc = acc[:, POOL_GROUP * (1 if gi else 0):]
            acc = acc + pltpu.roll(acc, w // 2, 0)
            mixed = acc[N_META:, :POOL_GROUP] * (1.0 / w) - u[:, lo:lo + POOL_GROUP]
            y = jnp.dot(mixed.astype(BF16), wpool_ref[gi], preferred_element_type=F32)
            yp_ref[0, rows, lo:lo + POOL_GROUP] = (y * pscale_ref[:, lo:lo + POOL_GROUP]).astype(BF16)
        carry_ref[...] = u[sizes[i] - N_META:, :]

        n_qc = ATTN_WIDTH // LANES
        xs = [proj_ref[rows, POOL_WIDTH + c * LANES:POOL_WIDTH + (c + 1) * LANES] for c in range(n_qc + 1)]
        outs = _head_norm_rope_many(xs, [qgain_ref[...]] * n_qc + [kgain_ref[...]], rope)
        for c in range(n_qc):
            q_ref[0, rows, c * LANES:(c + 1) * LANES] = outs[c].astype(BF16)
        kd_ref[0, rows, :] = _expand_kv(outs[n_qc]).astype(BF16)
        v = proj_ref[rows, POOL_WIDTH + ATTN_WIDTH + KV_WIDTH:]
        vt_ref[0, :, rows] = jnp.transpose(v).astype(BF16)

    a_next = normalise(0)
    for i in range(n_sub):
        project(i, a_next)
        if i + 1 < n_sub:
            a_next = normalise(i + 1)
        if i > 0:
            finish(i - 1)
    finish(n_sub - 1)


def _route(logits, upper):
    n = logits.shape[0]
    lt = jnp.transpose(logits)
    sub = SUBLANES
    row8 = lax.broadcasted_iota(jnp.int32, (sub, n), 0).astype(F32)
    row16 = lax.broadcasted_iota(jnp.int32, (N_EXPERTS, n), 0).astype(F32)
    g_ok = row8 < N_GROUPS
    gl = jnp.where(g_ok, lt[N_EXPERTS:N_EXPERTS + sub, :], NEG_INF)
    gmax = jnp.max(gl, axis=0, keepdims=True)
    gsum = jnp.sum(jnp.where(g_ok, jnp.exp(gl - gmax), 0.0), axis=0, keepdims=True)
    g_prob = 1.0 / gsum
    g_idx = jnp.min(jnp.where(gl == gmax, row8, float(sub)), axis=0, keepdims=True)
    e_lo = g_idx * EXPERTS_PER_GROUP
    emask = (row16 >= e_lo) & (row16 < e_lo + EXPERTS_PER_GROUP)
    el = jnp.where(emask, lt[:N_EXPERTS, :], NEG_INF)
    big = float(N_EXPERTS)
    e1 = jnp.max(el, axis=0, keepdims=True)
    i1 = jnp.min(jnp.where(el == e1, row16, big), axis=0, keepdims=True)
    el2 = jnp.where(row16 == i1, NEG_INF, el)
    e2 = jnp.max(el2, axis=0, keepdims=True)
    i2 = jnp.min(jnp.where(el2 == e2, row16, big), axis=0, keepdims=True)
    t = jnp.exp(e2 - e1)
    w1 = 1.0 / (1.0 + t)
    w2 = t * w1
    gates_t = jnp.where(row16 == i1, w1 * g_prob, 0.0) + jnp.where(row16 == i2, w2 * g_prob, 0.0)
    onehot = jnp.where(row8 == g_idx, 1.0, 0.0)
    earlier = jnp.dot(onehot.astype(BF16), upper, preferred_element_type=F32)
    rank = jnp.sum(onehot * earlier, axis=0, keepdims=True)
    hi = gates_t.astype(BF16).astype(F32)
    lo = (gates_t - hi).astype(BF16).astype(F32)
    gid8 = jnp.broadcast_to(g_idx, (sub, n))
    rank8 = jnp.broadcast_to(rank, (sub, n))
    assert GID_LANE == N_EXPERTS and RANK_LANE == GID_LANE + sub and LO_SHIFT == RANK_LANE + sub
    full_t = jnp.concatenate(
        [hi, gid8, rank8, lo, jnp.zeros((LANES - LO_SHIFT - N_EXPERTS, n), F32)], axis=0)
    rows = jnp.concatenate([g_idx, rank, jnp.zeros((sub - 2, n), F32)], axis=0)
    counts = jnp.sum(onehot, axis=1, keepdims=True)
    return (jnp.transpose(full_t).astype(BF16), rows,
            jnp.broadcast_to(counts, (sub, LANES)).astype(jnp.int32))


def _fold_rows(x, pair, final):
    rows = x.shape[0]
    while rows > SUBLANES:
        rows //= 2
        x = pair(x[:rows], x[rows:])
    return final(x, axis=0, keepdims=True)


def _attn_kernel(n_tiles, tiles_per_row, sink_ref, x_ref, yp_ref, q_ref, kd_ref, vt_ref, kmeta_ref, vtmeta_ref, wout_ref,
                 wr_ref, upper_ref, h2_ref, m_ref, table_ref, rows_ref, cnt_ref,
                 yattn_ref, kcat_ref, vbd_ref, s_ref, p_ref, inv_ref, logit_ref):
    step = pl.program_id(0)
    tile = jnp.minimum(step, n_tiles - 1)
    j = lax.rem(tile, tiles_per_row)
    slot = lax.rem(step, 2)
    nt = (((1,), (1,)), ((), ()))
    n_r = TQ_ATT // WINDOW
    band = 2 * WINDOW
    m0 = 2 * band

    @pl.when(step == 0)
    def _():
        vbd_ref[...] = jnp.zeros(vbd_ref.shape, BF16)
        p_ref[...] = jnp.zeros(p_ref.shape, BF16)
        yattn_ref[...] = jnp.zeros(yattn_ref.shape, BF16)

    def window_start(r):
        return pl.multiple_of(jnp.maximum(j * TQ_ATT + (r - 1) * WINDOW, 0), WINDOW)

    def stage_scores(r):
        start = window_start(r)
        for g in range(N_KV_HEADS):
            vb = vt_ref[0, g * HEAD_DIM:(g + 1) * HEAD_DIM, pl.ds(start, band)]
            vm = vtmeta_ref[g * HEAD_DIM:(g + 1) * HEAD_DIM, :]
            for hh in range(2):
                col = (2 * g + hh) * LANES
                kcat_ref[r, g, hh * band:(hh + 1) * band, :] = kd_ref[0, pl.ds(start, band), col:col + LANES]
                kcat_ref[r, g, m0 + hh * N_META:m0 + (hh + 1) * N_META, :] = kmeta_ref[:, col:col + LANES]
                rows = slice(hh * HEAD_DIM, (hh + 1) * HEAD_DIM)
                vbd_ref[r, g, rows, hh * band:(hh + 1) * band] = vb
                vbd_ref[r, g, rows, m0 + hh * N_META:m0 + (hh + 1) * N_META] = vm
            rq = slice(r * WINDOW, (r + 1) * WINDOW)
            qq = jnp.concatenate([q_ref[0, rq, (2 * g + ch) * LANES:(2 * g + ch + 1) * LANES]
                                  for ch in range(2)], axis=0)
            s_ref[r * N_KV_HEADS + g] = lax.dot_general(kcat_ref[r, g], qq, nt, preferred_element_type=F32)

    def stage_softmax(r):
        kpos = window_start(r) + lax.broadcasted_iota(jnp.int32, (band, WINDOW), 0)
        qpos = j * TQ_ATT + r * WINDOW + lax.broadcasted_iota(jnp.int32, (band, WINDOW), 1)
        d = qpos - kpos
        bias = jnp.where((d >= 0) & (d < WINDOW), 0.0, NEG_INF)
        bias = jnp.concatenate([bias, bias], axis=1)
        left = lax.broadcasted_iota(jnp.int32, (1, 2 * WINDOW), 1) < WINDOW
        for g in range(N_KV_HEADS):
            it = r * N_KV_HEADS + g
            for hh in range(2):
                sink = jnp.where(left, sink_ref[4 * g + hh], sink_ref[4 * g + 2 + hh])
                s = s_ref[it, hh * band:(hh + 1) * band, :] + bias
                sm = s_ref[it, m0 + hh * N_META:m0 + (hh + 1) * N_META, :]
                mx = jnp.maximum(_fold_rows(s, jnp.maximum, jnp.max), jnp.max(sm, axis=0, keepdims=True))
                mx = jnp.maximum(mx, sink)
                p = jnp.exp2(s - mx)
                pm = jnp.exp2(sm - mx)
                den = (_fold_rows(p, jnp.add, jnp.sum) + jnp.sum(pm, axis=0, keepdims=True)
                       + jnp.exp2(sink - mx))
                p_ref[it, hh * band:(hh + 1) * band, :] = p.astype(BF16)
                p_ref[it, m0 + hh * N_META:m0 + (hh + 1) * N_META, :] = pm.astype(BF16)
                inv_ref[it, hh * HEAD_DIM:(hh + 1) * HEAD_DIM, :] = jnp.broadcast_to(
                    1.0 / den, (HEAD_DIM, 2 * WINDOW))

    def stage_values(r):
        for g in range(N_KV_HEADS):
            it = r * N_KV_HEADS + g
            o_t = jnp.dot(vbd_ref[r, g], p_ref[it], preferred_element_type=F32) * inv_ref[it]
            for ch in range(2):
                c = 2 * g + ch
                yattn_ref[slot, r * WINDOW:(r + 1) * WINDOW, c * LANES:(c + 1) * LANES] = (
                    jnp.transpose(o_t[:, ch * WINDOW:(ch + 1) * WINDOW]).astype(BF16))

    def stage_out():
        mix = (jnp.dot(yp_ref[0], wout_ref[:POOL_WIDTH, :], preferred_element_type=F32)
               + jnp.dot(yattn_ref[1 - slot], wout_ref[POOL_WIDTH:, :], preferred_element_type=F32))
        h2 = x_ref[0] + mix
        h2_ref[0] = h2
        m = _rms_unit(h2).astype(BF16)
        m_ref[0] = m
        logits = jnp.dot(m, wr_ref[...], preferred_element_type=F32)
        for p in range(TQ_ATT // ROUTE_ROWS):
            logit_ref[p] = logits[p * ROUTE_ROWS:(p + 1) * ROUTE_ROWS, :]

    def stage_route(p):
        table, rows, counts = _route(logit_ref[p], upper_ref[...])
        table_ref[0, p * ROUTE_ROWS:(p + 1) * ROUTE_ROWS, :] = table
        rows_ref[p] = rows
        cnt_ref[p] = counts

    n_p = TQ_ATT // ROUTE_ROWS
    for t in range(max(n_r + 2, n_p + 1)):
        for k, stage in enumerate((stage_scores, stage_softmax, stage_values)):
            if 0 <= t - k < n_r:
                stage(t - k)
        if t == 0:
            stage_out()
        if 1 <= t <= n_p:
            stage_route(t - 1)


def _padded_rows(n):
    if isinstance(n, int):
        if n == 0:
            return 0
        return next((s for s in PASS_SIZES if n <= s), -(-n // SUB) * SUB)
    rows = lax.div(n + (SUB - 1), SUB) * SUB
    for size in reversed(PASS_SIZES):
        rows = jnp.where(n <= size, size, rows)
    return jnp.where(n == 0, 0, rows)


def _max_sorted_rows(n_tokens, n_groups):
    best = [0] + [-1] * n_tokens
    for _ in range(n_groups):
        nxt = [-1] * (n_tokens + 1)
        for used, rows in enumerate(best):
            if rows >= 0:
                for n in range(n_tokens - used + 1):
                    nxt[used + n] = max(nxt[used + n], rows + _padded_rows(n))
        best = nxt
    return best[n_tokens]


def _moe_kernel(cnt_ref, m_ref, table_ref, rows_ref, h2_ref, wg_ref, wu_ref, wd_ref, out_ref,
                ys_ref, hid_ref, plan_ref):
    i = pl.program_id(0)
    per = TM_MOE // ROUTE_ROWS
    counts = [[cnt_ref[(i * per + h) * N_GROUPS + g] for g in range(N_GROUPS)] for h in range(per)]
    base = []
    total_rows = 0
    for g in range(N_GROUPS):
        n_g = sum(counts[h][g] for h in range(per))
        base.append(total_rows)
        plan_ref[g] = n_g
        plan_ref[N_GROUPS + g] = total_rows + jnp.int32(0)
        total_rows = total_rows + _padded_rows(n_g)
    first = [[base[g] + sum(counts[hh][g] for hh in range(h)) for g in range(N_GROUPS)] for h in range(per)]

    @pl.when(i == 0)
    def _():
        ys_ref[...] = jnp.zeros(ys_ref.shape, BF16)

    def pick(gid, offsets):
        out = _as_f32(offsets[N_GROUPS - 1])
        for g in range(N_GROUPS - 2, -1, -1):
            out = jnp.where(gid == g, _as_f32(offsets[g]), out)
        return out

    table = table_ref[...]
    table_f = table.astype(F32)
    gid_col = table_f[:, GID_LANE:GID_LANE + 1]
    row_id = lax.broadcasted_iota(jnp.int32, (TM_MOE, 1), 0)
    off_col = pick(gid_col, first[per - 1])
    for h in range(per - 2, -1, -1):
        off_col = jnp.where(row_id < (h + 1) * ROUTE_ROWS, pick(gid_col, first[h]), off_col)
    pos_col = table_f[:, RANK_LANE:RANK_LANE + 1] + off_col
    pos_row = jnp.concatenate(
        [rows_ref[h, 1:2, :] + pick(rows_ref[h, 0:1, :], first[h]) for h in range(per)], axis=1)

    def run_pass(g, r0, size):
        rows = (r0 + lax.broadcasted_iota(jnp.int32, (size, 1), 0)).astype(F32)
        perm = jnp.where(pos_row == rows, 1.0, 0.0).astype(BF16)
        xs = jnp.dot(perm, m_ref[...], preferred_element_type=F32).astype(BF16)
        gs = jnp.dot(perm, table, preferred_element_type=F32)
        lane = lax.broadcasted_iota(jnp.int32, gs.shape, 1)
        for jj in range(EXPERTS_PER_GROUP):
            e = g * EXPERTS_PER_GROUP + jj
            gt = jnp.dot(xs, wg_ref[e], preferred_element_type=F32)
            up = jnp.dot(xs, wu_ref[e], preferred_element_type=F32)
            gate = jnp.sum(jnp.where((lane == e) | (lane == e + LO_SHIFT), gs, 0.0),
                           axis=-1, keepdims=True)
            hid = gt * (1.0 / (1.0 + jnp.exp(-gt))) * up * gate
            hid_ref[:size, jj * D_EXPERT:(jj + 1) * D_EXPERT] = hid.astype(BF16)
        w_down = wd_ref[pl.ds(pl.multiple_of(g * D_GROUP, D_GROUP), D_GROUP), :]
        y = jnp.dot(hid_ref[:size, :], w_down, preferred_element_type=F32)
        ys_ref[pl.ds(r0, size), :] = y.astype(BF16)

    def run_group(g, carry):
        n = plan_ref[g]
        r0 = pl.multiple_of(plan_ref[N_GROUPS + g], BF16_ROWS)
        below = 0
        for size in PASS_SIZES:
            @pl.when((n > below) & (n <= size))
            def _(size=size):
                run_pass(g, r0, size)
            below = size

        @pl.when(n > PASS_SIZES[-1])
        def _():
            def body(c, inner):
                run_pass(g, pl.multiple_of(r0 + c * SUB, BF16_ROWS), SUB)
                return inner
            lax.fori_loop(0, lax.div(n + (SUB - 1), SUB), body, 0)
        return carry

    lax.fori_loop(0, N_GROUPS, run_group, 0)

    def scatter(n_rows):
        cols = lax.broadcasted_iota(jnp.int32, (1, n_rows), 1).astype(F32)
        perm_t = jnp.where(pos_col == cols, 1.0, 0.0).astype(BF16)
        out_ref[...] = h2_ref[...] + jnp.dot(perm_t, ys_ref[:n_rows, :], preferred_element_type=F32)

    short = total_rows <= YS_SHORT

    @pl.when(short)
    def _():
        scatter(YS_SHORT)

    @pl.when(jnp.logical_not(short))
    def _():
        scatter(YS_ROWS)


YS_ROWS = -(-_max_sorted_rows(TM_MOE, N_GROUPS) // MXU_DEPTH) * MXU_DEPTH
YS_SHORT = YS_ROWS - MXU_DEPTH


def _as_f32(v):
    return float(v) if isinstance(v, int) else v.astype(F32)


def _const_spec(shape):
    n = len(shape)
    return pl.BlockSpec(shape, lambda *_: (0,) * n)


def _resident_spec(shape):
    n = len(shape)
    return pl.BlockSpec(shape, lambda *_: (0,) * n, pipeline_mode=pl.Buffered(1))


def kernel(x, meta_tokens, attn_norm_gain, w_in, w_pool, pool_scale, q_norm_gain, k_norm_gain,
           attn_sinks, w_out, ffn_norm_gain, w_group_router, w_expert_router, w_gate, w_up, w_down):
    B, S, D = x.shape
    assert D == D_MODEL and S % TQ_IN == 0 and S % TQ_ATT == 0 and (B * S) % TM_MOE == 0
    assert w_in.shape[0] == 1, "single layer"
    T = B * S

    rope = _rope_tables(N_META + S)
    fgain = ffn_norm_gain[0][:, None]
    qgain2 = jnp.tile(q_norm_gain[0] * LOG2_E, 2)[None, :]
    kgain2 = jnp.tile(k_norm_gain[0] * (HEAD_DIM ** 0.5), 2)[None, :]
    pscale = pool_scale[0][None, :]
    w_in_b = (attn_norm_gain[0][:, None] * w_in[0]).astype(BF16)
    w_pool_b = w_pool[0].astype(BF16)
    w_out_b = w_out[0].astype(BF16)
    w_r = (fgain * jnp.concatenate(
        [w_expert_router[0], w_group_router[0],
         jnp.zeros((D, LANES - N_EXPERTS - N_GROUPS), F32)], axis=1)).astype(BF16)
    wg = (fgain[None] * w_gate[0]).astype(BF16)
    wu = (fgain[None] * w_up[0]).astype(BF16)
    wd = w_down[0].astype(BF16).reshape(N_EXPERTS * D_EXPERT, D)
    params = pltpu.CompilerParams(vmem_limit_bytes=VMEM_LIMIT)

    u_meta, kd_meta, v_meta = pl.pallas_call(
        _meta_kernel,
        out_shape=(jax.ShapeDtypeStruct((N_META, POOL_WIDTH), F32),
                   jax.ShapeDtypeStruct((N_META, KV_EXP), BF16),
                   jax.ShapeDtypeStruct((N_META, KV_WIDTH), BF16)),
        compiler_params=params,
        name="meta_proj",
    )(meta_tokens, w_in_b, kgain2, rope[:N_META])
    vt_meta = v_meta.T

    yp, q, kd, vt = pl.pallas_call(
        _in_kernel,
        grid=(B, S // TQ_IN),
        in_specs=[
            pl.BlockSpec((1, TQ_IN, D), lambda b, j: (b, j, 0)),
            _const_spec((N_META, POOL_WIDTH)),
            _const_spec((D, IN_WIDTH)),
            _const_spec((len(POOL_WINDOWS), POOL_GROUP, POOL_GROUP)),
            _const_spec((1, POOL_WIDTH)),
            _const_spec((1, LANES)),
            _const_spec((1, LANES)),
            pl.BlockSpec((TQ_IN, 3 * LANES), lambda b, j: (j, 0)),
        ],
        out_specs=(
            pl.BlockSpec((1, TQ_IN, POOL_WIDTH), lambda b, j: (b, j, 0)),
            pl.BlockSpec((1, TQ_IN, ATTN_WIDTH), lambda b, j: (b, j, 0)),
            pl.BlockSpec((1, TQ_IN, KV_EXP), lambda b, j: (b, j, 0)),
            pl.BlockSpec((1, KV_WIDTH, TQ_IN), lambda b, j: (b, 0, j)),
        ),
        out_shape=(jax.ShapeDtypeStruct((B, S, POOL_WIDTH), BF16),
                   jax.ShapeDtypeStruct((B, S, ATTN_WIDTH), BF16),
                   jax.ShapeDtypeStruct((B, S, KV_EXP), BF16),
                   jax.ShapeDtypeStruct((B, KV_WIDTH, S), BF16)),
        scratch_shapes=[pltpu.VMEM((N_META, POOL_WIDTH), F32),
                        pltpu.VMEM((TQ_IN, IN_WIDTH), F32)],
        compiler_params=pltpu.CompilerParams(
            dimension_semantics=("arbitrary", "arbitrary"), vmem_limit_bytes=VMEM_LIMIT),
        name="in_proj",
    )(x, u_meta, w_in_b, w_pool_b, pscale, qgain2, kgain2, rope[N_META:])

    upper = jnp.asarray(np.triu(np.ones((ROUTE_ROWS, ROUTE_ROWS), np.float32), 1), dtype=BF16)
    per_row = S // TQ_ATT
    n_tiles = B * per_row

    def cur(s):
        t = jnp.minimum(s, n_tiles - 1)
        return t // per_row, t % per_row

    def lag(s):
        t = jnp.maximum(s - 1, 0)
        return t // per_row, t % per_row

    def lag_rows(s):
        return (*lag(s), 0)

    def lag_flat(s):
        return (jnp.maximum(s - 1, 0), 0, 0)

    h2, m, table, rows, cnt = pl.pallas_call(
        functools.partial(_attn_kernel, n_tiles, per_row),
        grid=(n_tiles + 1,),
        in_specs=[
            pl.BlockSpec(memory_space=pltpu.SMEM),
            pl.BlockSpec((1, TQ_ATT, D), lag_rows),
            pl.BlockSpec((1, TQ_ATT, POOL_WIDTH), lag_rows),
            pl.BlockSpec((1, TQ_ATT, ATTN_WIDTH), lambda s: (*cur(s), 0)),
            pl.BlockSpec((1, S, KV_EXP), lambda s: (cur(s)[0], 0, 0)),
            pl.BlockSpec((1, KV_WIDTH, S), lambda s: (cur(s)[0], 0, 0)),
            _const_spec((N_META, KV_EXP)),
            _const_spec((KV_WIDTH, N_META)),
            _const_spec((D, D)),
            _const_spec((D, LANES)),
            _const_spec((ROUTE_ROWS, ROUTE_ROWS)),
        ],
        out_specs=(
            pl.BlockSpec((1, TQ_ATT, D), lag_rows),
            pl.BlockSpec((1, TQ_ATT, D), lag_rows),
            pl.BlockSpec((1, TQ_ATT, LANES), lag_rows),
            pl.BlockSpec((TQ_ATT // ROUTE_ROWS, SUBLANES, ROUTE_ROWS), lag_flat),
            pl.BlockSpec((TQ_ATT // ROUTE_ROWS, SUBLANES, LANES), lag_flat),
        ),
        out_shape=(jax.ShapeDtypeStruct((B, S, D), F32),
                   jax.ShapeDtypeStruct((B, S, D), BF16),
                   jax.ShapeDtypeStruct((B, S, LANES), BF16),
                   jax.ShapeDtypeStruct((T // ROUTE_ROWS, SUBLANES, ROUTE_ROWS), F32),
                   jax.ShapeDtypeStruct((T // ROUTE_ROWS, SUBLANES, LANES), jnp.int32)),
        scratch_shapes=[
            pltpu.VMEM((2, TQ_ATT, ATTN_WIDTH), BF16),
            pltpu.VMEM((TQ_ATT // WINDOW, N_KV_HEADS, KEY_ROWS, LANES), BF16),
            pltpu.VMEM((TQ_ATT // WINDOW, N_KV_HEADS, LANES, KEY_PAD), BF16),
            pltpu.VMEM((N_ITEMS, KEY_ROWS, 2 * LANES), F32),
            pltpu.VMEM((N_ITEMS, KEY_PAD, 2 * LANES), BF16),
            pltpu.VMEM((N_ITEMS, LANES, 2 * LANES), F32),
            pltpu.VMEM((TQ_ATT // ROUTE_ROWS, ROUTE_ROWS, LANES), F32),
        ],
        compiler_params=pltpu.CompilerParams(
            dimension_semantics=("arbitrary",), vmem_limit_bytes=VMEM_LIMIT),
        name="attn_out",
    )(attn_sinks[0] * LOG2_E, x, yp, q, kd, vt, kd_meta, vt_meta, w_out_b, w_r, upper)

    cnt_flat = cnt[:, :N_GROUPS, 0].reshape(-1)
    out = pl.pallas_call(
        _moe_kernel,
        grid_spec=pltpu.PrefetchScalarGridSpec(
            num_scalar_prefetch=1,
            grid=(T // TM_MOE,),
            in_specs=[
                pl.BlockSpec((TM_MOE, D), lambda i, c: (i, 0)),
                pl.BlockSpec((TM_MOE, LANES), lambda i, c: (i, 0)),
                pl.BlockSpec((TM_MOE // ROUTE_ROWS, SUBLANES, ROUTE_ROWS), lambda i, c: (i, 0, 0)),
                pl.BlockSpec((TM_MOE, D), lambda i, c: (i, 0)),
                _resident_spec((N_EXPERTS, D, D_EXPERT)),
                _resident_spec((N_EXPERTS, D, D_EXPERT)),
                _resident_spec((N_EXPERTS * D_EXPERT, D)),
            ],
            out_specs=pl.BlockSpec((TM_MOE, D), lambda i, c: (i, 0)),
            scratch_shapes=[pltpu.VMEM((YS_ROWS, D), BF16),
                            pltpu.VMEM((max(SUB, *PASS_SIZES), D_GROUP), BF16),
                            pltpu.SMEM((2 * N_GROUPS,), jnp.int32)],
        ),
        out_shape=jax.ShapeDtypeStruct((T, D), F32),
        compiler_params=pltpu.CompilerParams(
            dimension_semantics=("arbitrary",), vmem_limit_bytes=VMEM_LIMIT),
        name="moe",
    )(cnt_flat, m.reshape(T, D), table.reshape(T, LANES), rows, h2.reshape(T, D), wg, wu, wd)
    return out.reshape(B, S, D)
```

```python
import functools

import numpy as np
import jax
import jax.numpy as jnp
from jax import lax
from jax.experimental import pallas as pl
from jax.experimental.pallas import tpu as pltpu

D_MODEL = 1024
N_META = 16
POOL_WIDTH = 512
POOL_WINDOWS = (2, 4, 8, 16)
POOL_GROUP = 128
HEAD_DIM = 64
N_HEADS = 8
N_KV_HEADS = 2
ATTN_WIDTH = N_HEADS * HEAD_DIM
KV_WIDTH = N_KV_HEADS * HEAD_DIM
WINDOW = 128
ROT_DIM = HEAD_DIM // 4
ROPE_THETA = 500000.0
IN_WIDTH = POOL_WIDTH + ATTN_WIDTH + 2 * KV_WIDTH
N_GROUPS = 4
EXPERTS_PER_GROUP = 4
N_EXPERTS = 16
D_EXPERT = 256
EPS = 1e-6
NEG_INF = -1e30
LOG2_E = 1.4426950408889634

LANES = 128
SUBLANES = 8
KV_EXP = 4 * LANES
TQ_IN = 2048
RB_IN = 256
TQ_ATT = 512
KEY_ROWS = 2 * (2 * WINDOW + N_META)
KEY_PAD = -(-KEY_ROWS // LANES) * LANES
OUT_BLOCKS = 2
ROUTE_ROWS = OUT_BLOCKS * WINDOW
W_SLICES = 4
N_ITEMS = (TQ_ATT // WINDOW) * N_KV_HEADS
TM_MOE = 512
SUB = 144
PASS_SIZES = (128, 144, 160)
BF16_ROWS = 16
MXU_DEPTH = 256
GID_LANE = N_EXPERTS
RANK_LANE = 24
LO_SHIFT = 32
D_GROUP = EXPERTS_PER_GROUP * D_EXPERT
VMEM_LIMIT = 56 * 1024 * 1024

BF16 = jnp.bfloat16
F32 = jnp.float32


def _rope_tables(n_pos):
    half = ROT_DIM // 2
    inv_freq = 1.0 / (ROPE_THETA ** (np.arange(half, dtype=np.float64) / half))
    ang = np.arange(n_pos, dtype=np.float64)[:, None] * inv_freq[None, :]
    cos, sin = np.cos(ang), np.sin(ang)
    c = np.ones((n_pos, HEAD_DIM)); c[:, :half] = cos; c[:, half:ROT_DIM] = cos
    sa = np.zeros((n_pos, HEAD_DIM)); sa[:, :half] = -sin
    sb = np.zeros((n_pos, HEAD_DIM)); sb[:, half:ROT_DIM] = sin
    tab = np.concatenate([np.tile(c, (1, 2)), np.tile(sa, (1, 2)), np.tile(sb, (1, 2))], axis=1)
    return jnp.asarray(tab, dtype=F32)


def _rms_unit(x):
    ms = jnp.mean(x * x, axis=-1, keepdims=True)
    return x * lax.rsqrt(ms + EPS)


def _head_norm_rope_many(xs, gains, rope):
    lo = lax.broadcasted_iota(jnp.int32, xs[0].shape, 1) < HEAD_DIM
    sums = []
    for x in xs:
        sq = x * x
        sums.append((jnp.sum(jnp.where(lo, sq, 0.0), axis=-1, keepdims=True),
                     jnp.sum(jnp.where(lo, 0.0, sq), axis=-1, keepdims=True)))
    ys = [x * lax.rsqrt(jnp.where(lo, s_lo, s_hi) + HEAD_DIM * EPS) * g
          for x, g, (s_lo, s_hi) in zip(xs, gains, sums)]
    half = ROT_DIM // 2
    rolled = [(pltpu.roll(y, LANES - half, 1), pltpu.roll(y, half, 1)) for y in ys]
    c, sa, sb = rope[:, 0:LANES], rope[:, LANES:2 * LANES], rope[:, 2 * LANES:3 * LANES]
    return [y * c + ra * sa + rb * sb for y, (ra, rb) in zip(ys, rolled)]


def _head_norm_rope(xc, gain2, rope):
    return _head_norm_rope_many([xc], [gain2], rope)[0]


def _expand_kv(t):
    lo = lax.broadcasted_iota(jnp.int32, t.shape, 1) < HEAD_DIM
    sw = pltpu.roll(t, HEAD_DIM, 1)
    z = jnp.zeros_like(t)
    return jnp.concatenate([jnp.where(lo, t, z), jnp.where(lo, z, sw),
                            jnp.where(lo, sw, z), jnp.where(lo, z, t)], axis=1)


def _project(x, w_in):
    return jnp.dot(_rms_unit(x).astype(BF16), w_in, preferred_element_type=F32)


def _meta_kernel(meta_ref, win_ref, kgain_ref, rope_ref, u_ref, kd_ref, v_ref):
    proj = _project(meta_ref[...], win_ref[...])
    u_ref[...] = proj[:, :POOL_WIDTH]
    k = proj[:, POOL_WIDTH + ATTN_WIDTH:POOL_WIDTH + ATTN_WIDTH + KV_WIDTH]
    v = proj[:, POOL_WIDTH + ATTN_WIDTH + KV_WIDTH:]
    k = _head_norm_rope(k, kgain_ref[...], rope_ref[...])
    kd_ref[...] = _expand_kv(k).astype(BF16)
    v_ref[...] = v.astype(BF16)


def _in_kernel(x_ref, umeta_ref, win_ref, wpool_ref, pscale_ref, qgain_ref, kgain_ref,
               rope_ref, yp_ref, q_ref, kd_ref, vt_ref, carry_ref, proj_ref):
    j = pl.program_id(1)

    @pl.when(j == 0)
    def _():
        carry_ref[...] = umeta_ref[...]

    sizes = [RB_IN] * (TQ_IN // RB_IN - 1) + [RB_IN // 2] * 2
    blocks = [slice(sum(sizes[:i]), sum(sizes[:i + 1])) for i in range(len(sizes))]
    n_sub = len(blocks)

    def normalise(i):
        return _rms_unit(x_ref[0, blocks[i], :]).astype(BF16)

    def project(i, a):
        proj_ref[blocks[i], :] = jnp.dot(a, win_ref[...], preferred_element_type=F32)

    def finish(i):
        rows = blocks[i]
        rope = rope_ref[rows, :]
        u = proj_ref[rows, :POOL_WIDTH]
        acc = jnp.concatenate([carry_ref[...], u], axis=0)
        for gi, w in enumerate(POOL_WINDOWS):
            lo = gi * POOL_GROUP
            acc = acc[:, POOL_GROUP * (1 if gi else 0):]
            acc = acc + pltpu.roll(acc, w // 2, 0)
            mixed = acc[N_META:, :POOL_GROUP] * (1.0 / w) - u[:, lo:lo + POOL_GROUP]
            y = jnp.dot(mixed.astype(BF16), wpool_ref[gi], preferred_element_type=F32)
            yp_ref[0, rows, lo:lo + POOL_GROUP] = (y * pscale_ref[:, lo:lo + POOL_GROUP]).astype(BF16)
        carry_ref[...] = u[sizes[i] - N_META:, :]

        n_qc = ATTN_WIDTH // LANES
        xs = [proj_ref[rows, POOL_WIDTH + c * LANES:POOL_WIDTH + (c + 1) * LANES] for c in range(n_qc + 1)]
        outs = _head_norm_rope_many(xs, [qgain_ref[...]] * n_qc + [kgain_ref[...]], rope)
        for c in range(n_qc):
            q_ref[0, rows, c * LANES:(c + 1) * LANES] = outs[c].astype(BF16)
        kd_ref[0, rows, :] = _expand_kv(outs[n_qc]).astype(BF16)
        v = proj_ref[rows, POOL_WIDTH + ATTN_WIDTH + KV_WIDTH:]
        vt_ref[0, :, rows] = jnp.transpose(v).astype(BF16)

    a_next = normalise(0)
    for i in range(n_sub):
        project(i, a_next)
        if i + 1 < n_sub:
            a_next = normalise(i + 1)
        if i > 0:
            finish(i - 1)
    finish(n_sub - 1)


def _route(logits, upper):
    n = logits.shape[0]
    lt = jnp.transpose(logits)
    sub = SUBLANES
    row8 = lax.broadcasted_iota(jnp.int32, (sub, n), 0).astype(F32)
    row16 = lax.broadcasted_iota(jnp.int32, (N_EXPERTS, n), 0).astype(F32)
    g_ok = row8 < N_GROUPS
    gl = jnp.where(g_ok, lt[N_EXPERTS:N_EXPERTS + sub, :], NEG_INF)
    gmax = jnp.max(gl, axis=0, keepdims=True)
    gsum = jnp.sum(jnp.where(g_ok, jnp.exp(gl - gmax), 0.0), axis=0, keepdims=True)
    g_prob = 1.0 / gsum
    g_idx = jnp.min(jnp.where(gl == gmax, row8, float(sub)), axis=0, keepdims=True)
    e_lo = g_idx * EXPERTS_PER_GROUP
    emask = (row16 >= e_lo) & (row16 < e_lo + EXPERTS_PER_GROUP)
    el = jnp.where(emask, lt[:N_EXPERTS, :], NEG_INF)
    big = float(N_EXPERTS)
    e1 = jnp.max(el, axis=0, keepdims=True)
    i1 = jnp.min(jnp.where(el == e1, row16, big), axis=0, keepdims=True)
    el2 = jnp.where(row16 == i1, NEG_INF, el)
    e2 = jnp.max(el2, axis=0, keepdims=True)
    i2 = jnp.min(jnp.where(el2 == e2, row16, big), axis=0, keepdims=True)
    t = jnp.exp(e2 - e1)
    w1 = 1.0 / (1.0 + t)
    w2 = t * w1
    gates_t = jnp.where(row16 == i1, w1 * g_prob, 0.0) + jnp.where(row16 == i2, w2 * g_prob, 0.0)
    onehot = jnp.where(row8 == g_idx, 1.0, 0.0)
    earlier = jnp.dot(onehot.astype(BF16), upper, preferred_element_type=F32)
    rank = jnp.sum(onehot * earlier, axis=0, keepdims=True)
    hi = gates_t.astype(BF16).astype(F32)
    lo = (gates_t - hi).astype(BF16).astype(F32)
    gid8 = jnp.broadcast_to(g_idx, (sub, n))
    rank8 = jnp.broadcast_to(rank, (sub, n))
    assert GID_LANE == N_EXPERTS and RANK_LANE == GID_LANE + sub and LO_SHIFT == RANK_LANE + sub
    full_t = jnp.concatenate(
        [hi, gid8, rank8, lo, jnp.zeros((LANES - LO_SHIFT - N_EXPERTS, n), F32)], axis=0)
    rows = jnp.concatenate([g_idx, rank, jnp.zeros((sub - 2, n), F32)], axis=0)
    counts = jnp.sum(onehot, axis=1, keepdims=True)
    return (jnp.transpose(full_t).astype(BF16), rows,
            jnp.broadcast_to(counts, (sub, LANES)).astype(jnp.int32))


def _fold_rows(x, pair, final):
    rows = x.shape[0]
    while rows > SUBLANES:
        rows //= 2
        x = pair(x[:rows], x[rows:])
    return final(x, axis=0, keepdims=True)


def _attn_kernel(n_tiles, tiles_per_row, sink_ref, x_ref, yp_ref, q_ref, kd_ref, vt_ref, kmeta_ref, vtmeta_ref, wout_ref,
                 wr_ref, upper_ref, fg_ref, wg32_ref, wu32_ref, wd32_ref,
                 h2_ref, m_ref, table_ref, rows_ref, cnt_ref, wg_ref, wu_ref, wd_ref,
                 yattn_ref, kcat_ref, vbd_ref, s_ref, p_ref, inv_ref, logit_ref):
    step = pl.program_id(0)
    tile = jnp.minimum(step, n_tiles - 1)
    j = lax.rem(tile, tiles_per_row)
    slot = lax.rem(step, 2)
    nt = (((1,), (1,)), ((), ()))
    n_r = TQ_ATT // WINDOW
    band = 2 * WINDOW
    m0 = 2 * band

    @pl.when(step == 0)
    def _():
        vbd_ref[...] = jnp.zeros(vbd_ref.shape, BF16)
        p_ref[...] = jnp.zeros(p_ref.shape, BF16)
        yattn_ref[...] = jnp.zeros(yattn_ref.shape, BF16)

    wg_ref[0] = (wg32_ref[0] * fg_ref[...]).astype(BF16)
    wu_ref[0] = (wu32_ref[0] * fg_ref[...]).astype(BF16)
    wd_ref[0] = wd32_ref[0].astype(BF16)

    def window_start(r):
        return pl.multiple_of(jnp.maximum(j * TQ_ATT + (r - 1) * WINDOW, 0), WINDOW)

    def stage_scores(r):
        start = window_start(r)
        for g in range(N_KV_HEADS):
            vb = vt_ref[0, g * HEAD_DIM:(g + 1) * HEAD_DIM, pl.ds(start, band)]
            vm = vtmeta_ref[g * HEAD_DIM:(g + 1) * HEAD_DIM, :]
            for hh in range(2):
                col = (2 * g + hh) * LANES
                kcat_ref[r, g, hh * band:(hh + 1) * band, :] = kd_ref[0, pl.ds(start, band), col:col + LANES]
                kcat_ref[r, g, m0 + hh * N_META:m0 + (hh + 1) * N_META, :] = kmeta_ref[:, col:col + LANES]
                rows = slice(hh * HEAD_DIM, (hh + 1) * HEAD_DIM)
                vbd_ref[r, g, rows, hh * band:(hh + 1) * band] = vb
                vbd_ref[r, g, rows, m0 + hh * N_META:m0 + (hh + 1) * N_META] = vm
            rq = slice(r * WINDOW, (r + 1) * WINDOW)
            qq = jnp.concatenate([q_ref[0, rq, (2 * g + ch) * LANES:(2 * g + ch + 1) * LANES]
                                  for ch in range(2)], axis=0)
            s_ref[r * N_KV_HEADS + g] = lax.dot_general(kcat_ref[r, g], qq, nt, preferred_element_type=F32)

    def stage_softmax(r):
        kpos = window_start(r) + lax.broadcasted_iota(jnp.int32, (band, WINDOW), 0)
        qpos = j * TQ_ATT + r * WINDOW + lax.broadcasted_iota(jnp.int32, (band, WINDOW), 1)
        d = qpos - kpos
        bias = jnp.where((d >= 0) & (d < WINDOW), 0.0, NEG_INF)
        bias = jnp.concatenate([bias, bias], axis=1)
        left = lax.broadcasted_iota(jnp.int32, (1, 2 * WINDOW), 1) < WINDOW
        for g in range(N_KV_HEADS):
            it = r * N_KV_HEADS + g
            for hh in range(2):
                sink = jnp.where(left, sink_ref[4 * g + hh], sink_ref[4 * g + 2 + hh])
                s = s_ref[it, hh * band:(hh + 1) * band, :] + bias
                sm = s_ref[it, m0 + hh * N_META:m0 + (hh + 1) * N_META, :]
                mx = jnp.maximum(_fold_rows(s, jnp.maximum, jnp.max), jnp.max(sm, axis=0, keepdims=True))
                mx = jnp.maximum(mx, sink)
                p = jnp.exp2(s - mx)
                pm = jnp.exp2(sm - mx)
                den = (_fold_rows(p, jnp.add, jnp.sum) + jnp.sum(pm, axis=0, keepdims=True)
                       + jnp.exp2(sink - mx))
                p_ref[it, hh * band:(hh + 1) * band, :] = p.astype(BF16)
                p_ref[it, m0 + hh * N_META:m0 + (hh + 1) * N_META, :] = pm.astype(BF16)
                inv_ref[it, hh * HEAD_DIM:(hh + 1) * HEAD_DIM, :] = jnp.broadcast_to(
                    1.0 / den, (HEAD_DIM, 2 * WINDOW))

    def stage_values(r):
        for g in range(N_KV_HEADS):
            it = r * N_KV_HEADS + g
            o_t = jnp.dot(vbd_ref[r, g], p_ref[it], preferred_element_type=F32) * inv_ref[it]
            for ch in range(2):
                c = 2 * g + ch
                yattn_ref[slot, r * WINDOW:(r + 1) * WINDOW, c * LANES:(c + 1) * LANES] = (
                    jnp.transpose(o_t[:, ch * WINDOW:(ch + 1) * WINDOW]).astype(BF16))

    def stage_out():
        mix = (jnp.dot(yp_ref[0], wout_ref[:POOL_WIDTH, :], preferred_element_type=F32)
               + jnp.dot(yattn_ref[1 - slot], wout_ref[POOL_WIDTH:, :], preferred_element_type=F32))
        h2 = x_ref[0] + mix
        h2_ref[0] = h2
        m = _rms_unit(h2).astype(BF16)
        m_ref[0] = m
        logits = jnp.dot(m, wr_ref[...], preferred_element_type=F32)
        for p in range(TQ_ATT // ROUTE_ROWS):
            logit_ref[p] = logits[p * ROUTE_ROWS:(p + 1) * ROUTE_ROWS, :]

    def stage_route(p):
        table, rows, counts = _route(logit_ref[p], upper_ref[...])
        table_ref[0, p * ROUTE_ROWS:(p + 1) * ROUTE_ROWS, :] = table
        rows_ref[p] = rows
        cnt_ref[p] = counts

    n_p = TQ_ATT // ROUTE_ROWS
    for t in range(max(n_r + 2, n_p + 1)):
        for k, stage in enumerate((stage_scores, stage_softmax, stage_values)):
            if 0 <= t - k < n_r:
                stage(t - k)
        if t == 0:
            stage_out()
        if 1 <= t <= n_p:
            stage_route(t - 1)


def _padded_rows(n):
    if isinstance(n, int):
        if n == 0:
            return 0
        return next((s for s in PASS_SIZES if n <= s), -(-n // SUB) * SUB)
    rows = lax.div(n + (SUB - 1), SUB) * SUB
    for size in reversed(PASS_SIZES):
        rows = jnp.where(n <= size, size, rows)
    return jnp.where(n == 0, 0, rows)


def _max_sorted_rows(n_tokens, n_groups):
    best = [0] + [-1] * n_tokens
    for _ in range(n_groups):
        nxt = [-1] * (n_tokens + 1)
        for used, rows in enumerate(best):
            if rows >= 0:
                for n in range(n_tokens - used + 1):
                    nxt[used + n] = max(nxt[used + n], rows + _padded_rows(n))
        best = nxt
    return best[n_tokens]


def _moe_kernel(cnt_ref, m_ref, table_ref, rows_ref, h2_ref, wg_ref, wu_ref, wd_ref, out_ref,
                ys_ref, hid_ref, plan_ref):
    i = pl.program_id(0)
    per = TM_MOE // ROUTE_ROWS
    counts = [[cnt_ref[(i * per + h) * N_GROUPS + g] for g in range(N_GROUPS)] for h in range(per)]
    base = []
    total_rows = 0
    for g in range(N_GROUPS):
        n_g = sum(counts[h][g] for h in range(per))
        base.append(total_rows)
        plan_ref[g] = n_g
        plan_ref[N_GROUPS + g] = total_rows + jnp.int32(0)
        total_rows = total_rows + _padded_rows(n_g)
    first = [[base[g] + sum(counts[hh][g] for hh in range(h)) for g in range(N_GROUPS)] for h in range(per)]

    @pl.when(i == 0)
    def _():
        ys_ref[...] = jnp.zeros(ys_ref.shape, BF16)

    def pick(gid, offsets):
        out = _as_f32(offsets[N_GROUPS - 1])
        for g in range(N_GROUPS - 2, -1, -1):
            out = jnp.where(gid == g, _as_f32(offsets[g]), out)
        return out

    table = table_ref[...]
    table_f = table.astype(F32)
    gid_col = table_f[:, GID_LANE:GID_LANE + 1]
    row_id = lax.broadcasted_iota(jnp.int32, (TM_MOE, 1), 0)
    off_col = pick(gid_col, first[per - 1])
    for h in range(per - 2, -1, -1):
        off_col = jnp.where(row_id < (h + 1) * ROUTE_ROWS, pick(gid_col, first[h]), off_col)
    pos_col = table_f[:, RANK_LANE:RANK_LANE + 1] + off_col
    pos_row = jnp.concatenate(
        [rows_ref[h, 1:2, :] + pick(rows_ref[h, 0:1, :], first[h]) for h in range(per)], axis=1)

    def run_pass(g, r0, size):
        rows = (r0 + lax.broadcasted_iota(jnp.int32, (size, 1), 0)).astype(F32)
        perm = jnp.where(pos_row == rows, 1.0, 0.0).astype(BF16)
        xs = jnp.dot(perm, m_ref[...], preferred_element_type=F32).astype(BF16)
        gs = jnp.dot(perm, table, preferred_element_type=F32)
        lane = lax.broadcasted_iota(jnp.int32, gs.shape, 1)
        for jj in range(EXPERTS_PER_GROUP):
            e = g * EXPERTS_PER_GROUP + jj
            gt = jnp.dot(xs, wg_ref[e], preferred_element_type=F32)
            up = jnp.dot(xs, wu_ref[e], preferred_element_type=F32)
            gate = jnp.sum(jnp.where((lane == e) | (lane == e + LO_SHIFT), gs, 0.0),
                           axis=-1, keepdims=True)
            hid = gt * (1.0 / (1.0 + jnp.exp(-gt))) * up * gate
            hid_ref[:size, jj * D_EXPERT:(jj + 1) * D_EXPERT] = hid.astype(BF16)
        w_down = wd_ref[pl.ds(pl.multiple_of(g * D_GROUP, D_GROUP), D_GROUP), :]
        y = jnp.dot(hid_ref[:size, :], w_down, preferred_element_type=F32)
        ys_ref[pl.ds(r0, size), :] = y.astype(BF16)

    def run_group(g, carry):
        n = plan_ref[g]
        r0 = pl.multiple_of(plan_ref[N_GROUPS + g], BF16_ROWS)
        below = 0
        for size in PASS_SIZES:
            @pl.when((n > below) & (n <= size))
            def _(size=size):
                run_pass(g, r0, size)
            below = size

        @pl.when(n > PASS_SIZES[-1])
        def _():
            def body(c, inner):
                run_pass(g, pl.multiple_of(r0 + c * SUB, BF16_ROWS), SUB)
                return inner
            lax.fori_loop(0, lax.div(n + (SUB - 1), SUB), body, 0)
        return carry

    lax.fori_loop(0, N_GROUPS, run_group, 0)

    def scatter(n_rows):
        cols = lax.broadcasted_iota(jnp.int32, (1, n_rows), 1).astype(F32)
        perm_t = jnp.where(pos_col == cols, 1.0, 0.0).astype(BF16)
        out_ref[...] = h2_ref[...] + jnp.dot(perm_t, ys_ref[:n_rows, :], preferred_element_type=F32)

    short = total_rows <= YS_SHORT

    @pl.when(short)
    def _():
        scatter(YS_SHORT)

    @pl.when(jnp.logical_not(short))
    def _():
        scatter(YS_ROWS)


YS_ROWS = -(-_max_sorted_rows(TM_MOE, N_GROUPS) // MXU_DEPTH) * MXU_DEPTH
YS_SHORT = YS_ROWS - MXU_DEPTH


def _as_f32(v):
    return float(v) if isinstance(v, int) else v.astype(F32)


def _const_spec(shape):
    n = len(shape)
    return pl.BlockSpec(shape, lambda *_: (0,) * n)


def _resident_spec(shape):
    n = len(shape)
    return pl.BlockSpec(shape, lambda *_: (0,) * n, pipeline_mode=pl.Buffered(1))


def kernel(x, meta_tokens, attn_norm_gain, w_in, w_pool, pool_scale, q_norm_gain, k_norm_gain,
           attn_sinks, w_out, ffn_norm_gain, w_group_router, w_expert_router, w_gate, w_up, w_down):
    B, S, D = x.shape
    assert D == D_MODEL and S % TQ_IN == 0 and S % TQ_ATT == 0 and (B * S) % TM_MOE == 0
    assert w_in.shape[0] == 1, "single layer"
    T = B * S

    rope = _rope_tables(N_META + S)
    fgain = ffn_norm_gain[0][:, None]
    qgain2 = jnp.tile(q_norm_gain[0] * LOG2_E, 2)[None, :]
    kgain2 = jnp.tile(k_norm_gain[0] * (HEAD_DIM ** 0.5), 2)[None, :]
    pscale = pool_scale[0][None, :]
    w_in_b = (attn_norm_gain[0][:, None] * w_in[0]).astype(BF16)
    w_pool_b = w_pool[0].astype(BF16)
    w_out_b = w_out[0].astype(BF16)
    w_r = (fgain * jnp.concatenate(
        [w_expert_router[0], w_group_router[0],
         jnp.zeros((D, LANES - N_EXPERTS - N_GROUPS), F32)], axis=1)).astype(BF16)
    params = pltpu.CompilerParams(vmem_limit_bytes=VMEM_LIMIT)

    u_meta, kd_meta, v_meta = pl.pallas_call(
        _meta_kernel,
        out_shape=(jax.ShapeDtypeStruct((N_META, POOL_WIDTH), F32),
                   jax.ShapeDtypeStruct((N_META, KV_EXP), BF16),
                   jax.ShapeDtypeStruct((N_META, KV_WIDTH), BF16)),
        compiler_params=params,
        name="meta_proj",
    )(meta_tokens, w_in_b, kgain2, rope[:N_META])
    vt_meta = v_meta.T

    yp, q, kd, vt = pl.pallas_call(
        _in_kernel,
        grid=(B, S // TQ_IN),
        in_specs=[
            pl.BlockSpec((1, TQ_IN, D), lambda b, j: (b, j, 0)),
            _const_spec((N_META, POOL_WIDTH)),
            _const_spec((D, IN_WIDTH)),
            _const_spec((len(POOL_WINDOWS), POOL_GROUP, POOL_GROUP)),
            _const_spec((1, POOL_WIDTH)),
            _const_spec((1, LANES)),
            _const_spec((1, LANES)),
            pl.BlockSpec((TQ_IN, 3 * LANES), lambda b, j: (j, 0)),
        ],
        out_specs=(
            pl.BlockSpec((1, TQ_IN, POOL_WIDTH), lambda b, j: (b, j, 0)),
            pl.BlockSpec((1, TQ_IN, ATTN_WIDTH), lambda b, j: (b, j, 0)),
            pl.BlockSpec((1, TQ_IN, KV_EXP), lambda b, j: (b, j, 0)),
            pl.BlockSpec((1, KV_WIDTH, TQ_IN), lambda b, j: (b, 0, j)),
        ),
        out_shape=(jax.ShapeDtypeStruct((B, S, POOL_WIDTH), BF16),
                   jax.ShapeDtypeStruct((B, S, ATTN_WIDTH), BF16),
                   jax.ShapeDtypeStruct((B, S, KV_EXP), BF16),
                   jax.ShapeDtypeStruct((B, KV_WIDTH, S), BF16)),
        scratch_shapes=[pltpu.VMEM((N_META, POOL_WIDTH), F32),
                        pltpu.VMEM((TQ_IN, IN_WIDTH), F32)],
        compiler_params=pltpu.CompilerParams(
            dimension_semantics=("arbitrary", "arbitrary"), vmem_limit_bytes=VMEM_LIMIT),
        name="in_proj",
    )(x, u_meta, w_in_b, w_pool_b, pscale, qgain2, kgain2, rope[N_META:])

    upper = jnp.asarray(np.triu(np.ones((ROUTE_ROWS, ROUTE_ROWS), np.float32), 1), dtype=BF16)
    per_row = S // TQ_ATT
    n_tiles = B * per_row

    def cur(s):
        t = jnp.minimum(s, n_tiles - 1)
        return t // per_row, t % per_row

    def lag(s):
        t = jnp.maximum(s - 1, 0)
        return t // per_row, t % per_row

    def lag_rows(s):
        return (*lag(s), 0)

    def lag_flat(s):
        return (jnp.maximum(s - 1, 0), 0, 0)

    w_steps = N_EXPERTS * W_SLICES
    assert n_tiles >= w_steps and D % W_SLICES == 0 and D_EXPERT % W_SLICES == 0

    def w_slice(s):
        k = jnp.minimum(s, w_steps - 1)
        return k // W_SLICES, k % W_SLICES, 0

    h2, m, table, rows, cnt, wg, wu, wd = pl.pallas_call(
        functools.partial(_attn_kernel, n_tiles, per_row),
        grid=(n_tiles + 1,),
        in_specs=[
            pl.BlockSpec(memory_space=pltpu.SMEM),
            pl.BlockSpec((1, TQ_ATT, D), lag_rows),
            pl.BlockSpec((1, TQ_ATT, POOL_WIDTH), lag_rows),
            pl.BlockSpec((1, TQ_ATT, ATTN_WIDTH), lambda s: (*cur(s), 0)),
            pl.BlockSpec((1, S, KV_EXP), lambda s: (cur(s)[0], 0, 0)),
            pl.BlockSpec((1, KV_WIDTH, S), lambda s: (cur(s)[0], 0, 0)),
            _const_spec((N_META, KV_EXP)),
            _const_spec((KV_WIDTH, N_META)),
            _const_spec((D, D)),
            _const_spec((D, LANES)),
            _const_spec((ROUTE_ROWS, ROUTE_ROWS)),
            pl.BlockSpec((D // W_SLICES, 1), lambda s: (w_slice(s)[1], 0)),
            pl.BlockSpec((1, D // W_SLICES, D_EXPERT), w_slice),
            pl.BlockSpec((1, D // W_SLICES, D_EXPERT), w_slice),
            pl.BlockSpec((1, D_EXPERT // W_SLICES, D), w_slice),
        ],
        out_specs=(
            pl.BlockSpec((1, TQ_ATT, D), lag_rows),
            pl.BlockSpec((1, TQ_ATT, D), lag_rows),
            pl.BlockSpec((1, TQ_ATT, LANES), lag_rows),
            pl.BlockSpec((TQ_ATT // ROUTE_ROWS, SUBLANES, ROUTE_ROWS), lag_flat),
            pl.BlockSpec((TQ_ATT // ROUTE_ROWS, SUBLANES, LANES), lag_flat),
            pl.BlockSpec((1, D // W_SLICES, D_EXPERT), w_slice),
            pl.BlockSpec((1, D // W_SLICES, D_EXPERT), w_slice),
            pl.BlockSpec((1, D_EXPERT // W_SLICES, D), w_slice),
        ),
        out_shape=(jax.ShapeDtypeStruct((B, S, D), F32),
                   jax.ShapeDtypeStruct((B, S, D), BF16),
                   jax.ShapeDtypeStruct((B, S, LANES), BF16),
                   jax.ShapeDtypeStruct((T // ROUTE_ROWS, SUBLANES, ROUTE_ROWS), F32),
                   jax.ShapeDtypeStruct((T // ROUTE_ROWS, SUBLANES, LANES), jnp.int32),
                   jax.ShapeDtypeStruct((N_EXPERTS, D, D_EXPERT), BF16),
                   jax.ShapeDtypeStruct((N_EXPERTS, D, D_EXPERT), BF16),
                   jax.ShapeDtypeStruct((N_EXPERTS, D_EXPERT, D), BF16)),
        scratch_shapes=[
            pltpu.VMEM((2, TQ_ATT, ATTN_WIDTH), BF16),
            pltpu.VMEM((TQ_ATT // WINDOW, N_KV_HEADS, KEY_ROWS, LANES), BF16),
            pltpu.VMEM((TQ_ATT // WINDOW, N_KV_HEADS, LANES, KEY_PAD), BF16),
            pltpu.VMEM((N_ITEMS, KEY_ROWS, 2 * LANES), F32),
            pltpu.VMEM((N_ITEMS, KEY_PAD, 2 * LANES), BF16),
            pltpu.VMEM((N_ITEMS, LANES, 2 * LANES), F32),
            pltpu.VMEM((TQ_ATT // ROUTE_ROWS, ROUTE_ROWS, LANES), F32),
        ],
        compiler_params=pltpu.CompilerParams(
            dimension_semantics=("arbitrary",), vmem_limit_bytes=VMEM_LIMIT),
        name="attn_out",
    )(attn_sinks[0] * LOG2_E, x, yp, q, kd, vt, kd_meta, vt_meta, w_out_b, w_r, upper,
      fgain, w_gate[0], w_up[0], w_down[0])

    cnt_flat = cnt[:, :N_GROUPS, 0].reshape(-1)
    out = pl.pallas_call(
        _moe_kernel,
        grid_spec=pltpu.PrefetchScalarGridSpec(
            num_scalar_prefetch=1,
            grid=(T // TM_MOE,),
            in_specs=[
                pl.BlockSpec((TM_MOE, D), lambda i, c: (i, 0)),
                pl.BlockSpec((TM_MOE, LANES), lambda i, c: (i, 0)),
                pl.BlockSpec((TM_MOE // ROUTE_ROWS, SUBLANES, ROUTE_ROWS), lambda i, c: (i, 0, 0)),
                pl.BlockSpec((TM_MOE, D), lambda i, c: (i, 0)),
                _resident_spec((N_EXPERTS, D, D_EXPERT)),
                _resident_spec((N_EXPERTS, D, D_EXPERT)),
                _resident_spec((N_EXPERTS * D_EXPERT, D)),
            ],
            out_specs=pl.BlockSpec((TM_MOE, D), lambda i, c: (i, 0)),
            scratch_shapes=[pltpu.VMEM((YS_ROWS, D), BF16),
                            pltpu.VMEM((max(SUB, *PASS_SIZES), D_GROUP), BF16),
                            pltpu.SMEM((2 * N_GROUPS,), jnp.int32)],
        ),
        out_shape=jax.ShapeDtypeStruct((T, D), F32),
        compiler_params=pltpu.CompilerParams(
            dimension_semantics=("arbitrary",), vmem_limit_bytes=VMEM_LIMIT),
        name="moe",
    )(cnt_flat, m.reshape(T, D), table.reshape(T, LANES), rows, h2.reshape(T, D), wg, wu,
      wd.reshape(N_EXPERTS * D_EXPERT, D))
    return out.reshape(B, S, D)
```

```python
import functools

import numpy as np
import jax
import jax.numpy as jnp
from jax import lax
from jax.experimental import pallas as pl
from jax.experimental.pallas import tpu as pltpu

D_MODEL = 1024
N_META = 16
POOL_WIDTH = 512
POOL_WINDOWS = (2, 4, 8, 16)
POOL_GROUP = 128
HEAD_DIM = 64
N_HEADS = 8
N_KV_HEADS = 2
ATTN_WIDTH = N_HEADS * HEAD_DIM
KV_WIDTH = N_KV_HEADS * HEAD_DIM
WINDOW = 128
ROT_DIM = HEAD_DIM // 4
ROPE_THETA = 500000.0
IN_WIDTH = POOL_WIDTH + ATTN_WIDTH + 2 * KV_WIDTH
N_GROUPS = 4
EXPERTS_PER_GROUP = 4
N_EXPERTS = 16
D_EXPERT = 256
EPS = 1e-6
NEG_INF = -1e30
LOG2_E = 1.4426950408889634

LANES = 128
SUBLANES = 8
KV_EXP = 4 * LANES
TQ_IN = 2048
RB_IN = 256
TQ_ATT = 512
KEY_ROWS = 2 * (2 * WINDOW + N_META)
KEY_PAD = -(-KEY_ROWS // LANES) * LANES
OUT_BLOCKS = 2
ROUTE_ROWS = OUT_BLOCKS * WINDOW
W_SLICES = 4
N_ITEMS = (TQ_ATT // WINDOW) * N_KV_HEADS
TM_MOE = 512
SUB = 144
PASS_SIZES = (128, 144, 160)
BF16_ROWS = 16
MXU_DEPTH = 256
GID_LANE = N_EXPERTS
RANK_LANE = 24
LO_SHIFT = 32
D_GROUP = EXPERTS_PER_GROUP * D_EXPERT
VMEM_LIMIT = 56 * 1024 * 1024

BF16 = jnp.bfloat16
F32 = jnp.float32


def _rope_tables(n_pos):
    half = ROT_DIM // 2
    inv_freq = 1.0 / (ROPE_THETA ** (np.arange(half, dtype=np.float64) / half))
    ang = np.arange(n_pos, dtype=np.float64)[:, None] * inv_freq[None, :]
    cos, sin = np.cos(ang), np.sin(ang)
    c = np.ones((n_pos, HEAD_DIM)); c[:, :half] = cos; c[:, half:ROT_DIM] = cos
    sa = np.zeros((n_pos, HEAD_DIM)); sa[:, :half] = -sin
    sb = np.zeros((n_pos, HEAD_DIM)); sb[:, half:ROT_DIM] = sin
    tab = np.concatenate([np.tile(c, (1, 2)), np.tile(sa, (1, 2)), np.tile(sb, (1, 2))], axis=1)
    return jnp.asarray(tab, dtype=F32)


def _rms_unit(x):
    ms = jnp.mean(x * x, axis=-1, keepdims=True)
    return x * lax.rsqrt(ms + EPS)


def _head_norm_rope_many(xs, gains, rope):
    lo = lax.broadcasted_iota(jnp.int32, xs[0].shape, 1) < HEAD_DIM
    sums = []
    for x in xs:
        sq = x * x
        sums.append((jnp.sum(jnp.where(lo, sq, 0.0), axis=-1, keepdims=True),
                     jnp.sum(jnp.where(lo, 0.0, sq), axis=-1, keepdims=True)))
    ys = [x * lax.rsqrt(jnp.where(lo, s_lo, s_hi) + HEAD_DIM * EPS) * g
          for x, g, (s_lo, s_hi) in zip(xs, gains, sums)]
    half = ROT_DIM // 2
    rolled = [(pltpu.roll(y, LANES - half, 1), pltpu.roll(y, half, 1)) for y in ys]
    c, sa, sb = rope[:, 0:LANES], rope[:, LANES:2 * LANES], rope[:, 2 * LANES:3 * LANES]
    return [y * c + ra * sa + rb * sb for y, (ra, rb) in zip(ys, rolled)]


def _head_norm_rope(xc, gain2, rope):
    return _head_norm_rope_many([xc], [gain2], rope)[0]


def _expand_kv(t):
    lo = lax.broadcasted_iota(jnp.int32, t.shape, 1) < HEAD_DIM
    sw = pltpu.roll(t, HEAD_DIM, 1)
    z = jnp.zeros_like(t)
    return jnp.concatenate([jnp.where(lo, t, z), jnp.where(lo, z, sw),
                            jnp.where(lo, sw, z), jnp.where(lo, z, t)], axis=1)


def _project(x, w_in):
    return jnp.dot(_rms_unit(x).astype(BF16), w_in, preferred_element_type=F32)


def _meta_kernel(meta_ref, win_ref, kgain_ref, rope_ref, u_ref, kd_ref, v_ref):
    proj = _project(meta_ref[...], win_ref[...])
    u_ref[...] = proj[:, :POOL_WIDTH]
    k = proj[:, POOL_WIDTH + ATTN_WIDTH:POOL_WIDTH + ATTN_WIDTH + KV_WIDTH]
    v = proj[:, POOL_WIDTH + ATTN_WIDTH + KV_WIDTH:]
    k = _head_norm_rope(k, kgain_ref[...], rope_ref[...])
    kd_ref[...] = _expand_kv(k).astype(BF16)
    v_ref[...] = v.astype(BF16)


def _in_kernel(x_ref, umeta_ref, win_ref, wpool_ref, pscale_ref, qgain_ref, kgain_ref,
               rope_ref, yp_ref, q_ref, kd_ref, vt_ref, carry_ref, proj_ref):
    j = pl.program_id(1)

    @pl.when(j == 0)
    def _():
        carry_ref[...] = umeta_ref[...]

    sizes = [RB_IN] * (TQ_IN // RB_IN - 1) + [RB_IN // 2] * 2
    blocks = [slice(sum(sizes[:i]), sum(sizes[:i + 1])) for i in range(len(sizes))]
    n_sub = len(blocks)

    def normalise(i):
        return _rms_unit(x_ref[0, blocks[i], :]).astype(BF16)

    def project(i, a):
        proj_ref[blocks[i], :] = jnp.dot(a, win_ref[...], preferred_element_type=F32)

    def finish(i):
        rows = blocks[i]
        rope = rope_ref[rows, :]
        u = proj_ref[rows, :POOL_WIDTH]
        acc = jnp.concatenate([carry_ref[...], u], axis=0)
        for gi, w in enumerate(POOL_WINDOWS):
            lo = gi * POOL_GROUP
            acc = acc[:, POOL_GROUP * (1 if gi else 0):]
            acc = acc + pltpu.roll(acc, w // 2, 0)
            mixed = acc[N_META:, :POOL_GROUP] * (1.0 / w) - u[:, lo:lo + POOL_GROUP]
            y = jnp.dot(mixed.astype(BF16), wpool_ref[gi], preferred_element_type=F32)
            yp_ref[0, rows, lo:lo + POOL_GROUP] = (y * pscale_ref[:, lo:lo + POOL_GROUP]).astype(BF16)
        carry_ref[...] = u[sizes[i] - N_META:, :]

        n_qc = ATTN_WIDTH // LANES
        xs = [proj_ref[rows, POOL_WIDTH + c * LANES:POOL_WIDTH + (c + 1) * LANES] for c in range(n_qc + 1)]
        outs = _head_norm_rope_many(xs, [qgain_ref[...]] * n_qc + [kgain_ref[...]], rope)
        for c in range(n_qc):
            q_ref[0, rows, c * LANES:(c + 1) * LANES] = outs[c].astype(BF16)
        kd_ref[0, rows, :] = _expand_kv(outs[n_qc]).astype(BF16)
        v = proj_ref[rows, POOL_WIDTH + ATTN_WIDTH + KV_WIDTH:]
        vt_ref[0, :, rows] = jnp.transpose(v).astype(BF16)

    a_next = normalise(0)
    for i in range(n_sub):
        project(i, a_next)
        if i + 1 < n_sub:
            a_next = normalise(i + 1)
        if i > 0:
            finish(i - 1)
    finish(n_sub - 1)


def _route(logits, upper):
    n = logits.shape[0]
    lt = jnp.transpose(logits)
    sub = SUBLANES
    row8 = lax.broadcasted_iota(jnp.int32, (sub, n), 0).astype(F32)
    row16 = lax.broadcasted_iota(jnp.int32, (N_EXPERTS, n), 0).astype(F32)
    g_ok = row8 < N_GROUPS
    gl = jnp.where(g_ok, lt[N_EXPERTS:N_EXPERTS + sub, :], NEG_INF)
    gmax = jnp.max(gl, axis=0, keepdims=True)
    gsum = jnp.sum(jnp.where(g_ok, jnp.exp(gl - gmax), 0.0), axis=0, keepdims=True)
    g_prob = 1.0 / gsum
    g_idx = jnp.min(jnp.where(gl == gmax, row8, float(sub)), axis=0, keepdims=True)
    e_lo = g_idx * EXPERTS_PER_GROUP
    emask = (row16 >= e_lo) & (row16 < e_lo + EXPERTS_PER_GROUP)
    el = jnp.where(emask, lt[:N_EXPERTS, :], NEG_INF)
    big = float(N_EXPERTS)
    e1 = jnp.max(el, axis=0, keepdims=True)
    i1 = jnp.min(jnp.where(el == e1, row16, big), axis=0, keepdims=True)
    el2 = jnp.where(row16 == i1, NEG_INF, el)
    e2 = jnp.max(el2, axis=0, keepdims=True)
    i2 = jnp.min(jnp.where(el2 == e2, row16, big), axis=0, keepdims=True)
    t = jnp.exp(e2 - e1)
    w1 = 1.0 / (1.0 + t)
    w2 = t * w1
    gates_t = jnp.where(row16 == i1, w1 * g_prob, 0.0) + jnp.where(row16 == i2, w2 * g_prob, 0.0)
    onehot = jnp.where(row8 == g_idx, 1.0, 0.0)
    earlier = jnp.dot(onehot.astype(BF16), upper, preferred_element_type=F32)
    rank = jnp.sum(onehot * earlier, axis=0, keepdims=True)
    hi = gates_t.astype(BF16).astype(F32)
    lo = (gates_t - hi).astype(BF16).astype(F32)
    gid8 = jnp.broadcast_to(g_idx, (sub, n))
    rank8 = jnp.broadcast_to(rank, (sub, n))
    assert GID_LANE == N_EXPERTS and RANK_LANE == GID_LANE + sub and LO_SHIFT == RANK_LANE + sub
    full_t = jnp.concatenate(
        [hi, gid8, rank8, lo, jnp.zeros((LANES - LO_SHIFT - N_EXPERTS, n), F32)], axis=0)
    rows = jnp.concatenate([g_idx, rank, jnp.zeros((sub - 2, n), F32)], axis=0)
    counts = jnp.sum(onehot, axis=1, keepdims=True)
    return (jnp.transpose(full_t).astype(BF16), rows,
            jnp.broadcast_to(counts, (sub, LANES)).astype(jnp.int32))


def _fold_rows(x, pair, final):
    rows = x.shape[0]
    while rows > SUBLANES:
        rows //= 2
        x = pair(x[:rows], x[rows:])
    return final(x, axis=0, keepdims=True)


def _attn_kernel(n_tiles, tiles_per_row, sink_ref, x_ref, yp_ref, q_ref, kd_ref, vt_ref, kmeta_ref, vtmeta_ref, wout_ref,
                 wr_ref, upper_ref, fg_ref, wg32_ref, wu32_ref, wd32_ref,
                 h2_ref, m_ref, table_ref, rows_ref, cnt_ref, wg_ref, wu_ref, wd_ref,
                 yattn_ref, kcat_ref, vbd_ref, s_ref, p_ref, inv_ref, logit_ref):
    step = pl.program_id(0)
    tile = jnp.minimum(step, n_tiles - 1)
    j = lax.rem(tile, tiles_per_row)
    slot = lax.rem(step, 2)
    nt = (((1,), (1,)), ((), ()))
    n_r = TQ_ATT // WINDOW
    band = 2 * WINDOW
    m0 = 2 * band

    @pl.when(step == 0)
    def _():
        vbd_ref[...] = jnp.zeros(vbd_ref.shape, BF16)
        p_ref[...] = jnp.zeros(p_ref.shape, BF16)
        yattn_ref[...] = jnp.zeros(yattn_ref.shape, BF16)

    wg_ref[0] = (wg32_ref[0] * fg_ref[...]).astype(BF16)
    wu_ref[0] = (wu32_ref[0] * fg_ref[...]).astype(BF16)
    wd_ref[0] = wd32_ref[0].astype(BF16)

    def window_start(r):
        return pl.multiple_of(jnp.maximum(j * TQ_ATT + (r - 1) * WINDOW, 0), WINDOW)

    def stage_scores(r):
        start = window_start(r)
        for g in range(N_KV_HEADS):
            vb = vt_ref[0, g * HEAD_DIM:(g + 1) * HEAD_DIM, pl.ds(start, band)]
            vm = vtmeta_ref[g * HEAD_DIM:(g + 1) * HEAD_DIM, :]
            for hh in range(2):
                col = (2 * g + hh) * LANES
                kcat_ref[r, g, hh * band:(hh + 1) * band, :] = kd_ref[0, pl.ds(start, band), col:col + LANES]
                kcat_ref[r, g, m0 + hh * N_META:m0 + (hh + 1) * N_META, :] = kmeta_ref[:, col:col + LANES]
                rows = slice(hh * HEAD_DIM, (hh + 1) * HEAD_DIM)
                vbd_ref[r, g, rows, hh * band:(hh + 1) * band] = vb
                vbd_ref[r, g, rows, m0 + hh * N_META:m0 + (hh + 1) * N_META] = vm
            rq = slice(r * WINDOW, (r + 1) * WINDOW)
            qq = jnp.concatenate([q_ref[0, rq, (2 * g + ch) * LANES:(2 * g + ch + 1) * LANES]
                                  for ch in range(2)], axis=0)
            s_ref[r * N_KV_HEADS + g] = lax.dot_general(kcat_ref[r, g], qq, nt, preferred_element_type=F32)

    def stage_softmax(r):
        kpos = window_start(r) + lax.broadcasted_iota(jnp.int32, (band, WINDOW), 0)
        qpos = j * TQ_ATT + r * WINDOW + lax.broadcasted_iota(jnp.int32, (band, WINDOW), 1)
        d = qpos - kpos
        bias = jnp.where((d >= 0) & (d < WINDOW), 0.0, NEG_INF)
        bias = jnp.concatenate([bias, bias], axis=1)
        left = lax.broadcasted_iota(jnp.int32, (1, 2 * WINDOW), 1) < WINDOW
        for g in range(N_KV_HEADS):
            it = r * N_KV_HEADS + g
            for hh in range(2):
                sink = jnp.where(left, sink_ref[4 * g + hh], sink_ref[4 * g + 2 + hh])
                s = s_ref[it, hh * band:(hh + 1) * band, :] + bias
                sm = s_ref[it, m0 + hh * N_META:m0 + (hh + 1) * N_META, :]
                mx = jnp.maximum(_fold_rows(s, jnp.maximum, jnp.max), jnp.max(sm, axis=0, keepdims=True))
                mx = jnp.maximum(mx, sink)
                p = jnp.exp2(s - mx)
                pm = jnp.exp2(sm - mx)
                den = (_fold_rows(p, jnp.add, jnp.sum) + jnp.sum(pm, axis=0, keepdims=True)
                       + jnp.exp2(sink - mx))
                p_ref[it, hh * band:(hh + 1) * band, :] = p.astype(BF16)
                p_ref[it, m0 + hh * N_META:m0 + (hh + 1) * N_META, :] = pm.astype(BF16)
                inv_ref[it, hh * HEAD_DIM:(hh + 1) * HEAD_DIM, :] = jnp.broadcast_to(
                    1.0 / den, (HEAD_DIM, 2 * WINDOW))

    def stage_values(r):
        for g in range(N_KV_HEADS):
            it = r * N_KV_HEADS + g
            o_t = jnp.dot(vbd_ref[r, g], p_ref[it], preferred_element_type=F32) * inv_ref[it]
            for ch in range(2):
                c = 2 * g + ch
                yattn_ref[slot, r * WINDOW:(r + 1) * WINDOW, c * LANES:(c + 1) * LANES] = (
                    jnp.transpose(o_t[:, ch * WINDOW:(ch + 1) * WINDOW]).astype(BF16))

    def stage_out():
        mix = (jnp.dot(yp_ref[0], wout_ref[:POOL_WIDTH, :], preferred_element_type=F32)
               + jnp.dot(yattn_ref[1 - slot], wout_ref[POOL_WIDTH:, :], preferred_element_type=F32))
        h2 = x_ref[0] + mix
        h2_ref[0] = h2
        m = _rms_unit(h2).astype(BF16)
        m_ref[0] = m
        logits = jnp.dot(m, wr_ref[...], preferred_element_type=F32)
        for p in range(TQ_ATT // ROUTE_ROWS):
            logit_ref[p] = logits[p * ROUTE_ROWS:(p + 1) * ROUTE_ROWS, :]

    def stage_route(p):
        table, rows, counts = _route(logit_ref[p], upper_ref[...])
        table_ref[0, p * ROUTE_ROWS:(p + 1) * ROUTE_ROWS, :] = table
        rows_ref[p] = rows
        cnt_ref[p] = counts

    n_p = TQ_ATT // ROUTE_ROWS
    for t in range(max(n_r + 2, n_p + 1)):
        for k, stage in enumerate((stage_scores, stage_softmax, stage_values)):
            if 0 <= t - k < n_r:
                stage(t - k)
        if t == 0:
            stage_out()
        if 1 <= t <= n_p:
            stage_route(t - 1)


def _pair_plan(na, nb):
    top = PASS_SIZES[-1]
    if isinstance(na, int) and isinstance(nb, int):
        if 1 <= na <= top and 1 <= nb <= top:
            size = next(s for s in PASS_SIZES if max(na, nb) <= s)
            return 1, size, size, size
        return 0, SUB, -(-na // SUB) * SUB, -(-nb // SUB) * SUB
    fast = (na >= 1) & (na <= top) & (nb >= 1) & (nb <= top)
    size = jnp.int32(top)
    for s in reversed(PASS_SIZES[:-1]):
        size = jnp.where(jnp.maximum(na, nb) <= s, s, size)
    slow = [lax.div(n + (SUB - 1), SUB) * SUB for n in (na, nb)]
    return (fast.astype(jnp.int32), size,
            jnp.where(fast, size, slow[0]), jnp.where(fast, size, slow[1]))


def _max_sorted_rows(n_tokens):
    pair = [max(sum(_pair_plan(na, t - na)[2:]) for na in range(t + 1)) for t in range(n_tokens + 1)]
    return max(pair[t] + pair[n_tokens - t] for t in range(n_tokens + 1))


def _moe_kernel(cnt_ref, m_ref, table_ref, rows_ref, h2_ref, wg_ref, wu_ref, wd_ref, out_ref,
                ys_ref, hid_ref, plan_ref):
    i = pl.program_id(0)
    per = TM_MOE // ROUTE_ROWS
    counts = [[cnt_ref[(i * per + h) * N_GROUPS + g] for g in range(N_GROUPS)] for h in range(per)]
    n_tok = [sum(counts[h][g] for h in range(per)) for g in range(N_GROUPS)]
    base = []
    total_rows = 0
    for p in range(N_GROUPS // 2):
        fast, size, rows_a, rows_b = _pair_plan(n_tok[2 * p], n_tok[2 * p + 1])
        plan_ref[2 * N_GROUPS + p] = fast
        plan_ref[2 * N_GROUPS + N_GROUPS // 2 + p] = size
        for g, rows in ((2 * p, rows_a), (2 * p + 1, rows_b)):
            base.append(total_rows)
            plan_ref[g] = n_tok[g]
            plan_ref[N_GROUPS + g] = total_rows + jnp.int32(0)
            total_rows = total_rows + rows
    first = [[base[g] + sum(counts[hh][g] for hh in range(h)) for g in range(N_GROUPS)] for h in range(per)]

    @pl.when(i == 0)
    def _():
        ys_ref[...] = jnp.zeros(ys_ref.shape, BF16)

    def pick(gid, offsets):
        out = _as_f32(offsets[N_GROUPS - 1])
        for g in range(N_GROUPS - 2, -1, -1):
            out = jnp.where(gid == g, _as_f32(offsets[g]), out)
        return out

    table = table_ref[...]
    table_f = table.astype(F32)
    gid_col = table_f[:, GID_LANE:GID_LANE + 1]
    row_id = lax.broadcasted_iota(jnp.int32, (TM_MOE, 1), 0)
    off_col = pick(gid_col, first[per - 1])
    for h in range(per - 2, -1, -1):
        off_col = jnp.where(row_id < (h + 1) * ROUTE_ROWS, pick(gid_col, first[h]), off_col)
    pos_col = table_f[:, RANK_LANE:RANK_LANE + 1] + off_col
    pos_row = jnp.concatenate(
        [rows_ref[h, 1:2, :] + pick(rows_ref[h, 0:1, :], first[h]) for h in range(per)], axis=1)

    def gather(g, r0, size):
        rows = (r0 + lax.broadcasted_iota(jnp.int32, (size, 1), 0)).astype(F32)
        perm = jnp.where(pos_row == rows, 1.0, 0.0).astype(BF16)
        xs = jnp.dot(perm, m_ref[...], preferred_element_type=F32).astype(BF16)
        gs = jnp.dot(perm, table, preferred_element_type=F32)
        return xs, gs

    def experts_up(g, xs, gs, size, slot):
        lane = lax.broadcasted_iota(jnp.int32, gs.shape, 1)
        for jj in range(EXPERTS_PER_GROUP):
            e = g * EXPERTS_PER_GROUP + jj
            gt = jnp.dot(xs, wg_ref[e], preferred_element_type=F32)
            up = jnp.dot(xs, wu_ref[e], preferred_element_type=F32)
            gate = jnp.sum(jnp.where((lane == e) | (lane == e + LO_SHIFT), gs, 0.0),
                           axis=-1, keepdims=True)
            hid = gt * (1.0 / (1.0 + jnp.exp(-gt))) * up * gate
            hid_ref[slot, :size, jj * D_EXPERT:(jj + 1) * D_EXPERT] = hid.astype(BF16)

    def experts_down(g, r0, size, slot):
        w_down = wd_ref[pl.ds(pl.multiple_of(g * D_GROUP, D_GROUP), D_GROUP), :]
        y = jnp.dot(hid_ref[slot, :size, :], w_down, preferred_element_type=F32)
        ys_ref[pl.ds(r0, size), :] = y.astype(BF16)

    def run_passes(jobs, size):
        got = [gather(g, r0, size) for g, r0 in jobs]
        for slot, ((g, _), (xs, gs)) in enumerate(zip(jobs, got)):
            experts_up(g, xs, gs, size, slot)
        for slot, (g, r0) in enumerate(jobs):
            experts_down(g, r0, size, slot)

    def run_pair(p, carry):
        groups = (2 * p, 2 * p + 1)
        firsts = [pl.multiple_of(plan_ref[N_GROUPS + g], BF16_ROWS) for g in groups]
        fast = plan_ref[2 * N_GROUPS + p]
        size_p = plan_ref[2 * N_GROUPS + N_GROUPS // 2 + p]
        for size in PASS_SIZES:
            @pl.when((fast == 1) & (size_p == size))
            def _(size=size):
                run_passes(list(zip(groups, firsts)), size)

        @pl.when(fast == 0)
        def _():
            for g, r0 in zip(groups, firsts):
                def body(c, inner, g=g, r0=r0):
                    run_passes([(g, pl.multiple_of(r0 + c * SUB, BF16_ROWS))], SUB)
                    return inner
                lax.fori_loop(0, lax.div(plan_ref[g] + (SUB - 1), SUB), body, 0)
        return carry

    lax.fori_loop(0, N_GROUPS // 2, run_pair, 0)

    def scatter(n_rows):
        cols = lax.broadcasted_iota(jnp.int32, (1, n_rows), 1).astype(F32)
        perm_t = jnp.where(pos_col == cols, 1.0, 0.0).astype(BF16)
        out_ref[...] = h2_ref[...] + jnp.dot(perm_t, ys_ref[:n_rows, :], preferred_element_type=F32)

    short = total_rows <= YS_SHORT

    @pl.when(short)
    def _():
        scatter(YS_SHORT)

    @pl.when(jnp.logical_not(short))
    def _():
        scatter(YS_ROWS)


YS_ROWS = -(-_max_sorted_rows(TM_MOE) // MXU_DEPTH) * MXU_DEPTH
YS_SHORT = YS_ROWS - MXU_DEPTH


def _as_f32(v):
    return float(v) if isinstance(v, int) else v.astype(F32)


def _const_spec(shape):
    n = len(shape)
    return pl.BlockSpec(shape, lambda *_: (0,) * n)


def _resident_spec(shape):
    n = len(shape)
    return pl.BlockSpec(shape, lambda *_: (0,) * n, pipeline_mode=pl.Buffered(1))


def kernel(x, meta_tokens, attn_norm_gain, w_in, w_pool, pool_scale, q_norm_gain, k_norm_gain,
           attn_sinks, w_out, ffn_norm_gain, w_group_router, w_expert_router, w_gate, w_up, w_down):
    B, S, D = x.shape
    assert D == D_MODEL and S % TQ_IN == 0 and S % TQ_ATT == 0 and (B * S) % TM_MOE == 0
    assert w_in.shape[0] == 1, "single layer"
    T = B * S

    rope = _rope_tables(N_META + S)
    fgain = ffn_norm_gain[0][:, None]
    qgain2 = jnp.tile(q_norm_gain[0] * LOG2_E, 2)[None, :]
    kgain2 = jnp.tile(k_norm_gain[0] * (HEAD_DIM ** 0.5), 2)[None, :]
    pscale = pool_scale[0][None, :]
    w_in_b = (attn_norm_gain[0][:, None] * w_in[0]).astype(BF16)
    w_pool_b = w_pool[0].astype(BF16)
    w_out_b = w_out[0].astype(BF16)
    w_r = (fgain * jnp.concatenate(
        [w_expert_router[0], w_group_router[0],
         jnp.zeros((D, LANES - N_EXPERTS - N_GROUPS), F32)], axis=1)).astype(BF16)
    params = pltpu.CompilerParams(vmem_limit_bytes=VMEM_LIMIT)

    u_meta, kd_meta, v_meta = pl.pallas_call(
        _meta_kernel,
        out_shape=(jax.ShapeDtypeStruct((N_META, POOL_WIDTH), F32),
                   jax.ShapeDtypeStruct((N_META, KV_EXP), BF16),
                   jax.ShapeDtypeStruct((N_META, KV_WIDTH), BF16)),
        compiler_params=params,
        name="meta_proj",
    )(meta_tokens, w_in_b, kgain2, rope[:N_META])
    vt_meta = v_meta.T

    yp, q, kd, vt = pl.pallas_call(
        _in_kernel,
        grid=(B, S // TQ_IN),
        in_specs=[
            pl.BlockSpec((1, TQ_IN, D), lambda b, j: (b, j, 0)),
            _const_spec((N_META, POOL_WIDTH)),
            _const_spec((D, IN_WIDTH)),
            _const_spec((len(POOL_WINDOWS), POOL_GROUP, POOL_GROUP)),
            _const_spec((1, POOL_WIDTH)),
            _const_spec((1, LANES)),
            _const_spec((1, LANES)),
            pl.BlockSpec((TQ_IN, 3 * LANES), lambda b, j: (j, 0)),
        ],
        out_specs=(
            pl.BlockSpec((1, TQ_IN, POOL_WIDTH), lambda b, j: (b, j, 0)),
            pl.BlockSpec((1, TQ_IN, ATTN_WIDTH), lambda b, j: (b, j, 0)),
            pl.BlockSpec((1, TQ_IN, KV_EXP), lambda b, j: (b, j, 0)),
            pl.BlockSpec((1, KV_WIDTH, TQ_IN), lambda b, j: (b, 0, j)),
        ),
        out_shape=(jax.ShapeDtypeStruct((B, S, POOL_WIDTH), BF16),
                   jax.ShapeDtypeStruct((B, S, ATTN_WIDTH), BF16),
                   jax.ShapeDtypeStruct((B, S, KV_EXP), BF16),
                   jax.ShapeDtypeStruct((B, KV_WIDTH, S), BF16)),
        scratch_shapes=[pltpu.VMEM((N_META, POOL_WIDTH), F32),
                        pltpu.VMEM((TQ_IN, IN_WIDTH), F32)],
        compiler_params=pltpu.CompilerParams(
            dimension_semantics=("arbitrary", "arbitrary"), vmem_limit_bytes=VMEM_LIMIT),
        name="in_proj",
    )(x, u_meta, w_in_b, w_pool_b, pscale, qgain2, kgain2, rope[N_META:])

    upper = jnp.asarray(np.triu(np.ones((ROUTE_ROWS, ROUTE_ROWS), np.float32), 1), dtype=BF16)
    per_row = S // TQ_ATT
    n_tiles = B * per_row

    def cur(s):
        t = jnp.minimum(s, n_tiles - 1)
        return t // per_row, t % per_row

    def lag(s):
        t = jnp.maximum(s - 1, 0)
        return t // per_row, t % per_row

    def lag_rows(s):
        return (*lag(s), 0)

    def lag_flat(s):
        return (jnp.maximum(s - 1, 0), 0, 0)

    w_steps = N_EXPERTS * W_SLICES
    assert n_tiles >= w_steps and D % W_SLICES == 0 and D_EXPERT % W_SLICES == 0

    def w_slice(s):
        k = jnp.minimum(s, w_steps - 1)
        return k // W_SLICES, k % W_SLICES, 0

    h2, m, table, rows, cnt, wg, wu, wd = pl.pallas_call(
        functools.partial(_attn_kernel, n_tiles, per_row),
        grid=(n_tiles + 1,),
        in_specs=[
            pl.BlockSpec(memory_space=pltpu.SMEM),
            pl.BlockSpec((1, TQ_ATT, D), lag_rows),
            pl.BlockSpec((1, TQ_ATT, POOL_WIDTH), lag_rows),
            pl.BlockSpec((1, TQ_ATT, ATTN_WIDTH), lambda s: (*cur(s), 0)),
            pl.BlockSpec((1, S, KV_EXP), lambda s: (cur(s)[0], 0, 0)),
            pl.BlockSpec((1, KV_WIDTH, S), lambda s: (cur(s)[0], 0, 0)),
            _const_spec((N_META, KV_EXP)),
            _const_spec((KV_WIDTH, N_META)),
            _const_spec((D, D)),
            _const_spec((D, LANES)),
            _const_spec((ROUTE_ROWS, ROUTE_ROWS)),
            pl.BlockSpec((D // W_SLICES, 1), lambda s: (w_slice(s)[1], 0)),
            pl.BlockSpec((1, D // W_SLICES, D_EXPERT), w_slice),
            pl.BlockSpec((1, D // W_SLICES, D_EXPERT), w_slice),
            pl.BlockSpec((1, D_EXPERT // W_SLICES, D), w_slice),
        ],
        out_specs=(
            pl.BlockSpec((1, TQ_ATT, D), lag_rows),
            pl.BlockSpec((1, TQ_ATT, D), lag_rows),
            pl.BlockSpec((1, TQ_ATT, LANES), lag_rows),
            pl.BlockSpec((TQ_ATT // ROUTE_ROWS, SUBLANES, ROUTE_ROWS), lag_flat),
            pl.BlockSpec((TQ_ATT // ROUTE_ROWS, SUBLANES, LANES), lag_flat),
            pl.BlockSpec((1, D // W_SLICES, D_EXPERT), w_slice),
            pl.BlockSpec((1, D // W_SLICES, D_EXPERT), w_slice),
            pl.BlockSpec((1, D_EXPERT // W_SLICES, D), w_slice),
        ),
        out_shape=(jax.ShapeDtypeStruct((B, S, D), F32),
                   jax.ShapeDtypeStruct((B, S, D), BF16),
                   jax.ShapeDtypeStruct((B, S, LANES), BF16),
                   jax.ShapeDtypeStruct((T // ROUTE_ROWS, SUBLANES, ROUTE_ROWS), F32),
                   jax.ShapeDtypeStruct((T // ROUTE_ROWS, SUBLANES, LANES), jnp.int32),
                   jax.ShapeDtypeStruct((N_EXPERTS, D, D_EXPERT), BF16),
                   jax.ShapeDtypeStruct((N_EXPERTS, D, D_EXPERT), BF16),
                   jax.ShapeDtypeStruct((N_EXPERTS, D_EXPERT, D), BF16)),
        scratch_shapes=[
            pltpu.VMEM((2, TQ_ATT, ATTN_WIDTH), BF16),
            pltpu.VMEM((TQ_ATT // WINDOW, N_KV_HEADS, KEY_ROWS, LANES), BF16),
            pltpu.VMEM((TQ_ATT // WINDOW, N_KV_HEADS, LANES, KEY_PAD), BF16),
            pltpu.VMEM((N_ITEMS, KEY_ROWS, 2 * LANES), F32),
            pltpu.VMEM((N_ITEMS, KEY_PAD, 2 * LANES), BF16),
            pltpu.VMEM((N_ITEMS, LANES, 2 * LANES), F32),
            pltpu.VMEM((TQ_ATT // ROUTE_ROWS, ROUTE_ROWS, LANES), F32),
        ],
        compiler_params=pltpu.CompilerParams(
            dimension_semantics=("arbitrary",), vmem_limit_bytes=VMEM_LIMIT),
        name="attn_out",
    )(attn_sinks[0] * LOG2_E, x, yp, q, kd, vt, kd_meta, vt_meta, w_out_b, w_r, upper,
      fgain, w_gate[0], w_up[0], w_down[0])

    cnt_flat = cnt[:, :N_GROUPS, 0].reshape(-1)
    out = pl.pallas_call(
        _moe_kernel,
        grid_spec=pltpu.PrefetchScalarGridSpec(
            num_scalar_prefetch=1,
            grid=(T // TM_MOE,),
            in_specs=[
                pl.BlockSpec((TM_MOE, D), lambda i, c: (i, 0)),
                pl.BlockSpec((TM_MOE, LANES), lambda i, c: (i, 0)),
                pl.BlockSpec((TM_MOE // ROUTE_ROWS, SUBLANES, ROUTE_ROWS), lambda i, c: (i, 0, 0)),
                pl.BlockSpec((TM_MOE, D), lambda i, c: (i, 0)),
                _resident_spec((N_EXPERTS, D, D_EXPERT)),
                _resident_spec((N_EXPERTS, D, D_EXPERT)),
                _resident_spec((N_EXPERTS * D_EXPERT, D)),
            ],
            out_specs=pl.BlockSpec((TM_MOE, D), lambda i, c: (i, 0)),
            scratch_shapes=[pltpu.VMEM((YS_ROWS, D), BF16),
                            pltpu.VMEM((2, max(SUB, *PASS_SIZES), D_GROUP), BF16),
                            pltpu.SMEM((3 * N_GROUPS,), jnp.int32)],
        ),
        out_shape=jax.ShapeDtypeStruct((T, D), F32),
        compiler_params=pltpu.CompilerParams(
            dimension_semantics=("arbitrary",), vmem_limit_bytes=VMEM_LIMIT),
        name="moe",
    )(cnt_flat, m.reshape(T, D), table.reshape(T, LANES), rows, h2.reshape(T, D), wg, wu,
      wd.reshape(N_EXPERTS * D_EXPERT, D))
    return out.reshape(B, S, D)
```

```python
import functools

import numpy as np
import jax
import jax.numpy as jnp
from jax import lax
from jax.experimental import pallas as pl
from jax.experimental.pallas import tpu as pltpu

D_MODEL = 1024
N_META = 16
POOL_WIDTH = 512
POOL_WINDOWS = (2, 4, 8, 16)
POOL_GROUP = 128
HEAD_DIM = 64
N_HEADS = 8
N_KV_HEADS = 2
ATTN_WIDTH = N_HEADS * HEAD_DIM
KV_WIDTH = N_KV_HEADS * HEAD_DIM
WINDOW = 128
ROT_DIM = HEAD_DIM // 4
ROPE_THETA = 500000.0
IN_WIDTH = POOL_WIDTH + ATTN_WIDTH + 2 * KV_WIDTH
N_GROUPS = 4
EXPERTS_PER_GROUP = 4
N_EXPERTS = 16
D_EXPERT = 256
EPS = 1e-6
NEG_INF = -1e30
LOG2_E = 1.4426950408889634

LANES = 128
SUBLANES = 8
KV_EXP = 4 * LANES
TQ_IN = 2048
RB_IN = 256
FIN_BLOCKS = 3
TQ_ATT = 512
KEY_ROWS = 2 * (2 * WINDOW + N_META)
KEY_PAD = -(-KEY_ROWS // LANES) * LANES
OUT_BLOCKS = 2
ROUTE_ROWS = OUT_BLOCKS * WINDOW
W_SLICES = 4
N_ITEMS = (TQ_ATT // WINDOW) * N_KV_HEADS
TM_MOE = 512
SUB = 144
PASS_SIZES = (128, 144, 160)
BF16_ROWS = 16
MXU_DEPTH = 256
GID_LANE = N_EXPERTS
RANK_LANE = 24
LO_SHIFT = 32
D_GROUP = EXPERTS_PER_GROUP * D_EXPERT
VMEM_LIMIT = 56 * 1024 * 1024

BF16 = jnp.bfloat16
F32 = jnp.float32


def _rope_tables(n_pos):
    half = ROT_DIM // 2
    inv_freq = 1.0 / (ROPE_THETA ** (np.arange(half, dtype=np.float64) / half))
    ang = np.arange(n_pos, dtype=np.float64)[:, None] * inv_freq[None, :]
    cos, sin = np.cos(ang), np.sin(ang)
    c = np.ones((n_pos, HEAD_DIM)); c[:, :half] = cos; c[:, half:ROT_DIM] = cos
    sa = np.zeros((n_pos, HEAD_DIM)); sa[:, :half] = -sin
    sb = np.zeros((n_pos, HEAD_DIM)); sb[:, half:ROT_DIM] = sin
    tab = np.concatenate([np.tile(c, (1, 2)), np.tile(sa, (1, 2)), np.tile(sb, (1, 2))], axis=1)
    return jnp.asarray(tab, dtype=F32)


def _rms_unit(x):
    ms = jnp.mean(x * x, axis=-1, keepdims=True)
    return x * lax.rsqrt(ms + EPS)


def _head_norm_rope_many(xs, gains, ropes):
    los = [lax.broadcasted_iota(jnp.int32, x.shape, 1) < HEAD_DIM for x in xs]
    sums = []
    for x, lo in zip(xs, los):
        sq = x * x
        sums.append((jnp.sum(jnp.where(lo, sq, 0.0), axis=-1, keepdims=True),
                     jnp.sum(jnp.where(lo, 0.0, sq), axis=-1, keepdims=True)))
    ys = [x * lax.rsqrt(jnp.where(lo, s_lo, s_hi) + HEAD_DIM * EPS) * g
          for x, g, lo, (s_lo, s_hi) in zip(xs, gains, los, sums)]
    half = ROT_DIM // 2
    rolled = [(pltpu.roll(y, LANES - half, 1), pltpu.roll(y, half, 1)) for y in ys]
    return [y * rope[:, 0:LANES] + ra * rope[:, LANES:2 * LANES] + rb * rope[:, 2 * LANES:3 * LANES]
            for y, (ra, rb), rope in zip(ys, rolled, ropes)]


def _head_norm_rope(xc, gain2, rope):
    return _head_norm_rope_many([xc], [gain2], [rope])[0]


def _expand_kv(t):
    lo = lax.broadcasted_iota(jnp.int32, t.shape, 1) < HEAD_DIM
    sw = pltpu.roll(t, HEAD_DIM, 1)
    z = jnp.zeros_like(t)
    return jnp.concatenate([jnp.where(lo, t, z), jnp.where(lo, z, sw),
                            jnp.where(lo, sw, z), jnp.where(lo, z, t)], axis=1)


def _project(x, w_in):
    return jnp.dot(_rms_unit(x).astype(BF16), w_in, preferred_element_type=F32)


def _meta_kernel(meta_ref, win_ref, kgain_ref, rope_ref, u_ref, kd_ref, v_ref):
    proj = _project(meta_ref[...], win_ref[...])
    u_ref[...] = proj[:, :POOL_WIDTH]
    k = proj[:, POOL_WIDTH + ATTN_WIDTH:POOL_WIDTH + ATTN_WIDTH + KV_WIDTH]
    v = proj[:, POOL_WIDTH + ATTN_WIDTH + KV_WIDTH:]
    k = _head_norm_rope(k, kgain_ref[...], rope_ref[...])
    kd_ref[...] = _expand_kv(k).astype(BF16)
    v_ref[...] = v.astype(BF16)


def _in_kernel(x_ref, umeta_ref, win_ref, wpool_ref, pscale_ref, qgain_ref, kgain_ref,
               rope_ref, yp_ref, q_ref, kd_ref, vt_ref, carry_ref, proj_ref):
    j = pl.program_id(1)

    @pl.when(j == 0)
    def _():
        carry_ref[...] = umeta_ref[...]

    sizes = [RB_IN] * (TQ_IN // RB_IN - 1) + [RB_IN // 2] * 2
    blocks = [slice(sum(sizes[:i]), sum(sizes[:i + 1])) for i in range(len(sizes))]
    n_sub = len(blocks)

    def normalise(i):
        return _rms_unit(x_ref[0, blocks[i], :]).astype(BF16)

    def project(i, a):
        proj_ref[blocks[i], :] = jnp.dot(a, win_ref[...], preferred_element_type=F32)

    def pool(i):
        rows = blocks[i]
        u = proj_ref[rows, :POOL_WIDTH]
        prev = carry_ref[...] if i == 0 else proj_ref[rows.start - N_META:rows.start, :POOL_WIDTH]
        acc = jnp.concatenate([prev, u], axis=0)
        for gi, w in enumerate(POOL_WINDOWS):
            lo = gi * POOL_GROUP
            acc = acc[:, POOL_GROUP * (1 if gi else 0):]
            acc = acc + pltpu.roll(acc, w // 2, 0)
            mixed = acc[N_META:, :POOL_GROUP] * (1.0 / w) - u[:, lo:lo + POOL_GROUP]
            y = jnp.dot(mixed.astype(BF16), wpool_ref[gi], preferred_element_type=F32)
            yp_ref[0, rows, lo:lo + POOL_GROUP] = (y * pscale_ref[:, lo:lo + POOL_GROUP]).astype(BF16)

    n_qc = ATTN_WIDTH // LANES

    def finish(ids):
        for i in ids:
            pool(i)
        xs, gains, ropes = [], [], []
        for i in ids:
            rows = blocks[i]
            xs += [proj_ref[rows, POOL_WIDTH + c * LANES:POOL_WIDTH + (c + 1) * LANES] for c in range(n_qc + 1)]
            gains += [qgain_ref[...]] * n_qc + [kgain_ref[...]]
            ropes += [rope_ref[rows, :]] * (n_qc + 1)
        outs = _head_norm_rope_many(xs, gains, ropes)
        for k, i in enumerate(ids):
            rows = blocks[i]
            mine = outs[k * (n_qc + 1):(k + 1) * (n_qc + 1)]
            for c in range(n_qc):
                q_ref[0, rows, c * LANES:(c + 1) * LANES] = mine[c].astype(BF16)
            kd_ref[0, rows, :] = _expand_kv(mine[n_qc]).astype(BF16)
            v = proj_ref[rows, POOL_WIDTH + ATTN_WIDTH + KV_WIDTH:]
            vt_ref[0, :, rows] = jnp.transpose(v).astype(BF16)

    a_next = normalise(0)
    pending = []
    for i in range(n_sub):
        project(i, a_next)
        if i + 1 < n_sub:
            a_next = normalise(i + 1)
        pending.append(i)
        if len(pending) > FIN_BLOCKS:
            finish(pending[:FIN_BLOCKS])
            pending = pending[FIN_BLOCKS:]
    finish(pending)
    carry_ref[...] = proj_ref[TQ_IN - N_META:, :POOL_WIDTH]


def _route(logits, upper):
    n = logits.shape[0]
    lt = jnp.transpose(logits)
    sub = SUBLANES
    row8 = lax.broadcasted_iota(jnp.int32, (sub, n), 0).astype(F32)
    row16 = lax.broadcasted_iota(jnp.int32, (N_EXPERTS, n), 0).astype(F32)
    g_ok = row8 < N_GROUPS
    gl = jnp.where(g_ok, lt[N_EXPERTS:N_EXPERTS + sub, :], NEG_INF)
    gmax = jnp.max(gl, axis=0, keepdims=True)
    gsum = jnp.sum(jnp.where(g_ok, jnp.exp(gl - gmax), 0.0), axis=0, keepdims=True)
    g_prob = 1.0 / gsum
    g_idx = jnp.min(jnp.where(gl == gmax, row8, float(sub)), axis=0, keepdims=True)
    e_lo = g_idx * EXPERTS_PER_GROUP
    emask = (row16 >= e_lo) & (row16 < e_lo + EXPERTS_PER_GROUP)
    el = jnp.where(emask, lt[:N_EXPERTS, :], NEG_INF)
    big = float(N_EXPERTS)
    e1 = jnp.max(el, axis=0, keepdims=True)
    i1 = jnp.min(jnp.where(el == e1, row16, big), axis=0, keepdims=True)
    el2 = jnp.where(row16 == i1, NEG_INF, el)
    e2 = jnp.max(el2, axis=0, keepdims=True)
    i2 = jnp.min(jnp.where(el2 == e2, row16, big), axis=0, keepdims=True)
    t = jnp.exp(e2 - e1)
    w1 = 1.0 / (1.0 + t)
    w2 = t * w1
    gates_t = jnp.where(row16 == i1, w1 * g_prob, 0.0) + jnp.where(row16 == i2, w2 * g_prob, 0.0)
    onehot = jnp.where(row8 == g_idx, 1.0, 0.0)
    earlier = jnp.dot(onehot.astype(BF16), upper, preferred_element_type=F32)
    rank = jnp.sum(onehot * earlier, axis=0, keepdims=True)
    hi = gates_t.astype(BF16).astype(F32)
    lo = (gates_t - hi).astype(BF16).astype(F32)
    gid8 = jnp.broadcast_to(g_idx, (sub, n))
    rank8 = jnp.broadcast_to(rank, (sub, n))
    assert GID_LANE == N_EXPERTS and RANK_LANE == GID_LANE + sub and LO_SHIFT == RANK_LANE + sub
    full_t = jnp.concatenate(
        [hi, gid8, rank8, lo, jnp.zeros((LANES - LO_SHIFT - N_EXPERTS, n), F32)], axis=0)
    rows = jnp.concatenate([g_idx, rank, jnp.zeros((sub - 2, n), F32)], axis=0)
    counts = jnp.sum(onehot, axis=1, keepdims=True)
    return (jnp.transpose(full_t).astype(BF16), rows,
            jnp.broadcast_to(counts, (sub, LANES)).astype(jnp.int32))


def _fold_rows(x, pair, final):
    rows = x.shape[0]
    while rows > SUBLANES:
        rows //= 2
        x = pair(x[:rows], x[rows:])
    return final(x, axis=0, keepdims=True)


def _attn_kernel(n_tiles, tiles_per_row, sink_ref, x_ref, yp_ref, q_ref, kd_ref, vt_ref, kmeta_ref, vtmeta_ref, wout_ref,
                 wr_ref, upper_ref, fg_ref, wg32_ref, wu32_ref, wd32_ref,
                 h2_ref, m_ref, table_ref, rows_ref, cnt_ref, wg_ref, wu_ref, wd_ref,
                 yattn_ref, kcat_ref, vbd_ref, s_ref, p_ref, inv_ref, logit_ref):
    step = pl.program_id(0)
    tile = jnp.minimum(step, n_tiles - 1)
    j = lax.rem(tile, tiles_per_row)
    slot = lax.rem(step, 2)
    nt = (((1,), (1,)), ((), ()))
    n_r = TQ_ATT // WINDOW
    band = 2 * WINDOW
    m0 = 2 * band

    @pl.when(step == 0)
    def _():
        vbd_ref[...] = jnp.zeros(vbd_ref.shape, BF16)
        p_ref[...] = jnp.zeros(p_ref.shape, BF16)
        yattn_ref[...] = jnp.zeros(yattn_ref.shape, BF16)

    wg_ref[0] = (wg32_ref[0] * fg_ref[...]).astype(BF16)
    wu_ref[0] = (wu32_ref[0] * fg_ref[...]).astype(BF16)
    wd_ref[0] = wd32_ref[0].astype(BF16)

    def window_start(r):
        return pl.multiple_of(jnp.maximum(j * TQ_ATT + (r - 1) * WINDOW, 0), WINDOW)

    def stage_scores(r):
        start = window_start(r)
        for g in range(N_KV_HEADS):
            vb = vt_ref[0, g * HEAD_DIM:(g + 1) * HEAD_DIM, pl.ds(start, band)]
            vm = vtmeta_ref[g * HEAD_DIM:(g + 1) * HEAD_DIM, :]
            for hh in range(2):
                col = (2 * g + hh) * LANES
                kcat_ref[r, g, hh * band:(hh + 1) * band, :] = kd_ref[0, pl.ds(start, band), col:col + LANES]
                kcat_ref[r, g, m0 + hh * N_META:m0 + (hh + 1) * N_META, :] = kmeta_ref[:, col:col + LANES]
                rows = slice(hh * HEAD_DIM, (hh + 1) * HEAD_DIM)
                vbd_ref[r, g, rows, hh * band:(hh + 1) * band] = vb
                vbd_ref[r, g, rows, m0 + hh * N_META:m0 + (hh + 1) * N_META] = vm
            rq = slice(r * WINDOW, (r + 1) * WINDOW)
            qq = jnp.concatenate([q_ref[0, rq, (2 * g + ch) * LANES:(2 * g + ch + 1) * LANES]
                                  for ch in range(2)], axis=0)
            s_ref[r * N_KV_HEADS + g] = lax.dot_general(kcat_ref[r, g], qq, nt, preferred_element_type=F32)

    def stage_softmax(r):
        kpos = window_start(r) + lax.broadcasted_iota(jnp.int32, (band, WINDOW), 0)
        qpos = j * TQ_ATT + r * WINDOW + lax.broadcasted_iota(jnp.int32, (band, WINDOW), 1)
        d = qpos - kpos
        bias = jnp.where((d >= 0) & (d < WINDOW), 0.0, NEG_INF)
        bias = jnp.concatenate([bias, bias], axis=1)
        left = lax.broadcasted_iota(jnp.int32, (1, 2 * WINDOW), 1) < WINDOW
        for g in range(N_KV_HEADS):
            it = r * N_KV_HEADS + g
            for hh in range(2):
                sink = jnp.where(left, sink_ref[4 * g + hh], sink_ref[4 * g + 2 + hh])
                s = s_ref[it, hh * band:(hh + 1) * band, :] + bias
                sm = s_ref[it, m0 + hh * N_META:m0 + (hh + 1) * N_META, :]
                mx = jnp.maximum(_fold_rows(s, jnp.maximum, jnp.max), jnp.max(sm, axis=0, keepdims=True))
                mx = jnp.maximum(mx, sink)
                p = jnp.exp2(s - mx)
                pm = jnp.exp2(sm - mx)
                den = (_fold_rows(p, jnp.add, jnp.sum) + jnp.sum(pm, axis=0, keepdims=True)
                       + jnp.exp2(sink - mx))
                p_ref[it, hh * band:(hh + 1) * band, :] = p.astype(BF16)
                p_ref[it, m0 + hh * N_META:m0 + (hh + 1) * N_META, :] = pm.astype(BF16)
                inv_ref[it, hh * HEAD_DIM:(hh + 1) * HEAD_DIM, :] = jnp.broadcast_to(
                    1.0 / den, (HEAD_DIM, 2 * WINDOW))

    def stage_values(r):
        for g in range(N_KV_HEADS):
            it = r * N_KV_HEADS + g
            o_t = jnp.dot(vbd_ref[r, g], p_ref[it], preferred_element_type=F32) * inv_ref[it]
            for ch in range(2):
                c = 2 * g + ch
                yattn_ref[slot, r * WINDOW:(r + 1) * WINDOW, c * LANES:(c + 1) * LANES] = (
                    jnp.transpose(o_t[:, ch * WINDOW:(ch + 1) * WINDOW]).astype(BF16))

    def stage_out():
        mix = (jnp.dot(yp_ref[0], wout_ref[:POOL_WIDTH, :], preferred_element_type=F32)
               + jnp.dot(yattn_ref[1 - slot], wout_ref[POOL_WIDTH:, :], preferred_element_type=F32))
        h2 = x_ref[0] + mix
        h2_ref[0] = h2
        m = _rms_unit(h2).astype(BF16)
        m_ref[0] = m
        logits = jnp.dot(m, wr_ref[...], preferred_element_type=F32)
        for p in range(TQ_ATT // ROUTE_ROWS):
            logit_ref[p] = logits[p * ROUTE_ROWS:(p + 1) * ROUTE_ROWS, :]

    def stage_route(p):
        table, rows, counts = _route(logit_ref[p], upper_ref[...])
        table_ref[0, p * ROUTE_ROWS:(p + 1) * ROUTE_ROWS, :] = table
        rows_ref[p] = rows
        cnt_ref[p] = counts

    n_p = TQ_ATT // ROUTE_ROWS
    for t in range(max(n_r + 2, n_p + 1)):
        for k, stage in enumerate((stage_scores, stage_softmax, stage_values)):
            if 0 <= t - k < n_r:
                stage(t - k)
        if t == 0:
            stage_out()
        if 1 <= t <= n_p:
            stage_route(t - 1)


def _pair_plan(na, nb):
    top = PASS_SIZES[-1]
    if isinstance(na, int) and isinstance(nb, int):
        if 1 <= na <= top and 1 <= nb <= top:
            size = next(s for s in PASS_SIZES if max(na, nb) <= s)
            return 1, size, size, size
        return 0, SUB, -(-na // SUB) * SUB, -(-nb // SUB) * SUB
    fast = (na >= 1) & (na <= top) & (nb >= 1) & (nb <= top)
    size = jnp.int32(top)
    for s in reversed(PASS_SIZES[:-1]):
        size = jnp.where(jnp.maximum(na, nb) <= s, s, size)
    slow = [lax.div(n + (SUB - 1), SUB) * SUB for n in (na, nb)]
    return (fast.astype(jnp.int32), size,
            jnp.where(fast, size, slow[0]), jnp.where(fast, size, slow[1]))


def _max_sorted_rows(n_tokens):
    pair = [max(sum(_pair_plan(na, t - na)[2:]) for na in range(t + 1)) for t in range(n_tokens + 1)]
    return max(pair[t] + pair[n_tokens - t] for t in range(n_tokens + 1))


def _moe_kernel(cnt_ref, m_ref, table_ref, rows_ref, h2_ref, wg_ref, wu_ref, wd_ref, out_ref,
                ys_ref, hid_ref, plan_ref):
    i = pl.program_id(0)
    per = TM_MOE // ROUTE_ROWS
    counts = [[cnt_ref[(i * per + h) * N_GROUPS + g] for g in range(N_GROUPS)] for h in range(per)]
    n_tok = [sum(counts[h][g] for h in range(per)) for g in range(N_GROUPS)]
    base = []
    total_rows = 0
    for p in range(N_GROUPS // 2):
        fast, size, rows_a, rows_b = _pair_plan(n_tok[2 * p], n_tok[2 * p + 1])
        plan_ref[2 * N_GROUPS + p] = fast
        plan_ref[2 * N_GROUPS + N_GROUPS // 2 + p] = size
        for g, rows in ((2 * p, rows_a), (2 * p + 1, rows_b)):
            base.append(total_rows)
            plan_ref[g] = n_tok[g]
            plan_ref[N_GROUPS + g] = total_rows + jnp.int32(0)
            total_rows = total_rows + rows
    first = [[base[g] + sum(counts[hh][g] for hh in range(h)) for g in range(N_GROUPS)] for h in range(per)]

    @pl.when(i == 0)
    def _():
        ys_ref[...] = jnp.zeros(ys_ref.shape, BF16)

    def pick(gid, offsets):
        out = _as_f32(offsets[N_GROUPS - 1])
        for g in range(N_GROUPS - 2, -1, -1):
            out = jnp.where(gid == g, _as_f32(offsets[g]), out)
        return out

    table = table_ref[...]
    table_f = table.astype(F32)
    gid_col = table_f[:, GID_LANE:GID_LANE + 1]
    row_id = lax.broadcasted_iota(jnp.int32, (TM_MOE, 1), 0)
    off_col = pick(gid_col, first[per - 1])
    for h in range(per - 2, -1, -1):
        off_col = jnp.where(row_id < (h + 1) * ROUTE_ROWS, pick(gid_col, first[h]), off_col)
    pos_col = table_f[:, RANK_LANE:RANK_LANE + 1] + off_col
    pos_row = jnp.concatenate(
        [rows_ref[h, 1:2, :] + pick(rows_ref[h, 0:1, :], first[h]) for h in range(per)], axis=1)

    def gather(g, r0, size):
        rows = (r0 + lax.broadcasted_iota(jnp.int32, (size, 1), 0)).astype(F32)
        perm = jnp.where(pos_row == rows, 1.0, 0.0).astype(BF16)
        xs = jnp.dot(perm, m_ref[...], preferred_element_type=F32).astype(BF16)
        gs = jnp.dot(perm, table, preferred_element_type=F32)
        return xs, gs

    def experts_up(g, xs, gs, size, slot):
        lane = lax.broadcasted_iota(jnp.int32, gs.shape, 1)
        for jj in range(EXPERTS_PER_GROUP):
            e = g * EXPERTS_PER_GROUP + jj
            gt = jnp.dot(xs, wg_ref[e], preferred_element_type=F32)
            up = jnp.dot(xs, wu_ref[e], preferred_element_type=F32)
            gate = jnp.sum(jnp.where((lane == e) | (lane == e + LO_SHIFT), gs, 0.0),
                           axis=-1, keepdims=True)
            hid = gt * (1.0 / (1.0 + jnp.exp(-gt))) * up * gate
            hid_ref[slot, :size, jj * D_EXPERT:(jj + 1) * D_EXPERT] = hid.astype(BF16)

    def experts_down(g, r0, size, slot):
        w_down = wd_ref[pl.ds(pl.multiple_of(g * D_GROUP, D_GROUP), D_GROUP), :]
        y = jnp.dot(hid_ref[slot, :size, :], w_down, preferred_element_type=F32)
        ys_ref[pl.ds(r0, size), :] = y.astype(BF16)

    def run_passes(jobs, size):
        got = [gather(g, r0, size) for g, r0 in jobs]
        for slot, ((g, _), (xs, gs)) in enumerate(zip(jobs, got)):
            experts_up(g, xs, gs, size, slot)
        for slot, (g, r0) in enumerate(jobs):
            experts_down(g, r0, size, slot)

    def run_pair(p, carry):
        groups = (2 * p, 2 * p + 1)
        firsts = [pl.multiple_of(plan_ref[N_GROUPS + g], BF16_ROWS) for g in groups]
        fast = plan_ref[2 * N_GROUPS + p]
        size_p = plan_ref[2 * N_GROUPS + N_GROUPS // 2 + p]
        for size in PASS_SIZES:
            @pl.when((fast == 1) & (size_p == size))
            def _(size=size):
                run_passes(list(zip(groups, firsts)), size)

        @pl.when(fast == 0)
        def _():
            for g, r0 in zip(groups, firsts):
                def body(c, inner, g=g, r0=r0):
                    run_passes([(g, pl.multiple_of(r0 + c * SUB, BF16_ROWS))], SUB)
                    return inner
                lax.fori_loop(0, lax.div(plan_ref[g] + (SUB - 1), SUB), body, 0)
        return carry

    lax.fori_loop(0, N_GROUPS // 2, run_pair, 0)

    def scatter(n_rows):
        cols = lax.broadcasted_iota(jnp.int32, (1, n_rows), 1).astype(F32)
        perm_t = jnp.where(pos_col == cols, 1.0, 0.0).astype(BF16)
        out_ref[...] = h2_ref[...] + jnp.dot(perm_t, ys_ref[:n_rows, :], preferred_element_type=F32)

    short = total_rows <= YS_SHORT

    @pl.when(short)
    def _():
        scatter(YS_SHORT)

    @pl.when(jnp.logical_not(short))
    def _():
        scatter(YS_ROWS)


YS_ROWS = -(-_max_sorted_rows(TM_MOE) // MXU_DEPTH) * MXU_DEPTH
YS_SHORT = YS_ROWS - MXU_DEPTH


def _as_f32(v):
    return float(v) if isinstance(v, int) else v.astype(F32)


def _const_spec(shape):
    n = len(shape)
    return pl.BlockSpec(shape, lambda *_: (0,) * n)


def _resident_spec(shape):
    n = len(shape)
    return pl.BlockSpec(shape, lambda *_: (0,) * n, pipeline_mode=pl.Buffered(1))


def kernel(x, meta_tokens, attn_norm_gain, w_in, w_pool, pool_scale, q_norm_gain, k_norm_gain,
           attn_sinks, w_out, ffn_norm_gain, w_group_router, w_expert_router, w_gate, w_up, w_down):
    B, S, D = x.shape
    assert D == D_MODEL and S % TQ_IN == 0 and S % TQ_ATT == 0 and (B * S) % TM_MOE == 0
    assert w_in.shape[0] == 1, "single layer"
    T = B * S

    rope = _rope_tables(N_META + S)
    fgain = ffn_norm_gain[0][:, None]
    qgain2 = jnp.tile(q_norm_gain[0] * LOG2_E, 2)[None, :]
    kgain2 = jnp.tile(k_norm_gain[0] * (HEAD_DIM ** 0.5), 2)[None, :]
    pscale = pool_scale[0][None, :]
    w_in_b = (attn_norm_gain[0][:, None] * w_in[0]).astype(BF16)
    w_pool_b = w_pool[0].astype(BF16)
    w_out_b = w_out[0].astype(BF16)
    w_r = (fgain * jnp.concatenate(
        [w_expert_router[0], w_group_router[0],
         jnp.zeros((D, LANES - N_EXPERTS - N_GROUPS), F32)], axis=1)).astype(BF16)
    params = pltpu.CompilerParams(vmem_limit_bytes=VMEM_LIMIT)

    u_meta, kd_meta, v_meta = pl.pallas_call(
        _meta_kernel,
        out_shape=(jax.ShapeDtypeStruct((N_META, POOL_WIDTH), F32),
                   jax.ShapeDtypeStruct((N_META, KV_EXP), BF16),
                   jax.ShapeDtypeStruct((N_META, KV_WIDTH), BF16)),
        compiler_params=params,
        name="meta_proj",
    )(meta_tokens, w_in_b, kgain2, rope[:N_META])
    vt_meta = v_meta.T

    yp, q, kd, vt = pl.pallas_call(
        _in_kernel,
        grid=(B, S // TQ_IN),
        in_specs=[
            pl.BlockSpec((1, TQ_IN, D), lambda b, j: (b, j, 0)),
            _const_spec((N_META, POOL_WIDTH)),
            _const_spec((D, IN_WIDTH)),
            _const_spec((len(POOL_WINDOWS), POOL_GROUP, POOL_GROUP)),
            _const_spec((1, POOL_WIDTH)),
            _const_spec((1, LANES)),
            _const_spec((1, LANES)),
            pl.BlockSpec((TQ_IN, 3 * LANES), lambda b, j: (j, 0)),
        ],
        out_specs=(
            pl.BlockSpec((1, TQ_IN, POOL_WIDTH), lambda b, j: (b, j, 0)),
            pl.BlockSpec((1, TQ_IN, ATTN_WIDTH), lambda b, j: (b, j, 0)),
            pl.BlockSpec((1, TQ_IN, KV_EXP), lambda b, j: (b, j, 0)),
            pl.BlockSpec((1, KV_WIDTH, TQ_IN), lambda b, j: (b, 0, j)),
        ),
        out_shape=(jax.ShapeDtypeStruct((B, S, POOL_WIDTH), BF16),
                   jax.ShapeDtypeStruct((B, S, ATTN_WIDTH), BF16),
                   jax.ShapeDtypeStruct((B, S, KV_EXP), BF16),
                   jax.ShapeDtypeStruct((B, KV_WIDTH, S), BF16)),
        scratch_shapes=[pltpu.VMEM((N_META, POOL_WIDTH), F32),
                        pltpu.VMEM((TQ_IN, IN_WIDTH), F32)],
        compiler_params=pltpu.CompilerParams(
            dimension_semantics=("arbitrary", "arbitrary"), vmem_limit_bytes=VMEM_LIMIT),
        name="in_proj",
    )(x, u_meta, w_in_b, w_pool_b, pscale, qgain2, kgain2, rope[N_META:])

    upper = jnp.asarray(np.triu(np.ones((ROUTE_ROWS, ROUTE_ROWS), np.float32), 1), dtype=BF16)
    per_row = S // TQ_ATT
    n_tiles = B * per_row

    def cur(s):
        t = jnp.minimum(s, n_tiles - 1)
        return t // per_row, t % per_row

    def lag(s):
        t = jnp.maximum(s - 1, 0)
        return t // per_row, t % per_row

    def lag_rows(s):
        return (*lag(s), 0)

    def lag_flat(s):
        return (jnp.maximum(s - 1, 0), 0, 0)

    w_steps = N_EXPERTS * W_SLICES
    assert n_tiles >= w_steps and D % W_SLICES == 0 and D_EXPERT % W_SLICES == 0

    def w_slice(s):
        k = jnp.minimum(s, w_steps - 1)
        return k // W_SLICES, k % W_SLICES, 0

    h2, m, table, rows, cnt, wg, wu, wd = pl.pallas_call(
        functools.partial(_attn_kernel, n_tiles, per_row),
        grid=(n_tiles + 1,),
        in_specs=[
            pl.BlockSpec(memory_space=pltpu.SMEM),
            pl.BlockSpec((1, TQ_ATT, D), lag_rows),
            pl.BlockSpec((1, TQ_ATT, POOL_WIDTH), lag_rows),
            pl.BlockSpec((1, TQ_ATT, ATTN_WIDTH), lambda s: (*cur(s), 0)),
            pl.BlockSpec((1, S, KV_EXP), lambda s: (cur(s)[0], 0, 0)),
            pl.BlockSpec((1, KV_WIDTH, S), lambda s: (cur(s)[0], 0, 0)),
            _const_spec((N_META, KV_EXP)),
            _const_spec((KV_WIDTH, N_META)),
            _const_spec((D, D)),
            _const_spec((D, LANES)),
            _const_spec((ROUTE_ROWS, ROUTE_ROWS)),
            pl.BlockSpec((D // W_SLICES, 1), lambda s: (w_slice(s)[1], 0)),
            pl.BlockSpec((1, D // W_SLICES, D_EXPERT), w_slice),
            pl.BlockSpec((1, D // W_SLICES, D_EXPERT), w_slice),
            pl.BlockSpec((1, D_EXPERT // W_SLICES, D), w_slice),
        ],
        out_specs=(
            pl.BlockSpec((1, TQ_ATT, D), lag_rows),
            pl.BlockSpec((1, TQ_ATT, D), lag_rows),
            pl.BlockSpec((1, TQ_ATT, LANES), lag_rows),
            pl.BlockSpec((TQ_ATT // ROUTE_ROWS, SUBLANES, ROUTE_ROWS), lag_flat),
            pl.BlockSpec((TQ_ATT // ROUTE_ROWS, SUBLANES, LANES), lag_flat),
            pl.BlockSpec((1, D // W_SLICES, D_EXPERT), w_slice),
            pl.BlockSpec((1, D // W_SLICES, D_EXPERT), w_slice),
            pl.BlockSpec((1, D_EXPERT // W_SLICES, D), w_slice),
        ),
        out_shape=(jax.ShapeDtypeStruct((B, S, D), F32),
                   jax.ShapeDtypeStruct((B, S, D), BF16),
                   jax.ShapeDtypeStruct((B, S, LANES), BF16),
                   jax.ShapeDtypeStruct((T // ROUTE_ROWS, SUBLANES, ROUTE_ROWS), F32),
                   jax.ShapeDtypeStruct((T // ROUTE_ROWS, SUBLANES, LANES), jnp.int32),
                   jax.ShapeDtypeStruct((N_EXPERTS, D, D_EXPERT), BF16),
                   jax.ShapeDtypeStruct((N_EXPERTS, D, D_EXPERT), BF16),
                   jax.ShapeDtypeStruct((N_EXPERTS, D_EXPERT, D), BF16)),
        scratch_shapes=[
            pltpu.VMEM((2, TQ_ATT, ATTN_WIDTH), BF16),
            pltpu.VMEM((TQ_ATT // WINDOW, N_KV_HEADS, KEY_ROWS, LANES), BF16),
            pltpu.VMEM((TQ_ATT // WINDOW, N_KV_HEADS, LANES, KEY_PAD), BF16),
            pltpu.VMEM((N_ITEMS, KEY_ROWS, 2 * LANES), F32),
            pltpu.VMEM((N_ITEMS, KEY_PAD, 2 * LANES), BF16),
            pltpu.VMEM((N_ITEMS, LANES, 2 * LANES), F32),
            pltpu.VMEM((TQ_ATT // ROUTE_ROWS, ROUTE_ROWS, LANES), F32),
        ],
        compiler_params=pltpu.CompilerParams(
            dimension_semantics=("arbitrary",), vmem_limit_bytes=VMEM_LIMIT),
        name="attn_out",
    )(attn_sinks[0] * LOG2_E, x, yp, q, kd, vt, kd_meta, vt_meta, w_out_b, w_r, upper,
      fgain, w_gate[0], w_up[0], w_down[0])

    cnt_flat = cnt[:, :N_GROUPS, 0].reshape(-1)
    out = pl.pallas_call(
        _moe_kernel,
        grid_spec=pltpu.PrefetchScalarGridSpec(
            num_scalar_prefetch=1,
            grid=(T // TM_MOE,),
            in_specs=[
                pl.BlockSpec((TM_MOE, D), lambda i, c: (i, 0)),
                pl.BlockSpec((TM_MOE, LANES), lambda i, c: (i, 0)),
                pl.BlockSpec((TM_MOE // ROUTE_ROWS, SUBLANES, ROUTE_ROWS), lambda i, c: (i, 0, 0)),
                pl.BlockSpec((TM_MOE, D), lambda i, c: (i, 0)),
                _resident_spec((N_EXPERTS, D, D_EXPERT)),
                _resident_spec((N_EXPERTS, D, D_EXPERT)),
                _resident_spec((N_EXPERTS * D_EXPERT, D)),
            ],
            out_specs=pl.BlockSpec((TM_MOE, D), lambda i, c: (i, 0)),
            scratch_shapes=[pltpu.VMEM((YS_ROWS, D), BF16),
                            pltpu.VMEM((2, max(SUB, *PASS_SIZES), D_GROUP), BF16),
                            pltpu.SMEM((3 * N_GROUPS,), jnp.int32)],
        ),
        out_shape=jax.ShapeDtypeStruct((T, D), F32),
        compiler_params=pltpu.CompilerParams(
            dimension_semantics=("arbitrary",), vmem_limit_bytes=VMEM_LIMIT),
        name="moe",
    )(cnt_flat, m.reshape(T, D), table.reshape(T, LANES), rows, h2.reshape(T, D), wg, wu,
      wd.reshape(N_EXPERTS * D_EXPERT, D))
    return out.reshape(B, S, D)
```

```python
import functools

import numpy as np
import jax
import jax.numpy as jnp
from jax import lax
from jax.experimental import pallas as pl
from jax.experimental.pallas import tpu as pltpu

D_MODEL = 1024
N_META = 16
POOL_WIDTH = 512
POOL_WINDOWS = (2, 4, 8, 16)
POOL_GROUP = 128
HEAD_DIM = 64
N_HEADS = 8
N_KV_HEADS = 2
ATTN_WIDTH = N_HEADS * HEAD_DIM
KV_WIDTH = N_KV_HEADS * HEAD_DIM
WINDOW = 128
ROT_DIM = HEAD_DIM // 4
ROPE_THETA = 500000.0
IN_WIDTH = POOL_WIDTH + ATTN_WIDTH + 2 * KV_WIDTH
N_GROUPS = 4
EXPERTS_PER_GROUP = 4
N_EXPERTS = 16
D_EXPERT = 256
EPS = 1e-6
NEG_INF = -1e30
LOG2_E = 1.4426950408889634

LANES = 128
SUBLANES = 8
KV_EXP = 4 * LANES
TQ_IN = 2048
RB_IN = 256
FIN_BLOCKS = 3
TQ_ATT = 1024
KEY_ROWS = 2 * (2 * WINDOW + N_META)
KEY_PAD = -(-KEY_ROWS // LANES) * LANES
OUT_BLOCKS = 2
ROUTE_ROWS = OUT_BLOCKS * WINDOW
W_SLICES = 2
STAGE_LAGS = (0, 1, 2)
N_ITEMS = (TQ_ATT // WINDOW) * N_KV_HEADS
TM_MOE = 512
SUB = 144
PASS_SIZES = (128, 144, 160)
BF16_ROWS = 16
MXU_DEPTH = 256
GID_LANE = N_EXPERTS
RANK_LANE = 24
LO_SHIFT = 32
D_GROUP = EXPERTS_PER_GROUP * D_EXPERT
VMEM_LIMIT = 56 * 1024 * 1024

BF16 = jnp.bfloat16
F32 = jnp.float32


def _rope_tables(n_pos):
    half = ROT_DIM // 2
    inv_freq = 1.0 / (ROPE_THETA ** (np.arange(half, dtype=np.float64) / half))
    ang = np.arange(n_pos, dtype=np.float64)[:, None] * inv_freq[None, :]
    cos, sin = np.cos(ang), np.sin(ang)
    c = np.ones((n_pos, HEAD_DIM)); c[:, :half] = cos; c[:, half:ROT_DIM] = cos
    sa = np.zeros((n_pos, HEAD_DIM)); sa[:, :half] = -sin
    sb = np.zeros((n_pos, HEAD_DIM)); sb[:, half:ROT_DIM] = sin
    tab = np.concatenate([np.tile(c, (1, 2)), np.tile(sa, (1, 2)), np.tile(sb, (1, 2))], axis=1)
    return jnp.asarray(tab, dtype=F32)


def _rms_unit(x):
    ms = jnp.mean(x * x, axis=-1, keepdims=True)
    return x * lax.rsqrt(ms + EPS)


def _head_norm_rope_many(xs, gains, ropes):
    los = [lax.broadcasted_iota(jnp.int32, x.shape, 1) < HEAD_DIM for x in xs]
    sums = []
    for x, lo in zip(xs, los):
        sq = x * x
        sums.append((jnp.sum(jnp.where(lo, sq, 0.0), axis=-1, keepdims=True),
                     jnp.sum(jnp.where(lo, 0.0, sq), axis=-1, keepdims=True)))
    ys = [x * lax.rsqrt(jnp.where(lo, s_lo, s_hi) + HEAD_DIM * EPS) * g
          for x, g, lo, (s_lo, s_hi) in zip(xs, gains, los, sums)]
    half = ROT_DIM // 2
    rolled = [(pltpu.roll(y, LANES - half, 1), pltpu.roll(y, half, 1)) for y in ys]
    return [y * rope[:, 0:LANES] + ra * rope[:, LANES:2 * LANES] + rb * rope[:, 2 * LANES:3 * LANES]
            for y, (ra, rb), rope in zip(ys, rolled, ropes)]


def _head_norm_rope(xc, gain2, rope):
    return _head_norm_rope_many([xc], [gain2], [rope])[0]


def _expand_kv(t):
    lo = lax.broadcasted_iota(jnp.int32, t.shape, 1) < HEAD_DIM
    sw = pltpu.roll(t, HEAD_DIM, 1)
    z = jnp.zeros_like(t)
    return jnp.concatenate([jnp.where(lo, t, z), jnp.where(lo, z, sw),
                            jnp.where(lo, sw, z), jnp.where(lo, z, t)], axis=1)


def _project(x, w_in):
    return jnp.dot(_rms_unit(x).astype(BF16), w_in, preferred_element_type=F32)


def _meta_kernel(meta_ref, win_ref, kgain_ref, rope_ref, u_ref, kd_ref, v_ref):
    proj = _project(meta_ref[...], win_ref[...])
    u_ref[...] = proj[:, :POOL_WIDTH]
    k = proj[:, POOL_WIDTH + ATTN_WIDTH:POOL_WIDTH + ATTN_WIDTH + KV_WIDTH]
    v = proj[:, POOL_WIDTH + ATTN_WIDTH + KV_WIDTH:]
    k = _head_norm_rope(k, kgain_ref[...], rope_ref[...])
    kd_ref[...] = _expand_kv(k).astype(BF16)
    v_ref[...] = v.astype(BF16)


def _in_kernel(x_ref, umeta_ref, win_ref, wpool_ref, pscale_ref, qgain_ref, kgain_ref,
               rope_ref, yp_ref, q_ref, kd_ref, vt_ref, carry_ref, proj_ref):
    j = pl.program_id(1)

    @pl.when(j == 0)
    def _():
        carry_ref[...] = umeta_ref[...]

    sizes = [RB_IN] * (TQ_IN // RB_IN)
    blocks = [slice(sum(sizes[:i]), sum(sizes[:i + 1])) for i in range(len(sizes))]
    n_sub = len(blocks)

    def normalise(i):
        return _rms_unit(x_ref[0, blocks[i], :]).astype(BF16)

    def project(i, a):
        proj_ref[blocks[i], :] = jnp.dot(a, win_ref[...], preferred_element_type=F32)

    def pool(i):
        rows = blocks[i]
        u = proj_ref[rows, :POOL_WIDTH]
        prev = carry_ref[...] if i == 0 else proj_ref[rows.start - N_META:rows.start, :POOL_WIDTH]
        acc = jnp.concatenate([prev, u], axis=0)
        for gi, w in enumerate(POOL_WINDOWS):
            lo = gi * POOL_GROUP
            acc = acc[:, POOL_GROUP * (1 if gi else 0):]
            acc = acc + pltpu.roll(acc, w // 2, 0)
            mixed = acc[N_META:, :POOL_GROUP] * (1.0 / w) - u[:, lo:lo + POOL_GROUP]
            y = jnp.dot(mixed.astype(BF16), wpool_ref[gi], preferred_element_type=F32)
            yp_ref[0, rows, lo:lo + POOL_GROUP] = (y * pscale_ref[:, lo:lo + POOL_GROUP]).astype(BF16)

    n_qc = ATTN_WIDTH // LANES

    def finish(ids):
        for i in ids:
            pool(i)
        xs, gains, ropes = [], [], []
        for i in ids:
            rows = blocks[i]
            xs += [proj_ref[rows, POOL_WIDTH + c * LANES:POOL_WIDTH + (c + 1) * LANES] for c in range(n_qc + 1)]
            gains += [qgain_ref[...]] * n_qc + [kgain_ref[...]]
            ropes += [rope_ref[rows, :]] * (n_qc + 1)
        outs = _head_norm_rope_many(xs, gains, ropes)
        for k, i in enumerate(ids):
            rows = blocks[i]
            mine = outs[k * (n_qc + 1):(k + 1) * (n_qc + 1)]
            for c in range(n_qc):
                q_ref[0, rows, c * LANES:(c + 1) * LANES] = mine[c].astype(BF16)
            kd_ref[0, rows, :] = _expand_kv(mine[n_qc]).astype(BF16)
            v = proj_ref[rows, POOL_WIDTH + ATTN_WIDTH + KV_WIDTH:]
            vt_ref[0, :, rows] = jnp.transpose(v).astype(BF16)

    a_next = normalise(0)
    pending = []
    for i in range(n_sub):
        project(i, a_next)
        if i + 1 < n_sub:
            a_next = normalise(i + 1)
        pending.append(i)
        if len(pending) > FIN_BLOCKS:
            finish(pending[:FIN_BLOCKS])
            pending = pending[FIN_BLOCKS:]
    finish(pending)
    carry_ref[...] = proj_ref[TQ_IN - N_META:, :POOL_WIDTH]


def _route(logits, upper):
    n = logits.shape[0]
    lt = jnp.transpose(logits)
    sub = SUBLANES
    row8 = lax.broadcasted_iota(jnp.int32, (sub, n), 0).astype(F32)
    row16 = lax.broadcasted_iota(jnp.int32, (N_EXPERTS, n), 0).astype(F32)
    g_ok = row8 < N_GROUPS
    gl = jnp.where(g_ok, lt[N_EXPERTS:N_EXPERTS + sub, :], NEG_INF)
    gmax = jnp.max(gl, axis=0, keepdims=True)
    gsum = jnp.sum(jnp.where(g_ok, jnp.exp(gl - gmax), 0.0), axis=0, keepdims=True)
    g_prob = 1.0 / gsum
    g_idx = jnp.min(jnp.where(gl == gmax, row8, float(sub)), axis=0, keepdims=True)
    e_lo = g_idx * EXPERTS_PER_GROUP
    emask = (row16 >= e_lo) & (row16 < e_lo + EXPERTS_PER_GROUP)
    el = jnp.where(emask, lt[:N_EXPERTS, :], NEG_INF)
    big = float(N_EXPERTS)
    e1 = jnp.max(el, axis=0, keepdims=True)
    i1 = jnp.min(jnp.where(el == e1, row16, big), axis=0, keepdims=True)
    el2 = jnp.where(row16 == i1, NEG_INF, el)
    e2 = jnp.max(el2, axis=0, keepdims=True)
    i2 = jnp.min(jnp.where(el2 == e2, row16, big), axis=0, keepdims=True)
    t = jnp.exp(e2 - e1)
    w1 = 1.0 / (1.0 + t)
    w2 = t * w1
    gates_t = jnp.where(row16 == i1, w1 * g_prob, 0.0) + jnp.where(row16 == i2, w2 * g_prob, 0.0)
    onehot = jnp.where(row8 == g_idx, 1.0, 0.0)
    earlier = jnp.dot(onehot.astype(BF16), upper, preferred_element_type=F32)
    rank = jnp.sum(onehot * earlier, axis=0, keepdims=True)
    hi = gates_t.astype(BF16).astype(F32)
    lo = (gates_t - hi).astype(BF16).astype(F32)
    gid8 = jnp.broadcast_to(g_idx, (sub, n))
    rank8 = jnp.broadcast_to(rank, (sub, n))
    assert GID_LANE == N_EXPERTS and RANK_LANE == GID_LANE + sub and LO_SHIFT == RANK_LANE + sub
    full_t = jnp.concatenate(
        [hi, gid8, rank8, lo, jnp.zeros((LANES - LO_SHIFT - N_EXPERTS, n), F32)], axis=0)
    rows = jnp.concatenate([g_idx, rank, jnp.zeros((sub - 2, n), F32)], axis=0)
    counts = jnp.sum(onehot, axis=1, keepdims=True)
    return (jnp.transpose(full_t).astype(BF16), rows,
            jnp.broadcast_to(counts, (sub, LANES)).astype(jnp.int32))


def _fold_rows(x, pair, final):
    rows = x.shape[0]
    while rows > SUBLANES:
        rows //= 2
        x = pair(x[:rows], x[rows:])
    return final(x, axis=0, keepdims=True)


def _attn_kernel(n_tiles, tiles_per_row, sink_ref, x_ref, yp_ref, q_ref, kd_ref, vt_ref, kmeta_ref, vtmeta_ref, wout_ref,
                 wr_ref, upper_ref, fg_ref, wg32_ref, wu32_ref, wd32_ref,
                 h2_ref, m_ref, table_ref, rows_ref, cnt_ref, wg_ref, wu_ref, wd_ref,
                 yattn_ref, kcat_ref, vbd_ref, s_ref, p_ref, inv_ref, logit_ref):
    step = pl.program_id(0)
    tile = jnp.minimum(step, n_tiles - 1)
    j = lax.rem(tile, tiles_per_row)
    slot = lax.rem(step, 2)
    nt = (((1,), (1,)), ((), ()))
    n_r = TQ_ATT // WINDOW
    band = 2 * WINDOW
    m0 = 2 * band

    @pl.when(step == 0)
    def _():
        vbd_ref[...] = jnp.zeros(vbd_ref.shape, BF16)
        p_ref[...] = jnp.zeros(p_ref.shape, BF16)
        yattn_ref[...] = jnp.zeros(yattn_ref.shape, BF16)

    wg_ref[0] = (wg32_ref[0] * fg_ref[...]).astype(BF16)
    wu_ref[0] = (wu32_ref[0] * fg_ref[...]).astype(BF16)
    wd_ref[0] = wd32_ref[0].astype(BF16)

    def window_start(r):
        return pl.multiple_of(jnp.maximum(j * TQ_ATT + (r - 1) * WINDOW, 0), WINDOW)

    def stage_scores(r):
        start = window_start(r)
        for g in range(N_KV_HEADS):
            vb = vt_ref[0, g * HEAD_DIM:(g + 1) * HEAD_DIM, pl.ds(start, band)]
            vm = vtmeta_ref[g * HEAD_DIM:(g + 1) * HEAD_DIM, :]
            for hh in range(2):
                col = (2 * g + hh) * LANES
                kcat_ref[r, g, hh * band:(hh + 1) * band, :] = kd_ref[0, pl.ds(start, band), col:col + LANES]
                kcat_ref[r, g, m0 + hh * N_META:m0 + (hh + 1) * N_META, :] = kmeta_ref[:, col:col + LANES]
                rows = slice(hh * HEAD_DIM, (hh + 1) * HEAD_DIM)
                vbd_ref[r, g, rows, hh * band:(hh + 1) * band] = vb
                vbd_ref[r, g, rows, m0 + hh * N_META:m0 + (hh + 1) * N_META] = vm
            rq = slice(r * WINDOW, (r + 1) * WINDOW)
            qq = jnp.concatenate([q_ref[0, rq, (2 * g + ch) * LANES:(2 * g + ch + 1) * LANES]
                                  for ch in range(2)], axis=0)
            s_ref[r * N_KV_HEADS + g] = lax.dot_general(kcat_ref[r, g], qq, nt, preferred_element_type=F32)

    def stage_softmax(r):
        kpos = window_start(r) + lax.broadcasted_iota(jnp.int32, (band, WINDOW), 0)
        qpos = j * TQ_ATT + r * WINDOW + lax.broadcasted_iota(jnp.int32, (band, WINDOW), 1)
        d = qpos - kpos
        bias = jnp.where((d >= 0) & (d < WINDOW), 0.0, NEG_INF)
        bias = jnp.concatenate([bias, bias], axis=1)
        left = lax.broadcasted_iota(jnp.int32, (1, 2 * WINDOW), 1) < WINDOW
        for g in range(N_KV_HEADS):
            it = r * N_KV_HEADS + g
            for hh in range(2):
                sink = jnp.where(left, sink_ref[4 * g + hh], sink_ref[4 * g + 2 + hh])
                s = s_ref[it, hh * band:(hh + 1) * band, :] + bias
                sm = s_ref[it, m0 + hh * N_META:m0 + (hh + 1) * N_META, :]
                mx = jnp.maximum(_fold_rows(s, jnp.maximum, jnp.max), jnp.max(sm, axis=0, keepdims=True))
                mx = jnp.maximum(mx, sink)
                p = jnp.exp2(s - mx)
                pm = jnp.exp2(sm - mx)
                den = (_fold_rows(p, jnp.add, jnp.sum) + jnp.sum(pm, axis=0, keepdims=True)
                       + jnp.exp2(sink - mx))
                p_ref[it, hh * band:(hh + 1) * band, :] = p.astype(BF16)
                p_ref[it, m0 + hh * N_META:m0 + (hh + 1) * N_META, :] = pm.astype(BF16)
                inv_ref[it, hh * HEAD_DIM:(hh + 1) * HEAD_DIM, :] = jnp.broadcast_to(
                    1.0 / den, (HEAD_DIM, 2 * WINDOW))

    def stage_values(r):
        for g in range(N_KV_HEADS):
            it = r * N_KV_HEADS + g
            o_t = jnp.dot(vbd_ref[r, g], p_ref[it], preferred_element_type=F32) * inv_ref[it]
            for ch in range(2):
                c = 2 * g + ch
                yattn_ref[slot, r * WINDOW:(r + 1) * WINDOW, c * LANES:(c + 1) * LANES] = (
                    jnp.transpose(o_t[:, ch * WINDOW:(ch + 1) * WINDOW]).astype(BF16))

    def stage_out():
        mix = (jnp.dot(yp_ref[0], wout_ref[:POOL_WIDTH, :], preferred_element_type=F32)
               + jnp.dot(yattn_ref[1 - slot], wout_ref[POOL_WIDTH:, :], preferred_element_type=F32))
        h2 = x_ref[0] + mix
        h2_ref[0] = h2
        m = _rms_unit(h2).astype(BF16)
        m_ref[0] = m
        logits = jnp.dot(m, wr_ref[...], preferred_element_type=F32)
        for p in range(TQ_ATT // ROUTE_ROWS):
            logit_ref[p] = logits[p * ROUTE_ROWS:(p + 1) * ROUTE_ROWS, :]

    def stage_route(p):
        table, rows, counts = _route(logit_ref[p], upper_ref[...])
        table_ref[0, p * ROUTE_ROWS:(p + 1) * ROUTE_ROWS, :] = table
        rows_ref[p] = rows
        cnt_ref[p] = counts

    n_p = TQ_ATT // ROUTE_ROWS
    for t in range(max(n_r + STAGE_LAGS[-1], n_p + 1)):
        for k, stage in zip(STAGE_LAGS, (stage_scores, stage_softmax, stage_values)):
            if 0 <= t - k < n_r:
                stage(t - k)
        if t == 0:
            stage_out()
        if 1 <= t <= n_p:
            stage_route(t - 1)


def _pair_plan(na, nb):
    top = PASS_SIZES[-1]
    if isinstance(na, int) and isinstance(nb, int):
        if 1 <= na <= top and 1 <= nb <= top:
            size = next(s for s in PASS_SIZES if max(na, nb) <= s)
            return 1, size, size, size
        return 0, SUB, -(-na // SUB) * SUB, -(-nb // SUB) * SUB
    fast = (na >= 1) & (na <= top) & (nb >= 1) & (nb <= top)
    size = jnp.int32(top)
    for s in reversed(PASS_SIZES[:-1]):
        size = jnp.where(jnp.maximum(na, nb) <= s, s, size)
    slow = [lax.div(n + (SUB - 1), SUB) * SUB for n in (na, nb)]
    return (fast.astype(jnp.int32), size,
            jnp.where(fast, size, slow[0]), jnp.where(fast, size, slow[1]))


def _max_sorted_rows(n_tokens):
    pair = [max(sum(_pair_plan(na, t - na)[2:]) for na in range(t + 1)) for t in range(n_tokens + 1)]
    return max(pair[t] + pair[n_tokens - t] for t in range(n_tokens + 1))


def _moe_kernel(cnt_ref, m_ref, table_ref, rows_ref, h2_ref, wg_ref, wu_ref, wd_ref, out_ref,
                ys_ref, hid_ref, plan_ref):
    i = pl.program_id(0)
    per = TM_MOE // ROUTE_ROWS
    counts = [[cnt_ref[(i * per + h) * N_GROUPS + g] for g in range(N_GROUPS)] for h in range(per)]
    n_tok = [sum(counts[h][g] for h in range(per)) for g in range(N_GROUPS)]
    base = []
    total_rows = 0
    for p in range(N_GROUPS // 2):
        fast, size, rows_a, rows_b = _pair_plan(n_tok[2 * p], n_tok[2 * p + 1])
        plan_ref[2 * N_GROUPS + p] = fast
        plan_ref[2 * N_GROUPS + N_GROUPS // 2 + p] = size
        for g, rows in ((2 * p, rows_a), (2 * p + 1, rows_b)):
            base.append(total_rows)
            plan_ref[g] = n_tok[g]
            plan_ref[N_GROUPS + g] = total_rows + jnp.int32(0)
            total_rows = total_rows + rows
    first = [[base[g] + sum(counts[hh][g] for hh in range(h)) for g in range(N_GROUPS)] for h in range(per)]

    @pl.when(i == 0)
    def _():
        ys_ref[...] = jnp.zeros(ys_ref.shape, BF16)

    def pick(gid, offsets):
        out = _as_f32(offsets[N_GROUPS - 1])
        for g in range(N_GROUPS - 2, -1, -1):
            out = jnp.where(gid == g, _as_f32(offsets[g]), out)
        return out

    table = table_ref[...]
    table_f = table.astype(F32)
    gid_col = table_f[:, GID_LANE:GID_LANE + 1]
    row_id = lax.broadcasted_iota(jnp.int32, (TM_MOE, 1), 0)
    off_col = pick(gid_col, first[per - 1])
    for h in range(per - 2, -1, -1):
        off_col = jnp.where(row_id < (h + 1) * ROUTE_ROWS, pick(gid_col, first[h]), off_col)
    pos_col = table_f[:, RANK_LANE:RANK_LANE + 1] + off_col
    pos_row = jnp.concatenate(
        [rows_ref[h, 1:2, :] + pick(rows_ref[h, 0:1, :], first[h]) for h in range(per)], axis=1)

    def gather(g, r0, size):
        rows = (r0 + lax.broadcasted_iota(jnp.int32, (size, 1), 0)).astype(F32)
        perm = jnp.where(pos_row == rows, 1.0, 0.0).astype(BF16)
        xs = jnp.dot(perm, m_ref[...], preferred_element_type=F32).astype(BF16)
        gs = jnp.dot(perm, table, preferred_element_type=F32)
        return xs, gs

    def experts_up(g, xs, gs, size, slot):
        lane = lax.broadcasted_iota(jnp.int32, gs.shape, 1)
        for jj in range(EXPERTS_PER_GROUP):
            e = g * EXPERTS_PER_GROUP + jj
            gt = jnp.dot(xs, wg_ref[e], preferred_element_type=F32)
            up = jnp.dot(xs, wu_ref[e], preferred_element_type=F32)
            gate = jnp.sum(jnp.where((lane == e) | (lane == e + LO_SHIFT), gs, 0.0),
                           axis=-1, keepdims=True)
            hid = gt * (1.0 / (1.0 + jnp.exp(-gt))) * up * gate
            hid_ref[slot, :size, jj * D_EXPERT:(jj + 1) * D_EXPERT] = hid.astype(BF16)

    def experts_down(g, r0, size, slot):
        w_down = wd_ref[pl.ds(pl.multiple_of(g * D_GROUP, D_GROUP), D_GROUP), :]
        y = jnp.dot(hid_ref[slot, :size, :], w_down, preferred_element_type=F32)
        ys_ref[pl.ds(r0, size), :] = y.astype(BF16)

    def run_passes(jobs, size):
        got = [gather(g, r0, size) for g, r0 in jobs]
        for slot, ((g, _), (xs, gs)) in enumerate(zip(jobs, got)):
            experts_up(g, xs, gs, size, slot)
        for slot, (g, r0) in enumerate(jobs):
            experts_down(g, r0, size, slot)

    def run_pair(p, carry):
        groups = (2 * p, 2 * p + 1)
        firsts = [pl.multiple_of(plan_ref[N_GROUPS + g], BF16_ROWS) for g in groups]
        fast = plan_ref[2 * N_GROUPS + p]
        size_p = plan_ref[2 * N_GROUPS + N_GROUPS // 2 + p]
        for size in PASS_SIZES:
            @pl.when((fast == 1) & (size_p == size))
            def _(size=size):
                run_passes(list(zip(groups, firsts)), size)

        @pl.when(fast == 0)
        def _():
            for g, r0 in zip(groups, firsts):
                def body(c, inner, g=g, r0=r0):
                    run_passes([(g, pl.multiple_of(r0 + c * SUB, BF16_ROWS))], SUB)
                    return inner
                lax.fori_loop(0, lax.div(plan_ref[g] + (SUB - 1), SUB), body, 0)
        return carry

    lax.fori_loop(0, N_GROUPS // 2, run_pair, 0)

    def scatter(n_rows):
        cols = lax.broadcasted_iota(jnp.int32, (1, n_rows), 1).astype(F32)
        perm_t = jnp.where(pos_col == cols, 1.0, 0.0).astype(BF16)
        out_ref[...] = h2_ref[...] + jnp.dot(perm_t, ys_ref[:n_rows, :], preferred_element_type=F32)

    short = total_rows <= YS_SHORT

    @pl.when(short)
    def _():
        scatter(YS_SHORT)

    @pl.when(jnp.logical_not(short))
    def _():
        scatter(YS_ROWS)


YS_ROWS = -(-_max_sorted_rows(TM_MOE) // MXU_DEPTH) * MXU_DEPTH
YS_SHORT = YS_ROWS - MXU_DEPTH


def _as_f32(v):
    return float(v) if isinstance(v, int) else v.astype(F32)


def _const_spec(shape):
    n = len(shape)
    return pl.BlockSpec(shape, lambda *_: (0,) * n)


def _resident_spec(shape):
    n = len(shape)
    return pl.BlockSpec(shape, lambda *_: (0,) * n, pipeline_mode=pl.Buffered(1))


def kernel(x, meta_tokens, attn_norm_gain, w_in, w_pool, pool_scale, q_norm_gain, k_norm_gain,
           attn_sinks, w_out, ffn_norm_gain, w_group_router, w_expert_router, w_gate, w_up, w_down):
    B, S, D = x.shape
    assert D == D_MODEL and S % TQ_IN == 0 and S % TQ_ATT == 0 and (B * S) % TM_MOE == 0
    assert w_in.shape[0] == 1, "single layer"
    T = B * S

    rope = _rope_tables(N_META + S)
    fgain = ffn_norm_gain[0][:, None]
    qgain2 = jnp.tile(q_norm_gain[0] * LOG2_E, 2)[None, :]
    kgain2 = jnp.tile(k_norm_gain[0] * (HEAD_DIM ** 0.5), 2)[None, :]
    pscale = pool_scale[0][None, :]
    w_in_b = (attn_norm_gain[0][:, None] * w_in[0]).astype(BF16)
    w_pool_b = w_pool[0].astype(BF16)
    w_out_b = w_out[0].astype(BF16)
    w_r = (fgain * jnp.concatenate(
        [w_expert_router[0], w_group_router[0],
         jnp.zeros((D, LANES - N_EXPERTS - N_GROUPS), F32)], axis=1)).astype(BF16)
    params = pltpu.CompilerParams(vmem_limit_bytes=VMEM_LIMIT)

    u_meta, kd_meta, v_meta = pl.pallas_call(
        _meta_kernel,
        out_shape=(jax.ShapeDtypeStruct((N_META, POOL_WIDTH), F32),
                   jax.ShapeDtypeStruct((N_META, KV_EXP), BF16),
                   jax.ShapeDtypeStruct((N_META, KV_WIDTH), BF16)),
        compiler_params=params,
        name="meta_proj",
    )(meta_tokens, w_in_b, kgain2, rope[:N_META])
    vt_meta = v_meta.T

    yp, q, kd, vt = pl.pallas_call(
        _in_kernel,
        grid=(B, S // TQ_IN),
        in_specs=[
            pl.BlockSpec((1, TQ_IN, D), lambda b, j: (b, j, 0)),
            _const_spec((N_META, POOL_WIDTH)),
            _const_spec((D, IN_WIDTH)),
            _const_spec((len(POOL_WINDOWS), POOL_GROUP, POOL_GROUP)),
            _const_spec((1, POOL_WIDTH)),
            _const_spec((1, LANES)),
            _const_spec((1, LANES)),
            pl.BlockSpec((TQ_IN, 3 * LANES), lambda b, j: (j, 0)),
        ],
        out_specs=(
            pl.BlockSpec((1, TQ_IN, POOL_WIDTH), lambda b, j: (b, j, 0)),
            pl.BlockSpec((1, TQ_IN, ATTN_WIDTH), lambda b, j: (b, j, 0)),
            pl.BlockSpec((1, TQ_IN, KV_EXP), lambda b, j: (b, j, 0)),
            pl.BlockSpec((1, KV_WIDTH, TQ_IN), lambda b, j: (b, 0, j)),
        ),
        out_shape=(jax.ShapeDtypeStruct((B, S, POOL_WIDTH), BF16),
                   jax.ShapeDtypeStruct((B, S, ATTN_WIDTH), BF16),
                   jax.ShapeDtypeStruct((B, S, KV_EXP), BF16),
                   jax.ShapeDtypeStruct((B, KV_WIDTH, S), BF16)),
        scratch_shapes=[pltpu.VMEM((N_META, POOL_WIDTH), F32),
                        pltpu.VMEM((TQ_IN, IN_WIDTH), F32)],
        compiler_params=pltpu.CompilerParams(
            dimension_semantics=("arbitrary", "arbitrary"), vmem_limit_bytes=VMEM_LIMIT),
        name="in_proj",
    )(x, u_meta, w_in_b, w_pool_b, pscale, qgain2, kgain2, rope[N_META:])

    upper = jnp.asarray(np.triu(np.ones((ROUTE_ROWS, ROUTE_ROWS), np.float32), 1), dtype=BF16)
    per_row = S // TQ_ATT
    n_tiles = B * per_row

    def cur(s):
        t = jnp.minimum(s, n_tiles - 1)
        return t // per_row, t % per_row

    def lag(s):
        t = jnp.maximum(s - 1, 0)
        return t // per_row, t % per_row

    def lag_rows(s):
        return (*lag(s), 0)

    def lag_flat(s):
        return (jnp.maximum(s - 1, 0), 0, 0)

    w_steps = N_EXPERTS * W_SLICES
    assert n_tiles >= w_steps and D % W_SLICES == 0 and D_EXPERT % W_SLICES == 0

    def w_slice(s):
        k = jnp.minimum(s, w_steps - 1)
        return k // W_SLICES, k % W_SLICES, 0

    h2, m, table, rows, cnt, wg, wu, wd = pl.pallas_call(
        functools.partial(_attn_kernel, n_tiles, per_row),
        grid=(n_tiles + 1,),
        in_specs=[
            pl.BlockSpec(memory_space=pltpu.SMEM),
            pl.BlockSpec((1, TQ_ATT, D), lag_rows),
            pl.BlockSpec((1, TQ_ATT, POOL_WIDTH), lag_rows),
            pl.BlockSpec((1, TQ_ATT, ATTN_WIDTH), lambda s: (*cur(s), 0)),
            pl.BlockSpec((1, S, KV_EXP), lambda s: (cur(s)[0], 0, 0)),
            pl.BlockSpec((1, KV_WIDTH, S), lambda s: (cur(s)[0], 0, 0)),
            _const_spec((N_META, KV_EXP)),
            _const_spec((KV_WIDTH, N_META)),
            _const_spec((D, D)),
            _const_spec((D, LANES)),
            _const_spec((ROUTE_ROWS, ROUTE_ROWS)),
            pl.BlockSpec((D // W_SLICES, 1), lambda s: (w_slice(s)[1], 0)),
            pl.BlockSpec((1, D // W_SLICES, D_EXPERT), w_slice),
            pl.BlockSpec((1, D // W_SLICES, D_EXPERT), w_slice),
            pl.BlockSpec((1, D_EXPERT // W_SLICES, D), w_slice),
        ],
        out_specs=(
            pl.BlockSpec((1, TQ_ATT, D), lag_rows),
            pl.BlockSpec((1, TQ_ATT, D), lag_rows),
            pl.BlockSpec((1, TQ_ATT, LANES), lag_rows),
            pl.BlockSpec((TQ_ATT // ROUTE_ROWS, SUBLANES, ROUTE_ROWS), lag_flat),
            pl.BlockSpec((TQ_ATT // ROUTE_ROWS, SUBLANES, LANES), lag_flat),
            pl.BlockSpec((1, D // W_SLICES, D_EXPERT), w_slice),
            pl.BlockSpec((1, D // W_SLICES, D_EXPERT), w_slice),
            pl.BlockSpec((1, D_EXPERT // W_SLICES, D), w_slice),
        ),
        out_shape=(jax.ShapeDtypeStruct((B, S, D), F32),
                   jax.ShapeDtypeStruct((B, S, D), BF16),
                   jax.ShapeDtypeStruct((B, S, LANES), BF16),
                   jax.ShapeDtypeStruct((T // ROUTE_ROWS, SUBLANES, ROUTE_ROWS), F32),
                   jax.ShapeDtypeStruct((T // ROUTE_ROWS, SUBLANES, LANES), jnp.int32),
                   jax.ShapeDtypeStruct((N_EXPERTS, D, D_EXPERT), BF16),
                   jax.ShapeDtypeStruct((N_EXPERTS, D, D_EXPERT), BF16),
                   jax.ShapeDtypeStruct((N_EXPERTS, D_EXPERT, D), BF16)),
        scratch_shapes=[
            pltpu.VMEM((2, TQ_ATT, ATTN_WIDTH), BF16),
            pltpu.VMEM((TQ_ATT // WINDOW, N_KV_HEADS, KEY_ROWS, LANES), BF16),
            pltpu.VMEM((TQ_ATT // WINDOW, N_KV_HEADS, LANES, KEY_PAD), BF16),
            pltpu.VMEM((N_ITEMS, KEY_ROWS, 2 * LANES), F32),
            pltpu.VMEM((N_ITEMS, KEY_PAD, 2 * LANES), BF16),
            pltpu.VMEM((N_ITEMS, LANES, 2 * LANES), F32),
            pltpu.VMEM((TQ_ATT // ROUTE_ROWS, ROUTE_ROWS, LANES), F32),
        ],
        compiler_params=pltpu.CompilerParams(
            dimension_semantics=("arbitrary",), vmem_limit_bytes=VMEM_LIMIT),
        name="attn_out",
    )(attn_sinks[0] * LOG2_E, x, yp, q, kd, vt, kd_meta, vt_meta, w_out_b, w_r, upper,
      fgain, w_gate[0], w_up[0], w_down[0])

    cnt_flat = cnt[:, :N_GROUPS, 0].reshape(-1)
    out = pl.pallas_call(
        _moe_kernel,
        grid_spec=pltpu.PrefetchScalarGridSpec(
            num_scalar_prefetch=1,
            grid=(T // TM_MOE,),
            in_specs=[
                pl.BlockSpec((TM_MOE, D), lambda i, c: (i, 0)),
                pl.BlockSpec((TM_MOE, LANES), lambda i, c: (i, 0)),
                pl.BlockSpec((TM_MOE // ROUTE_ROWS, SUBLANES, ROUTE_ROWS), lambda i, c: (i, 0, 0)),
                pl.BlockSpec((TM_MOE, D), lambda i, c: (i, 0)),
                _resident_spec((N_EXPERTS, D, D_EXPERT)),
                _resident_spec((N_EXPERTS, D, D_EXPERT)),
                _resident_spec((N_EXPERTS * D_EXPERT, D)),
            ],
            out_specs=pl.BlockSpec((TM_MOE, D), lambda i, c: (i, 0)),
            scratch_shapes=[pltpu.VMEM((YS_ROWS, D), BF16),
                            pltpu.VMEM((2, max(SUB, *PASS_SIZES), D_GROUP), BF16),
                            pltpu.SMEM((3 * N_GROUPS,), jnp.int32)],
        ),
        out_shape=jax.ShapeDtypeStruct((T, D), F32),
        compiler_params=pltpu.CompilerParams(
            dimension_semantics=("arbitrary",), vmem_limit_bytes=VMEM_LIMIT),
        name="moe",
    )(cnt_flat, m.reshape(T, D), table.reshape(T, LANES), rows, h2.reshape(T, D), wg, wu,
      wd.reshape(N_EXPERTS * D_EXPERT, D))
    return out.reshape(B, S, D)
```

```python
import functools

import numpy as np
import jax
import jax.numpy as jnp
from jax import lax
from jax.experimental import pallas as pl
from jax.experimental.pallas import tpu as pltpu

D_MODEL = 1024
N_META = 16
POOL_WIDTH = 512
POOL_WINDOWS = (2, 4, 8, 16)
POOL_GROUP = 128
HEAD_DIM = 64
N_HEADS = 8
N_KV_HEADS = 2
ATTN_WIDTH = N_HEADS * HEAD_DIM
KV_WIDTH = N_KV_HEADS * HEAD_DIM
WINDOW = 128
ROT_DIM = HEAD_DIM // 4
ROPE_THETA = 500000.0
IN_WIDTH = POOL_WIDTH + ATTN_WIDTH + 2 * KV_WIDTH
N_GROUPS = 4
EXPERTS_PER_GROUP = 4
N_EXPERTS = 16
D_EXPERT = 256
EPS = 1e-6
NEG_INF = -1e30
LOG2_E = 1.4426950408889634

LANES = 128
SUBLANES = 8
KV_EXP = 4 * LANES
TQ_IN = 2048
RB_IN = 256
FIN_BLOCKS = 3
TQ_ATT = 1024
KEY_ROWS = 2 * (2 * WINDOW + N_META)
KEY_PAD = -(-KEY_ROWS // LANES) * LANES
OUT_BLOCKS = 2
ROUTE_ROWS = OUT_BLOCKS * WINDOW
W_SLICES = 2
STAGE_LAGS = (0, 1, 2)
N_ITEMS = (TQ_ATT // WINDOW) * N_KV_HEADS
TM_MOE = 512
SUB = 144
PASS_SIZES = (128, 144, 160)
BF16_ROWS = 16
MXU_DEPTH = 256
GID_LANE = N_EXPERTS
RANK_LANE = 24
LO_SHIFT = 32
D_GROUP = EXPERTS_PER_GROUP * D_EXPERT
VMEM_LIMIT = 56 * 1024 * 1024

BF16 = jnp.bfloat16
F32 = jnp.float32


def _rope_tables(n_pos):
    half = ROT_DIM // 2
    inv_freq = 1.0 / (ROPE_THETA ** (np.arange(half, dtype=np.float64) / half))
    ang = np.arange(n_pos, dtype=np.float64)[:, None] * inv_freq[None, :]
    cos, sin = np.cos(ang), np.sin(ang)
    c = np.ones((n_pos, HEAD_DIM)); c[:, :half] = cos; c[:, half:ROT_DIM] = cos
    sa = np.zeros((n_pos, HEAD_DIM)); sa[:, :half] = -sin
    sb = np.zeros((n_pos, HEAD_DIM)); sb[:, half:ROT_DIM] = sin
    tab = np.concatenate([np.tile(c, (1, 2)), np.tile(sa, (1, 2)), np.tile(sb, (1, 2))], axis=1)
    return jnp.asarray(tab, dtype=F32)


def _rms_unit(x):
    ms = jnp.mean(x * x, axis=-1, keepdims=True)
    return x * lax.rsqrt(ms + EPS)


def _head_norm_rope_many(xs, gains, ropes):
    los = [lax.broadcasted_iota(jnp.int32, x.shape, 1) < HEAD_DIM for x in xs]
    sums = []
    for x, lo in zip(xs, los):
        sq = x * x
        sums.append((jnp.sum(jnp.where(lo, sq, 0.0), axis=-1, keepdims=True),
                     jnp.sum(jnp.where(lo, 0.0, sq), axis=-1, keepdims=True)))
    ys = [x * lax.rsqrt(jnp.where(lo, s_lo, s_hi) + HEAD_DIM * EPS) * g
          for x, g, lo, (s_lo, s_hi) in zip(xs, gains, los, sums)]
    half = ROT_DIM // 2
    rolled = [(pltpu.roll(y, LANES - half, 1), pltpu.roll(y, half, 1)) for y in ys]
    return [y * rope[:, 0:LANES] + ra * rope[:, LANES:2 * LANES] + rb * rope[:, 2 * LANES:3 * LANES]
            for y, (ra, rb), rope in zip(ys, rolled, ropes)]


def _head_norm_rope(xc, gain2, rope):
    return _head_norm_rope_many([xc], [gain2], [rope])[0]


def _expand_kv(t):
    lo = lax.broadcasted_iota(jnp.int32, t.shape, 1) < HEAD_DIM
    sw = pltpu.roll(t, HEAD_DIM, 1)
    z = jnp.zeros_like(t)
    return jnp.concatenate([jnp.where(lo, t, z), jnp.where(lo, z, sw),
                            jnp.where(lo, sw, z), jnp.where(lo, z, t)], axis=1)


def _project(x, w_in):
    return jnp.dot(_rms_unit(x).astype(BF16), w_in, preferred_element_type=F32)


def _meta_kernel(meta_ref, win_ref, kgain_ref, rope_ref, u_ref, kd_ref, v_ref):
    proj = _project(meta_ref[...], win_ref[...])
    u_ref[...] = proj[:, :POOL_WIDTH]
    k = proj[:, POOL_WIDTH + ATTN_WIDTH:POOL_WIDTH + ATTN_WIDTH + KV_WIDTH]
    v = proj[:, POOL_WIDTH + ATTN_WIDTH + KV_WIDTH:]
    k = _head_norm_rope(k, kgain_ref[...], rope_ref[...])
    kd_ref[...] = _expand_kv(k).astype(BF16)
    v_ref[...] = v.astype(BF16)


def _in_kernel(x_ref, umeta_ref, win_ref, wpool_ref, pscale_ref, qgain_ref, kgain_ref,
               rope_ref, yp_ref, q_ref, kd_ref, vt_ref, carry_ref, proj_ref):
    j = pl.program_id(1)

    @pl.when(j == 0)
    def _():
        carry_ref[...] = umeta_ref[...]

    sizes = [RB_IN] * (TQ_IN // RB_IN)
    blocks = [slice(sum(sizes[:i]), sum(sizes[:i + 1])) for i in range(len(sizes))]
    n_sub = len(blocks)

    def normalise(i):
        return _rms_unit(x_ref[0, blocks[i], :]).astype(BF16)

    def project(i, a):
        proj_ref[blocks[i], :] = jnp.dot(a, win_ref[...], preferred_element_type=F32)

    def pool(i):
        rows = blocks[i]
        u = proj_ref[rows, :POOL_WIDTH]
        prev = carry_ref[...] if i == 0 else proj_ref[rows.start - N_META:rows.start, :POOL_WIDTH]
        acc = jnp.concatenate([prev, u], axis=0)
        for gi, w in enumerate(POOL_WINDOWS):
            lo = gi * POOL_GROUP
            acc = acc[:, POOL_GROUP * (1 if gi else 0):]
            acc = acc + pltpu.roll(acc, w // 2, 0)
            mixed = acc[N_META:, :POOL_GROUP] * (1.0 / w) - u[:, lo:lo + POOL_GROUP]
            y = jnp.dot(mixed.astype(BF16), wpool_ref[gi], preferred_element_type=F32)
            yp_ref[0, rows, lo:lo + POOL_GROUP] = (y * pscale_ref[:, lo:lo + POOL_GROUP]).astype(BF16)

    n_qc = ATTN_WIDTH // LANES

    def finish(ids):
        for i in ids:
            pool(i)
        xs, gains, ropes = [], [], []
        for i in ids:
            rows = blocks[i]
            xs += [proj_ref[rows, POOL_WIDTH + c * LANES:POOL_WIDTH + (c + 1) * LANES] for c in range(n_qc + 1)]
            gains += [qgain_ref[...]] * n_qc + [kgain_ref[...]]
            ropes += [rope_ref[rows, :]] * (n_qc + 1)
        outs = _head_norm_rope_many(xs, gains, ropes)
        for k, i in enumerate(ids):
            rows = blocks[i]
            mine = outs[k * (n_qc + 1):(k + 1) * (n_qc + 1)]
            for c in range(n_qc):
                q_ref[0, rows, c * LANES:(c + 1) * LANES] = mine[c].astype(BF16)
            kd_ref[0, rows, :] = _expand_kv(mine[n_qc]).astype(BF16)
            v = proj_ref[rows, POOL_WIDTH + ATTN_WIDTH + KV_WIDTH:]
            vt_ref[0, :, rows] = jnp.transpose(v).astype(BF16)

    a_next = normalise(0)
    pending = []
    for i in range(n_sub):
        project(i, a_next)
        if i + 1 < n_sub:
            a_next = normalise(i + 1)
        pending.append(i)
        if len(pending) > FIN_BLOCKS:
            finish(pending[:FIN_BLOCKS])
            pending = pending[FIN_BLOCKS:]
    finish(pending)
    carry_ref[...] = proj_ref[TQ_IN - N_META:, :POOL_WIDTH]


def _route(logits, upper):
    n = logits.shape[0]
    lt = jnp.transpose(logits)
    sub = SUBLANES
    row8 = lax.broadcasted_iota(jnp.int32, (sub, n), 0).astype(F32)
    row16 = lax.broadcasted_iota(jnp.int32, (N_EXPERTS, n), 0).astype(F32)
    g_ok = row8 < N_GROUPS
    gl = jnp.where(g_ok, lt[N_EXPERTS:N_EXPERTS + sub, :], NEG_INF)
    gmax = jnp.max(gl, axis=0, keepdims=True)
    gsum = jnp.sum(jnp.where(g_ok, jnp.exp(gl - gmax), 0.0), axis=0, keepdims=True)
    g_prob = 1.0 / gsum
    g_idx = jnp.min(jnp.where(gl == gmax, row8, float(sub)), axis=0, keepdims=True)
    e_lo = g_idx * EXPERTS_PER_GROUP
    emask = (row16 >= e_lo) & (row16 < e_lo + EXPERTS_PER_GROUP)
    el = jnp.where(emask, lt[:N_EXPERTS, :], NEG_INF)
    big = float(N_EXPERTS)
    e1 = jnp.max(el, axis=0, keepdims=True)
    i1 = jnp.min(jnp.where(el == e1, row16, big), axis=0, keepdims=True)
    el2 = jnp.where(row16 == i1, NEG_INF, el)
    e2 = jnp.max(el2, axis=0, keepdims=True)
    i2 = jnp.min(jnp.where(el2 == e2, row16, big), axis=0, keepdims=True)
    t = jnp.exp(e2 - e1)
    w1 = 1.0 / (1.0 + t)
    w2 = t * w1
    gates_t = jnp.where(row16 == i1, w1 * g_prob, 0.0) + jnp.where(row16 == i2, w2 * g_prob, 0.0)
    onehot = jnp.where(row8 == g_idx, 1.0, 0.0)
    earlier = jnp.dot(onehot.astype(BF16), upper, preferred_element_type=F32)
    rank = jnp.sum(onehot * earlier, axis=0, keepdims=True)
    hi = gates_t.astype(BF16).astype(F32)
    lo = (gates_t - hi).astype(BF16).astype(F32)
    gid8 = jnp.broadcast_to(g_idx, (sub, n))
    rank8 = jnp.broadcast_to(rank, (sub, n))
    assert GID_LANE == N_EXPERTS and RANK_LANE == GID_LANE + sub and LO_SHIFT == RANK_LANE + sub
    full_t = jnp.concatenate(
        [hi, gid8, rank8, lo, jnp.zeros((LANES - LO_SHIFT - N_EXPERTS, n), F32)], axis=0)
    rows = jnp.concatenate([g_idx, rank, jnp.zeros((sub - 2, n), F32)], axis=0)
    counts = jnp.sum(onehot, axis=1, keepdims=True)
    return (jnp.transpose(full_t).astype(BF16), rows,
            jnp.broadcast_to(counts, (sub, LANES)).astype(jnp.int32))


def _fold_rows(x, pair, final):
    rows = x.shape[0]
    while rows > SUBLANES:
        rows //= 2
        x = pair(x[:rows], x[rows:])
    return final(x, axis=0, keepdims=True)


def _attn_kernel(n_tiles, tiles_per_row, sink_ref, x_ref, yp_ref, q_ref, kd_ref, vt_ref, kmeta_ref, vtmeta_ref, wout_ref,
                 wr_ref, upper_ref, fg_ref, wg32_ref, wu32_ref, wd32_ref,
                 h2_ref, m_ref, table_ref, rows_ref, cnt_ref, wg_ref, wu_ref, wd_ref,
                 yattn_ref, kcat_ref, vbd_ref, s_ref, p_ref, inv_ref, logit_ref):
    step = pl.program_id(0)
    tile = jnp.minimum(step, n_tiles - 1)
    j = lax.rem(tile, tiles_per_row)
    slot = lax.rem(step, 2)
    nt = (((1,), (1,)), ((), ()))
    n_r = TQ_ATT // WINDOW
    band = 2 * WINDOW
    m0 = 2 * band

    @pl.when(step == 0)
    def _():
        vbd_ref[...] = jnp.zeros(vbd_ref.shape, BF16)
        p_ref[...] = jnp.zeros(p_ref.shape, BF16)
        yattn_ref[...] = jnp.zeros(yattn_ref.shape, BF16)

    wg_ref[0] = (wg32_ref[0] * fg_ref[...]).astype(BF16)
    wu_ref[0] = (wu32_ref[0] * fg_ref[...]).astype(BF16)
    wd_ref[0] = wd32_ref[0].astype(BF16)

    def window_start(r):
        return pl.multiple_of(jnp.maximum(j * TQ_ATT + (r - 1) * WINDOW, 0), WINDOW)

    def stage_scores(r):
        start = window_start(r)
        for g in range(N_KV_HEADS):
            vb = vt_ref[0, g * HEAD_DIM:(g + 1) * HEAD_DIM, pl.ds(start, band)]
            vm = vtmeta_ref[g * HEAD_DIM:(g + 1) * HEAD_DIM, :]
            for hh in range(2):
                col = (2 * g + hh) * LANES
                kcat_ref[r, g, hh * band:(hh + 1) * band, :] = kd_ref[0, pl.ds(start, band), col:col + LANES]
                kcat_ref[r, g, m0 + hh * N_META:m0 + (hh + 1) * N_META, :] = kmeta_ref[:, col:col + LANES]
                rows = slice(hh * HEAD_DIM, (hh + 1) * HEAD_DIM)
                vbd_ref[r, g, rows, hh * band:(hh + 1) * band] = vb
                vbd_ref[r, g, rows, m0 + hh * N_META:m0 + (hh + 1) * N_META] = vm
            rq = slice(r * WINDOW, (r + 1) * WINDOW)
            qq = jnp.concatenate([q_ref[0, rq, (2 * g + ch) * LANES:(2 * g + ch + 1) * LANES]
                                  for ch in range(2)], axis=0)
            s_ref[r * N_KV_HEADS + g] = lax.dot_general(kcat_ref[r, g], qq, nt, preferred_element_type=F32)

    def stage_softmax(r):
        kpos = window_start(r) + lax.broadcasted_iota(jnp.int32, (band, WINDOW), 0)
        qpos = j * TQ_ATT + r * WINDOW + lax.broadcasted_iota(jnp.int32, (band, WINDOW), 1)
        d = qpos - kpos
        bias = jnp.where((d >= 0) & (d < WINDOW), 0.0, NEG_INF)
        bias = jnp.concatenate([bias, bias], axis=1)
        left = lax.broadcasted_iota(jnp.int32, (1, 2 * WINDOW), 1) < WINDOW
        for g in range(N_KV_HEADS):
            it = r * N_KV_HEADS + g
            for hh in range(2):
                sink = jnp.where(left, sink_ref[4 * g + hh], sink_ref[4 * g + 2 + hh])
                s = s_ref[it, hh * band:(hh + 1) * band, :] + bias
                sm = s_ref[it, m0 + hh * N_META:m0 + (hh + 1) * N_META, :]
                mx = jnp.maximum(_fold_rows(s, jnp.maximum, jnp.max), jnp.max(sm, axis=0, keepdims=True))
                mx = jnp.maximum(mx, sink)
                p = jnp.exp2(s - mx)
                pm = jnp.exp2(sm - mx)
                den = (_fold_rows(p, jnp.add, jnp.sum) + jnp.sum(pm, axis=0, keepdims=True)
                       + jnp.exp2(sink - mx))
                p_ref[it, hh * band:(hh + 1) * band, :] = p.astype(BF16)
                p_ref[it, m0 + hh * N_META:m0 + (hh + 1) * N_META, :] = pm.astype(BF16)
                inv_ref[it, hh * HEAD_DIM:(hh + 1) * HEAD_DIM, :] = jnp.broadcast_to(
                    1.0 / den, (HEAD_DIM, 2 * WINDOW))

    def stage_values(r):
        for g in range(N_KV_HEADS):
            it = r * N_KV_HEADS + g
            o_t = jnp.dot(vbd_ref[r, g], p_ref[it], preferred_element_type=F32) * inv_ref[it]
            for ch in range(2):
                c = 2 * g + ch
                yattn_ref[slot, r * WINDOW:(r + 1) * WINDOW, c * LANES:(c + 1) * LANES] = (
                    jnp.transpose(o_t[:, ch * WINDOW:(ch + 1) * WINDOW]).astype(BF16))

    def stage_out():
        mix = (jnp.dot(yp_ref[0], wout_ref[:POOL_WIDTH, :], preferred_element_type=F32)
               + jnp.dot(yattn_ref[1 - slot], wout_ref[POOL_WIDTH:, :], preferred_element_type=F32))
        h2 = x_ref[0] + mix
        h2_ref[0] = h2
        m = _rms_unit(h2).astype(BF16)
        m_ref[0] = m
        logits = jnp.dot(m, wr_ref[...], preferred_element_type=F32)
        for p in range(TQ_ATT // ROUTE_ROWS):
            logit_ref[p] = logits[p * ROUTE_ROWS:(p + 1) * ROUTE_ROWS, :]

    def stage_route(p):
        table, rows, counts = _route(logit_ref[p], upper_ref[...])
        table_ref[0, p * ROUTE_ROWS:(p + 1) * ROUTE_ROWS, :] = table
        rows_ref[p] = rows
        cnt_ref[p] = counts

    n_p = TQ_ATT // ROUTE_ROWS
    for t in range(max(n_r + STAGE_LAGS[-1], n_p + 1)):
        for k, stage in zip(STAGE_LAGS, (stage_scores, stage_softmax, stage_values)):
            if 0 <= t - k < n_r:
                stage(t - k)
        if t == 0:
            stage_out()
        if 1 <= t <= n_p:
            stage_route(t - 1)


def _pair_plan(na, nb):
    top = PASS_SIZES[-1]
    if isinstance(na, int) and isinstance(nb, int):
        if 1 <= na <= top and 1 <= nb <= top:
            size = next(s for s in PASS_SIZES if max(na, nb) <= s)
            return 1, size, size, size
        return 0, SUB, -(-na // SUB) * SUB, -(-nb // SUB) * SUB
    fast = (na >= 1) & (na <= top) & (nb >= 1) & (nb <= top)
    size = jnp.int32(top)
    for s in reversed(PASS_SIZES[:-1]):
        size = jnp.where(jnp.maximum(na, nb) <= s, s, size)
    slow = [lax.div(n + (SUB - 1), SUB) * SUB for n in (na, nb)]
    return (fast.astype(jnp.int32), size,
            jnp.where(fast, size, slow[0]), jnp.where(fast, size, slow[1]))


def _max_sorted_rows(n_tokens):
    pair = [max(sum(_pair_plan(na, t - na)[2:]) for na in range(t + 1)) for t in range(n_tokens + 1)]
    return max(pair[t] + pair[n_tokens - t] for t in range(n_tokens + 1))


def _moe_kernel(n_tiles, cnt_ref, m_ref, table_ref, rows_ref, rows_next_ref, h2_ref, wg_ref, wu_ref, wd_ref,
                out_ref, ys_ref, hid_ref, plan_ref, posr_ref, posc_ref):
    i = pl.program_id(0)
    slot = lax.rem(i, 2)
    per = TM_MOE // ROUTE_ROWS

    def prepare(tile, dst, rows_blk):
        counts = [[cnt_ref[(tile * per + h) * N_GROUPS + g] for g in range(N_GROUPS)] for h in range(per)]
        n_tok = [sum(counts[h][g] for h in range(per)) for g in range(N_GROUPS)]
        base = []
        total_rows = 0
        for p in range(N_GROUPS // 2):
            fast, size, rows_a, rows_b = _pair_plan(n_tok[2 * p], n_tok[2 * p + 1])
            plan_ref[dst, 2 * N_GROUPS + p] = fast
            plan_ref[dst, 2 * N_GROUPS + N_GROUPS // 2 + p] = size
            for g, rows in ((2 * p, rows_a), (2 * p + 1, rows_b)):
                base.append(total_rows)
                plan_ref[dst, g] = n_tok[g]
                plan_ref[dst, N_GROUPS + g] = total_rows + jnp.int32(0)
                total_rows = total_rows + rows
        plan_ref[dst, 3 * N_GROUPS] = total_rows
        first = [[base[g] + sum(counts[hh][g] for hh in range(h)) for g in range(N_GROUPS)]
                 for h in range(per)]

        def pick(gid, offsets):
            out = _as_f32(offsets[N_GROUPS - 1])
            for g in range(N_GROUPS - 2, -1, -1):
                out = jnp.where(gid == g, _as_f32(offsets[g]), out)
            return out

        pos = jnp.concatenate(
            [rows_blk[h, 1:2, :] + pick(rows_blk[h, 0:1, :], first[h]) for h in range(per)], axis=1)
        posr_ref[dst] = jnp.broadcast_to(pos, (SUBLANES, TM_MOE))
        posc_ref[dst] = jnp.transpose(jnp.broadcast_to(pos, (LANES, TM_MOE)))

    @pl.when(i == 0)
    def _():
        ys_ref[...] = jnp.zeros(ys_ref.shape, BF16)
        prepare(0, 0, rows_ref)

    table = table_ref[...]
    pos_row = posr_ref[slot, 0:1, :]
    pos_col = posc_ref[slot, :, 0:1]
    total_rows = plan_ref[slot, 3 * N_GROUPS]

    def gather(g, r0, size):
        rows = (r0 + lax.broadcasted_iota(jnp.int32, (size, 1), 0)).astype(F32)
        perm = jnp.where(pos_row == rows, 1.0, 0.0).astype(BF16)
        xs = jnp.dot(perm, m_ref[...], preferred_element_type=F32).astype(BF16)
        gs = jnp.dot(perm, table, preferred_element_type=F32)
        return xs, gs

    def experts_up(g, xs, gs, size, slot):
        lane = lax.broadcasted_iota(jnp.int32, gs.shape, 1)
        for jj in range(EXPERTS_PER_GROUP):
            e = g * EXPERTS_PER_GROUP + jj
            gt = jnp.dot(xs, wg_ref[e], preferred_element_type=F32)
            up = jnp.dot(xs, wu_ref[e], preferred_element_type=F32)
            gate = jnp.sum(jnp.where((lane == e) | (lane == e + LO_SHIFT), gs, 0.0),
                           axis=-1, keepdims=True)
            hid = gt * (1.0 / (1.0 + jnp.exp(-gt))) * up * gate
            hid_ref[slot, :size, jj * D_EXPERT:(jj + 1) * D_EXPERT] = hid.astype(BF16)

    def experts_down(g, r0, size, slot):
        w_down = wd_ref[pl.ds(pl.multiple_of(g * D_GROUP, D_GROUP), D_GROUP), :]
        y = jnp.dot(hid_ref[slot, :size, :], w_down, preferred_element_type=F32)
        ys_ref[pl.ds(r0, size), :] = y.astype(BF16)

    def run_passes(jobs, size):
        got = [gather(g, r0, size) for g, r0 in jobs]
        for slot, ((g, _), (xs, gs)) in enumerate(zip(jobs, got)):
            experts_up(g, xs, gs, size, slot)
        for slot, (g, r0) in enumerate(jobs):
            experts_down(g, r0, size, slot)

    def run_pair(p, carry):
        groups = (2 * p, 2 * p + 1)
        firsts = [pl.multiple_of(plan_ref[slot, N_GROUPS + g], BF16_ROWS) for g in groups]
        fast = plan_ref[slot, 2 * N_GROUPS + p]
        size_p = plan_ref[slot, 2 * N_GROUPS + N_GROUPS // 2 + p]
        for size in PASS_SIZES:
            @pl.when((fast == 1) & (size_p == size))
            def _(size=size):
                run_passes(list(zip(groups, firsts)), size)

        @pl.when(fast == 0)
        def _():
            for g, r0 in zip(groups, firsts):
                def body(c, inner, g=g, r0=r0):
                    run_passes([(g, pl.multiple_of(r0 + c * SUB, BF16_ROWS))], SUB)
                    return inner
                lax.fori_loop(0, lax.div(plan_ref[slot, g] + (SUB - 1), SUB), body, 0)
        return carry

    lax.fori_loop(0, N_GROUPS // 2, run_pair, 0)

    def scatter(n_rows):
        prepare(jnp.minimum(i + 1, n_tiles - 1), 1 - slot, rows_next_ref)
        cols = lax.broadcasted_iota(jnp.int32, (1, n_rows), 1).astype(F32)
        perm_t = jnp.where(pos_col == cols, 1.0, 0.0).astype(BF16)
        out_ref[...] = h2_ref[...] + jnp.dot(perm_t, ys_ref[:n_rows, :], preferred_element_type=F32)

    short = total_rows <= YS_SHORT

    @pl.when(short)
    def _():
        scatter(YS_SHORT)

    @pl.when(jnp.logical_not(short))
    def _():
        scatter(YS_ROWS)


YS_ROWS = -(-_max_sorted_rows(TM_MOE) // MXU_DEPTH) * MXU_DEPTH
YS_SHORT = YS_ROWS - MXU_DEPTH


def _as_f32(v):
    return float(v) if isinstance(v, int) else v.astype(F32)


def _const_spec(shape):
    n = len(shape)
    return pl.BlockSpec(shape, lambda *_: (0,) * n)


def _resident_spec(shape):
    n = len(shape)
    return pl.BlockSpec(shape, lambda *_: (0,) * n, pipeline_mode=pl.Buffered(1))


def kernel(x, meta_tokens, attn_norm_gain, w_in, w_pool, pool_scale, q_norm_gain, k_norm_gain,
           attn_sinks, w_out, ffn_norm_gain, w_group_router, w_expert_router, w_gate, w_up, w_down):
    B, S, D = x.shape
    assert D == D_MODEL and S % TQ_IN == 0 and S % TQ_ATT == 0 and (B * S) % TM_MOE == 0
    assert w_in.shape[0] == 1, "single layer"
    T = B * S

    rope = _rope_tables(N_META + S)
    fgain = ffn_norm_gain[0][:, None]
    qgain2 = jnp.tile(q_norm_gain[0] * LOG2_E, 2)[None, :]
    kgain2 = jnp.tile(k_norm_gain[0] * (HEAD_DIM ** 0.5), 2)[None, :]
    pscale = pool_scale[0][None, :]
    w_in_b = (attn_norm_gain[0][:, None] * w_in[0]).astype(BF16)
    w_pool_b = w_pool[0].astype(BF16)
    w_out_b = w_out[0].astype(BF16)
    w_r = (fgain * jnp.concatenate(
        [w_expert_router[0], w_group_router[0],
         jnp.zeros((D, LANES - N_EXPERTS - N_GROUPS), F32)], axis=1)).astype(BF16)
    params = pltpu.CompilerParams(vmem_limit_bytes=VMEM_LIMIT)

    u_meta, kd_meta, v_meta = pl.pallas_call(
        _meta_kernel,
        out_shape=(jax.ShapeDtypeStruct((N_META, POOL_WIDTH), F32),
                   jax.ShapeDtypeStruct((N_META, KV_EXP), BF16),
                   jax.ShapeDtypeStruct((N_META, KV_WIDTH), BF16)),
        compiler_params=params,
        name="meta_proj",
    )(meta_tokens, w_in_b, kgain2, rope[:N_META])
    vt_meta = v_meta.T

    yp, q, kd, vt = pl.pallas_call(
        _in_kernel,
        grid=(B, S // TQ_IN),
        in_specs=[
            pl.BlockSpec((1, TQ_IN, D), lambda b, j: (b, j, 0)),
            _const_spec((N_META, POOL_WIDTH)),
            _const_spec((D, IN_WIDTH)),
            _const_spec((len(POOL_WINDOWS), POOL_GROUP, POOL_GROUP)),
            _const_spec((1, POOL_WIDTH)),
            _const_spec((1, LANES)),
            _const_spec((1, LANES)),
            pl.BlockSpec((TQ_IN, 3 * LANES), lambda b, j: (j, 0)),
        ],
        out_specs=(
            pl.BlockSpec((1, TQ_IN, POOL_WIDTH), lambda b, j: (b, j, 0)),
            pl.BlockSpec((1, TQ_IN, ATTN_WIDTH), lambda b, j: (b, j, 0)),
            pl.BlockSpec((1, TQ_IN, KV_EXP), lambda b, j: (b, j, 0)),
            pl.BlockSpec((1, KV_WIDTH, TQ_IN), lambda b, j: (b, 0, j)),
        ),
        out_shape=(jax.ShapeDtypeStruct((B, S, POOL_WIDTH), BF16),
                   jax.ShapeDtypeStruct((B, S, ATTN_WIDTH), BF16),
                   jax.ShapeDtypeStruct((B, S, KV_EXP), BF16),
                   jax.ShapeDtypeStruct((B, KV_WIDTH, S), BF16)),
        scratch_shapes=[pltpu.VMEM((N_META, POOL_WIDTH), F32),
                        pltpu.VMEM((TQ_IN, IN_WIDTH), F32)],
        compiler_params=pltpu.CompilerParams(
            dimension_semantics=("arbitrary", "arbitrary"), vmem_limit_bytes=VMEM_LIMIT),
        name="in_proj",
    )(x, u_meta, w_in_b, w_pool_b, pscale, qgain2, kgain2, rope[N_META:])

    upper = jnp.asarray(np.triu(np.ones((ROUTE_ROWS, ROUTE_ROWS), np.float32), 1), dtype=BF16)
    per_row = S // TQ_ATT
    n_tiles = B * per_row

    def cur(s):
        t = jnp.minimum(s, n_tiles - 1)
        return t // per_row, t % per_row

    def lag(s):
        t = jnp.maximum(s - 1, 0)
        return t // per_row, t % per_row

    def lag_rows(s):
        return (*lag(s), 0)

    def lag_flat(s):
        return (jnp.maximum(s - 1, 0), 0, 0)

    w_steps = N_EXPERTS * W_SLICES
    assert n_tiles >= w_steps and D % W_SLICES == 0 and D_EXPERT % W_SLICES == 0

    def w_slice(s):
        k = jnp.minimum(s, w_steps - 1)
        return k // W_SLICES, k % W_SLICES, 0

    h2, m, table, rows, cnt, wg, wu, wd = pl.pallas_call(
        functools.partial(_attn_kernel, n_tiles, per_row),
        grid=(n_tiles + 1,),
        in_specs=[
            pl.BlockSpec(memory_space=pltpu.SMEM),
            pl.BlockSpec((1, TQ_ATT, D), lag_rows),
            pl.BlockSpec((1, TQ_ATT, POOL_WIDTH), lag_rows),
            pl.BlockSpec((1, TQ_ATT, ATTN_WIDTH), lambda s: (*cur(s), 0)),
            pl.BlockSpec((1, S, KV_EXP), lambda s: (cur(s)[0], 0, 0)),
            pl.BlockSpec((1, KV_WIDTH, S), lambda s: (cur(s)[0], 0, 0)),
            _const_spec((N_META, KV_EXP)),
            _const_spec((KV_WIDTH, N_META)),
            _const_spec((D, D)),
            _const_spec((D, LANES)),
            _const_spec((ROUTE_ROWS, ROUTE_ROWS)),
            pl.BlockSpec((D // W_SLICES, 1), lambda s: (w_slice(s)[1], 0)),
            pl.BlockSpec((1, D // W_SLICES, D_EXPERT), w_slice),
            pl.BlockSpec((1, D // W_SLICES, D_EXPERT), w_slice),
            pl.BlockSpec((1, D_EXPERT // W_SLICES, D), w_slice),
        ],
        out_specs=(
            pl.BlockSpec((1, TQ_ATT, D), lag_rows),
            pl.BlockSpec((1, TQ_ATT, D), lag_rows),
            pl.BlockSpec((1, TQ_ATT, LANES), lag_rows),
            pl.BlockSpec((TQ_ATT // ROUTE_ROWS, SUBLANES, ROUTE_ROWS), lag_flat),
            pl.BlockSpec((TQ_ATT // ROUTE_ROWS, SUBLANES, LANES), lag_flat),
            pl.BlockSpec((1, D // W_SLICES, D_EXPERT), w_slice),
            pl.BlockSpec((1, D // W_SLICES, D_EXPERT), w_slice),
            pl.BlockSpec((1, D_EXPERT // W_SLICES, D), w_slice),
        ),
        out_shape=(jax.ShapeDtypeStruct((B, S, D), F32),
                   jax.ShapeDtypeStruct((B, S, D), BF16),
                   jax.ShapeDtypeStruct((B, S, LANES), BF16),
                   jax.ShapeDtypeStruct((T // ROUTE_ROWS, SUBLANES, ROUTE_ROWS), F32),
                   jax.ShapeDtypeStruct((T // ROUTE_ROWS, SUBLANES, LANES), jnp.int32),
                   jax.ShapeDtypeStruct((N_EXPERTS, D, D_EXPERT), BF16),
                   jax.ShapeDtypeStruct((N_EXPERTS, D, D_EXPERT), BF16),
                   jax.ShapeDtypeStruct((N_EXPERTS, D_EXPERT, D), BF16)),
        scratch_shapes=[
            pltpu.VMEM((2, TQ_ATT, ATTN_WIDTH), BF16),
            pltpu.VMEM((TQ_ATT // WINDOW, N_KV_HEADS, KEY_ROWS, LANES), BF16),
            pltpu.VMEM((TQ_ATT // WINDOW, N_KV_HEADS, LANES, KEY_PAD), BF16),
            pltpu.VMEM((N_ITEMS, KEY_ROWS, 2 * LANES), F32),
            pltpu.VMEM((N_ITEMS, KEY_PAD, 2 * LANES), BF16),
            pltpu.VMEM((N_ITEMS, LANES, 2 * LANES), F32),
            pltpu.VMEM((TQ_ATT // ROUTE_ROWS, ROUTE_ROWS, LANES), F32),
        ],
        compiler_params=pltpu.CompilerParams(
            dimension_semantics=("arbitrary",), vmem_limit_bytes=VMEM_LIMIT),
        name="attn_out",
    )(attn_sinks[0] * LOG2_E, x, yp, q, kd, vt, kd_meta, vt_meta, w_out_b, w_r, upper,
      fgain, w_gate[0], w_up[0], w_down[0])

    cnt_flat = cnt[:, :N_GROUPS, 0].reshape(-1)
    n_moe = T // TM_MOE
    out = pl.pallas_call(
        functools.partial(_moe_kernel, n_moe),
        grid_spec=pltpu.PrefetchScalarGridSpec(
            num_scalar_prefetch=1,
            grid=(n_moe,),
            in_specs=[
                pl.BlockSpec((TM_MOE, D), lambda i, c: (i, 0)),
                pl.BlockSpec((TM_MOE, LANES), lambda i, c: (i, 0)),
                pl.BlockSpec((TM_MOE // ROUTE_ROWS, SUBLANES, ROUTE_ROWS), lambda i, c: (i, 0, 0)),
                pl.BlockSpec((TM_MOE // ROUTE_ROWS, SUBLANES, ROUTE_ROWS),
                             lambda i, c: (jnp.minimum(i + 1, n_moe - 1), 0, 0)),
                pl.BlockSpec((TM_MOE, D), lambda i, c: (i, 0)),
                _resident_spec((N_EXPERTS, D, D_EXPERT)),
                _resident_spec((N_EXPERTS, D, D_EXPERT)),
                _resident_spec((N_EXPERTS * D_EXPERT, D)),
            ],
            out_specs=pl.BlockSpec((TM_MOE, D), lambda i, c: (i, 0)),
            scratch_shapes=[pltpu.VMEM((YS_ROWS, D), BF16),
                            pltpu.VMEM((2, max(SUB, *PASS_SIZES), D_GROUP), BF16),
                            pltpu.SMEM((2, 3 * N_GROUPS + 1), jnp.int32),
                            pltpu.VMEM((2, SUBLANES, TM_MOE), F32),
                            pltpu.VMEM((2, TM_MOE, LANES), F32)],
        ),
        out_shape=jax.ShapeDtypeStruct((T, D), F32),
        compiler_params=pltpu.CompilerParams(
            dimension_semantics=("arbitrary",), vmem_limit_bytes=VMEM_LIMIT),
        name="moe",
    )(cnt_flat, m.reshape(T, D), table.reshape(T, LANES), rows, rows, h2.reshape(T, D), wg, wu,
      wd.reshape(N_EXPERTS * D_EXPERT, D))
    return out.reshape(B, S, D)
```

```python
import functools

import numpy as np
import jax
import jax.numpy as jnp
from jax import lax
from jax.experimental import pallas as pl
from jax.experimental.pallas import tpu as pltpu

D_MODEL = 1024
N_META = 16
POOL_WIDTH = 512
POOL_WINDOWS = (2, 4, 8, 16)
POOL_GROUP = 128
HEAD_DIM = 64
N_HEADS = 8
N_KV_HEADS = 2
ATTN_WIDTH = N_HEADS * HEAD_DIM
KV_WIDTH = N_KV_HEADS * HEAD_DIM
WINDOW = 128
ROT_DIM = HEAD_DIM // 4
ROPE_THETA = 500000.0
IN_WIDTH = POOL_WIDTH + ATTN_WIDTH + 2 * KV_WIDTH
N_GROUPS = 4
EXPERTS_PER_GROUP = 4
N_EXPERTS = 16
D_EXPERT = 256
EPS = 1e-6
NEG_INF = -1e30
LOG2_E = 1.4426950408889634

LANES = 128
SUBLANES = 8
KV_EXP = 4 * LANES
TQ_IN = 2048
RB_IN = 256
FIN_BLOCKS = 3
TQ_ATT = 1024
KEY_ROWS = 2 * (2 * WINDOW + N_META)
KEY_PAD = -(-KEY_ROWS // LANES) * LANES
OUT_BLOCKS = 2
ROUTE_ROWS = OUT_BLOCKS * WINDOW
W_SLICES = 2
STAGE_LAGS = (0, 1, 2)
ATT_GROUP = 4
N_ITEMS = (TQ_ATT // WINDOW) * N_KV_HEADS
TM_MOE = 512
SUB = 144
PASS_SIZES = (128, 144, 160)
BF16_ROWS = 16
MXU_DEPTH = 256
GID_LANE = N_EXPERTS
RANK_LANE = 24
LO_SHIFT = 32
D_GROUP = EXPERTS_PER_GROUP * D_EXPERT
VMEM_LIMIT = 56 * 1024 * 1024

BF16 = jnp.bfloat16
F32 = jnp.float32


def _rope_tables(n_pos):
    half = ROT_DIM // 2
    inv_freq = 1.0 / (ROPE_THETA ** (np.arange(half, dtype=np.float64) / half))
    ang = np.arange(n_pos, dtype=np.float64)[:, None] * inv_freq[None, :]
    cos, sin = np.cos(ang), np.sin(ang)
    c = np.ones((n_pos, HEAD_DIM)); c[:, :half] = cos; c[:, half:ROT_DIM] = cos
    sa = np.zeros((n_pos, HEAD_DIM)); sa[:, :half] = -sin
    sb = np.zeros((n_pos, HEAD_DIM)); sb[:, half:ROT_DIM] = sin
    tab = np.concatenate([np.tile(c, (1, 2)), np.tile(sa, (1, 2)), np.tile(sb, (1, 2))], axis=1)
    return jnp.asarray(tab, dtype=F32)


def _rms_unit(x):
    ms = jnp.mean(x * x, axis=-1, keepdims=True)
    return x * lax.rsqrt(ms + EPS)


def _head_norm_rope_many(xs, gains, ropes):
    los = [lax.broadcasted_iota(jnp.int32, x.shape, 1) < HEAD_DIM for x in xs]
    sums = []
    for x, lo in zip(xs, los):
        sq = x * x
        sums.append((jnp.sum(jnp.where(lo, sq, 0.0), axis=-1, keepdims=True),
                     jnp.sum(jnp.where(lo, 0.0, sq), axis=-1, keepdims=True)))
    ys = [x * lax.rsqrt(jnp.where(lo, s_lo, s_hi) + HEAD_DIM * EPS) * g
          for x, g, lo, (s_lo, s_hi) in zip(xs, gains, los, sums)]
    half = ROT_DIM // 2
    rolled = [(pltpu.roll(y, LANES - half, 1), pltpu.roll(y, half, 1)) for y in ys]
    return [y * rope[:, 0:LANES] + ra * rope[:, LANES:2 * LANES] + rb * rope[:, 2 * LANES:3 * LANES]
            for y, (ra, rb), rope in zip(ys, rolled, ropes)]


def _head_norm_rope(xc, gain2, rope):
    return _head_norm_rope_many([xc], [gain2], [rope])[0]


def _expand_kv(t):
    lo = lax.broadcasted_iota(jnp.int32, t.shape, 1) < HEAD_DIM
    sw = pltpu.roll(t, HEAD_DIM, 1)
    z = jnp.zeros_like(t)
    return jnp.concatenate([jnp.where(lo, t, z), jnp.where(lo, z, sw),
                            jnp.where(lo, sw, z), jnp.where(lo, z, t)], axis=1)


def _project(x, w_in):
    return jnp.dot(_rms_unit(x).astype(BF16), w_in, preferred_element_type=F32)


def _meta_kernel(meta_ref, win_ref, kgain_ref, rope_ref, u_ref, kd_ref, v_ref):
    proj = _project(meta_ref[...], win_ref[...])
    u_ref[...] = proj[:, :POOL_WIDTH]
    k = proj[:, POOL_WIDTH + ATTN_WIDTH:POOL_WIDTH + ATTN_WIDTH + KV_WIDTH]
    v = proj[:, POOL_WIDTH + ATTN_WIDTH + KV_WIDTH:]
    k = _head_norm_rope(k, kgain_ref[...], rope_ref[...])
    kd_ref[...] = _expand_kv(k).astype(BF16)
    v_ref[...] = v.astype(BF16)


def _in_kernel(x_ref, umeta_ref, win_ref, wpool_ref, pscale_ref, qgain_ref, kgain_ref,
               rope_ref, yp_ref, q_ref, kd_ref, vt_ref, carry_ref, proj_ref):
    j = pl.program_id(1)

    @pl.when(j == 0)
    def _():
        carry_ref[...] = umeta_ref[...]

    sizes = [RB_IN] * (TQ_IN // RB_IN)
    blocks = [slice(sum(sizes[:i]), sum(sizes[:i + 1])) for i in range(len(sizes))]
    n_sub = len(blocks)

    def normalise(i):
        return _rms_unit(x_ref[0, blocks[i], :]).astype(BF16)

    def project(i, a):
        proj_ref[blocks[i], :] = jnp.dot(a, win_ref[...], preferred_element_type=F32)

    def pool(i):
        rows = blocks[i]
        u = proj_ref[rows, :POOL_WIDTH]
        prev = carry_ref[...] if i == 0 else proj_ref[rows.start - N_META:rows.start, :POOL_WIDTH]
        acc = jnp.concatenate([prev, u], axis=0)
        for gi, w in enumerate(POOL_WINDOWS):
            lo = gi * POOL_GROUP
            acc = acc[:, POOL_GROUP * (1 if gi else 0):]
            acc = acc + pltpu.roll(acc, w // 2, 0)
            mixed = acc[N_META:, :POOL_GROUP] * (1.0 / w) - u[:, lo:lo + POOL_GROUP]
            y = jnp.dot(mixed.astype(BF16), wpool_ref[gi], preferred_element_type=F32)
            yp_ref[0, rows, lo:lo + POOL_GROUP] = (y * pscale_ref[:, lo:lo + POOL_GROUP]).astype(BF16)

    n_qc = ATTN_WIDTH // LANES

    def finish(ids):
        for i in ids:
            pool(i)
        xs, gains, ropes = [], [], []
        for i in ids:
            rows = blocks[i]
            xs += [proj_ref[rows, POOL_WIDTH + c * LANES:POOL_WIDTH + (c + 1) * LANES] for c in range(n_qc + 1)]
            gains += [qgain_ref[...]] * n_qc + [kgain_ref[...]]
            ropes += [rope_ref[rows, :]] * (n_qc + 1)
        outs = _head_norm_rope_many(xs, gains, ropes)
        for k, i in enumerate(ids):
            rows = blocks[i]
            mine = outs[k * (n_qc + 1):(k + 1) * (n_qc + 1)]
            for c in range(n_qc):
                q_ref[0, rows, c * LANES:(c + 1) * LANES] = mine[c].astype(BF16)
            kd_ref[0, rows, :] = _expand_kv(mine[n_qc]).astype(BF16)
            v = proj_ref[rows, POOL_WIDTH + ATTN_WIDTH + KV_WIDTH:]
            vt_ref[0, :, rows] = jnp.transpose(v).astype(BF16)

    a_next = normalise(0)
    pending = []
    for i in range(n_sub):
        project(i, a_next)
        if i + 1 < n_sub:
            a_next = normalise(i + 1)
        pending.append(i)
        if len(pending) > FIN_BLOCKS:
            finish(pending[:FIN_BLOCKS])
            pending = pending[FIN_BLOCKS:]
    finish(pending)
    carry_ref[...] = proj_ref[TQ_IN - N_META:, :POOL_WIDTH]


def _route(logits, upper):
    n = logits.shape[0]
    lt = jnp.transpose(logits)
    sub = SUBLANES
    row8 = lax.broadcasted_iota(jnp.int32, (sub, n), 0).astype(F32)
    row16 = lax.broadcasted_iota(jnp.int32, (N_EXPERTS, n), 0).astype(F32)
    g_ok = row8 < N_GROUPS
    gl = jnp.where(g_ok, lt[N_EXPERTS:N_EXPERTS + sub, :], NEG_INF)
    gmax = jnp.max(gl, axis=0, keepdims=True)
    gsum = jnp.sum(jnp.where(g_ok, jnp.exp(gl - gmax), 0.0), axis=0, keepdims=True)
    g_prob = 1.0 / gsum
    g_idx = jnp.min(jnp.where(gl == gmax, row8, float(sub)), axis=0, keepdims=True)
    e_lo = g_idx * EXPERTS_PER_GROUP
    emask = (row16 >= e_lo) & (row16 < e_lo + EXPERTS_PER_GROUP)
    el = jnp.where(emask, lt[:N_EXPERTS, :], NEG_INF)
    big = float(N_EXPERTS)
    e1 = jnp.max(el, axis=0, keepdims=True)
    i1 = jnp.min(jnp.where(el == e1, row16, big), axis=0, keepdims=True)
    el2 = jnp.where(row16 == i1, NEG_INF, el)
    e2 = jnp.max(el2, axis=0, keepdims=True)
    i2 = jnp.min(jnp.where(el2 == e2, row16, big), axis=0, keepdims=True)
    t = jnp.exp(e2 - e1)
    w1 = 1.0 / (1.0 + t)
    w2 = t * w1
    gates_t = jnp.where(row16 == i1, w1 * g_prob, 0.0) + jnp.where(row16 == i2, w2 * g_prob, 0.0)
    onehot = jnp.where(row8 == g_idx, 1.0, 0.0)
    earlier = jnp.dot(onehot.astype(BF16), upper, preferred_element_type=F32)
    rank = jnp.sum(onehot * earlier, axis=0, keepdims=True)
    hi = gates_t.astype(BF16).astype(F32)
    lo = (gates_t - hi).astype(BF16).astype(F32)
    gid8 = jnp.broadcast_to(g_idx, (sub, n))
    rank8 = jnp.broadcast_to(rank, (sub, n))
    assert GID_LANE == N_EXPERTS and RANK_LANE == GID_LANE + sub and LO_SHIFT == RANK_LANE + sub
    full_t = jnp.concatenate(
        [hi, gid8, rank8, lo, jnp.zeros((LANES - LO_SHIFT - N_EXPERTS, n), F32)], axis=0)
    rows = jnp.concatenate([g_idx, rank, jnp.zeros((sub - 2, n), F32)], axis=0)
    counts = jnp.sum(onehot, axis=1, keepdims=True)
    return (jnp.transpose(full_t).astype(BF16), rows,
            jnp.broadcast_to(counts, (sub, LANES)).astype(jnp.int32))


def _fold_rows(x, pair, final):
    rows = x.shape[0]
    while rows > SUBLANES:
        rows //= 2
        x = pair(x[:rows], x[rows:])
    return final(x, axis=0, keepdims=True)


def _attn_kernel(n_tiles, tiles_per_row, sink_ref, x_ref, yp_ref, q_ref, kd_ref, vt_ref, kmeta_ref, vtmeta_ref, wout_ref,
                 wr_ref, upper_ref, fg_ref, wg32_ref, wu32_ref, wd32_ref,
                 h2_ref, m_ref, table_ref, rows_ref, cnt_ref, wg_ref, wu_ref, wd_ref,
                 yattn_ref, kcat_ref, vbd_ref, s_ref, p_ref, inv_ref, logit_ref):
    step = pl.program_id(0)
    tile = jnp.minimum(step, n_tiles - 1)
    j = lax.rem(tile, tiles_per_row)
    slot = lax.rem(step, 2)
    nt = (((1,), (1,)), ((), ()))
    n_r = TQ_ATT // WINDOW
    band = 2 * WINDOW
    m0 = 2 * band

    @pl.when(step == 0)
    def _():
        vbd_ref[...] = jnp.zeros(vbd_ref.shape, BF16)
        p_ref[...] = jnp.zeros(p_ref.shape, BF16)
        yattn_ref[...] = jnp.zeros(yattn_ref.shape, BF16)

    wg_ref[0] = (wg32_ref[0] * fg_ref[...]).astype(BF16)
    wu_ref[0] = (wu32_ref[0] * fg_ref[...]).astype(BF16)
    wd_ref[0] = wd32_ref[0].astype(BF16)

    def window_start(r):
        return pl.multiple_of(jnp.maximum(j * TQ_ATT + (r - 1) * WINDOW, 0), WINDOW)

    def stage_scores(r):
        start = window_start(r)
        for g in range(N_KV_HEADS):
            vb = vt_ref[0, g * HEAD_DIM:(g + 1) * HEAD_DIM, pl.ds(start, band)]
            vm = vtmeta_ref[g * HEAD_DIM:(g + 1) * HEAD_DIM, :]
            for hh in range(2):
                col = (2 * g + hh) * LANES
                kcat_ref[r, g, hh * band:(hh + 1) * band, :] = kd_ref[0, pl.ds(start, band), col:col + LANES]
                kcat_ref[r, g, m0 + hh * N_META:m0 + (hh + 1) * N_META, :] = kmeta_ref[:, col:col + LANES]
                rows = slice(hh * HEAD_DIM, (hh + 1) * HEAD_DIM)
                vbd_ref[r, g, rows, hh * band:(hh + 1) * band] = vb
                vbd_ref[r, g, rows, m0 + hh * N_META:m0 + (hh + 1) * N_META] = vm
            rq = slice(r * WINDOW, (r + 1) * WINDOW)
            qq = jnp.concatenate([q_ref[0, rq, (2 * g + ch) * LANES:(2 * g + ch + 1) * LANES]
                                  for ch in range(2)], axis=0)
            s_ref[r * N_KV_HEADS + g] = lax.dot_general(kcat_ref[r, g], qq, nt, preferred_element_type=F32)

    def stage_softmax(r):
        kpos = window_start(r) + lax.broadcasted_iota(jnp.int32, (band, WINDOW), 0)
        qpos = j * TQ_ATT + r * WINDOW + lax.broadcasted_iota(jnp.int32, (band, WINDOW), 1)
        d = qpos - kpos
        bias = jnp.where((d >= 0) & (d < WINDOW), 0.0, NEG_INF)
        bias = jnp.concatenate([bias, bias], axis=1)
        left = lax.broadcasted_iota(jnp.int32, (1, 2 * WINDOW), 1) < WINDOW
        for g in range(N_KV_HEADS):
            it = r * N_KV_HEADS + g
            for hh in range(2):
                sink = jnp.where(left, sink_ref[4 * g + hh], sink_ref[4 * g + 2 + hh])
                s = s_ref[it, hh * band:(hh + 1) * band, :] + bias
                sm = s_ref[it, m0 + hh * N_META:m0 + (hh + 1) * N_META, :]
                mx = jnp.maximum(_fold_rows(s, jnp.maximum, jnp.max), jnp.max(sm, axis=0, keepdims=True))
                mx = jnp.maximum(mx, sink)
                p = jnp.exp2(s - mx)
                pm = jnp.exp2(sm - mx)
                den = (_fold_rows(p, jnp.add, jnp.sum) + jnp.sum(pm, axis=0, keepdims=True)
                       + jnp.exp2(sink - mx))
                p_ref[it, hh * band:(hh + 1) * band, :] = p.astype(BF16)
                p_ref[it, m0 + hh * N_META:m0 + (hh + 1) * N_META, :] = pm.astype(BF16)
                inv_ref[it, hh * HEAD_DIM:(hh + 1) * HEAD_DIM, :] = jnp.broadcast_to(
                    1.0 / den, (HEAD_DIM, 2 * WINDOW))

    def stage_values(r):
        for g in range(N_KV_HEADS):
            it = r * N_KV_HEADS + g
            o_t = jnp.dot(vbd_ref[r, g], p_ref[it], preferred_element_type=F32) * inv_ref[it]
            for ch in range(2):
                c = 2 * g + ch
                yattn_ref[slot, r * WINDOW:(r + 1) * WINDOW, c * LANES:(c + 1) * LANES] = (
                    jnp.transpose(o_t[:, ch * WINDOW:(ch + 1) * WINDOW]).astype(BF16))

    def stage_out():
        mix = (jnp.dot(yp_ref[0], wout_ref[:POOL_WIDTH, :], preferred_element_type=F32)
               + jnp.dot(yattn_ref[1 - slot], wout_ref[POOL_WIDTH:, :], preferred_element_type=F32))
        h2 = x_ref[0] + mix
        h2_ref[0] = h2
        m = _rms_unit(h2).astype(BF16)
        m_ref[0] = m
        logits = jnp.dot(m, wr_ref[...], preferred_element_type=F32)
        for p in range(TQ_ATT // ROUTE_ROWS):
            logit_ref[p] = logits[p * ROUTE_ROWS:(p + 1) * ROUTE_ROWS, :]

    def stage_route(p):
        table, rows, counts = _route(logit_ref[p], upper_ref[...])
        table_ref[0, p * ROUTE_ROWS:(p + 1) * ROUTE_ROWS, :] = table
        rows_ref[p] = rows
        cnt_ref[p] = counts

    n_p = TQ_ATT // ROUTE_ROWS
    n_grp = n_r // ATT_GROUP
    for t in range(max(n_grp + STAGE_LAGS[-1], n_p + 1)):
        for k, stage in zip(STAGE_LAGS, (stage_scores, stage_softmax, stage_values)):
            if 0 <= t - k < n_grp:
                for r in range((t - k) * ATT_GROUP, (t - k + 1) * ATT_GROUP):
                    stage(r)
        if t == 0:
            stage_out()
        if 1 <= t <= n_p:
            stage_route(t - 1)


def _pair_plan(na, nb):
    top = PASS_SIZES[-1]
    if isinstance(na, int) and isinstance(nb, int):
        if 1 <= na <= top and 1 <= nb <= top:
            size = next(s for s in PASS_SIZES if max(na, nb) <= s)
            return 1, size, size, size
        return 0, SUB, -(-na // SUB) * SUB, -(-nb // SUB) * SUB
    fast = (na >= 1) & (na <= top) & (nb >= 1) & (nb <= top)
    size = jnp.int32(top)
    for s in reversed(PASS_SIZES[:-1]):
        size = jnp.where(jnp.maximum(na, nb) <= s, s, size)
    slow = [lax.div(n + (SUB - 1), SUB) * SUB for n in (na, nb)]
    return (fast.astype(jnp.int32), size,
            jnp.where(fast, size, slow[0]), jnp.where(fast, size, slow[1]))


def _max_sorted_rows(n_tokens):
    pair = [max(sum(_pair_plan(na, t - na)[2:]) for na in range(t + 1)) for t in range(n_tokens + 1)]
    return max(pair[t] + pair[n_tokens - t] for t in range(n_tokens + 1))


def _moe_kernel(n_tiles, cnt_ref, m_ref, table_ref, rows_ref, rows_next_ref, h2_ref, wg_ref, wu_ref, wd_ref,
                out_ref, ys_ref, hid_ref, plan_ref, posr_ref, posc_ref):
    i = pl.program_id(0)
    slot = lax.rem(i, 2)
    per = TM_MOE // ROUTE_ROWS

    def prepare(tile, dst, rows_blk):
        counts = [[cnt_ref[(tile * per + h) * N_GROUPS + g] for g in range(N_GROUPS)] for h in range(per)]
        n_tok = [sum(counts[h][g] for h in range(per)) for g in range(N_GROUPS)]
        base = []
        total_rows = 0
        for p in range(N_GROUPS // 2):
            fast, size, rows_a, rows_b = _pair_plan(n_tok[2 * p], n_tok[2 * p + 1])
            plan_ref[dst, 2 * N_GROUPS + p] = fast
            plan_ref[dst, 2 * N_GROUPS + N_GROUPS // 2 + p] = size
            for g, rows in ((2 * p, rows_a), (2 * p + 1, rows_b)):
                base.append(total_rows)
                plan_ref[dst, g] = n_tok[g]
                plan_ref[dst, N_GROUPS + g] = total_rows + jnp.int32(0)
                total_rows = total_rows + rows
        plan_ref[dst, 3 * N_GROUPS] = total_rows
        first = [[base[g] + sum(counts[hh][g] for hh in range(h)) for g in range(N_GROUPS)]
                 for h in range(per)]

        def pick(gid, offsets):
            out = _as_f32(offsets[N_GROUPS - 1])
            for g in range(N_GROUPS - 2, -1, -1):
                out = jnp.where(gid == g, _as_f32(offsets[g]), out)
            return out

        pos = jnp.concatenate(
            [rows_blk[h, 1:2, :] + pick(rows_blk[h, 0:1, :], first[h]) for h in range(per)], axis=1)
        posr_ref[dst] = jnp.broadcast_to(pos, (SUBLANES, TM_MOE))
        posc_ref[dst] = jnp.transpose(jnp.broadcast_to(pos, (LANES, TM_MOE)))

    @pl.when(i == 0)
    def _():
        ys_ref[...] = jnp.zeros(ys_ref.shape, BF16)
        prepare(0, 0, rows_ref)

    table = table_ref[...]
    pos_row = posr_ref[slot, 0:1, :]
    pos_col = posc_ref[slot, :, 0:1]
    total_rows = plan_ref[slot, 3 * N_GROUPS]

    def gather(g, r0, size):
        rows = (r0 + lax.broadcasted_iota(jnp.int32, (size, 1), 0)).astype(F32)
        perm = jnp.where(pos_row == rows, 1.0, 0.0).astype(BF16)
        xs = jnp.dot(perm, m_ref[...], preferred_element_type=F32).astype(BF16)
        gs = jnp.dot(perm, table, preferred_element_type=F32)
        return xs, gs

    def experts_up(g, xs, gs, size, slot):
        lane = lax.broadcasted_iota(jnp.int32, gs.shape, 1)
        for jj in range(EXPERTS_PER_GROUP):
            e = g * EXPERTS_PER_GROUP + jj
            gt = jnp.dot(xs, wg_ref[e], preferred_element_type=F32)
            up = jnp.dot(xs, wu_ref[e], preferred_element_type=F32)
            gate = jnp.sum(jnp.where((lane == e) | (lane == e + LO_SHIFT), gs, 0.0),
                           axis=-1, keepdims=True)
            hid = gt * (1.0 / (1.0 + jnp.exp(-gt))) * up * gate
            hid_ref[slot, :size, jj * D_EXPERT:(jj + 1) * D_EXPERT] = hid.astype(BF16)

    def experts_down(g, r0, size, slot):
        w_down = wd_ref[pl.ds(pl.multiple_of(g * D_GROUP, D_GROUP), D_GROUP), :]
        y = jnp.dot(hid_ref[slot, :size, :], w_down, preferred_element_type=F32)
        ys_ref[pl.ds(r0, size), :] = y.astype(BF16)

    def run_passes(jobs, size):
        got = [gather(g, r0, size) for g, r0 in jobs]
        for slot, ((g, _), (xs, gs)) in enumerate(zip(jobs, got)):
            experts_up(g, xs, gs, size, slot)
        for slot, (g, r0) in enumerate(jobs):
            experts_down(g, r0, size, slot)

    def run_pair(p, carry):
        groups = (2 * p, 2 * p + 1)
        firsts = [pl.multiple_of(plan_ref[slot, N_GROUPS + g], BF16_ROWS) for g in groups]
        fast = plan_ref[slot, 2 * N_GROUPS + p]
        size_p = plan_ref[slot, 2 * N_GROUPS + N_GROUPS // 2 + p]
        for size in PASS_SIZES:
            @pl.when((fast == 1) & (size_p == size))
            def _(size=size):
                run_passes(list(zip(groups, firsts)), size)

        @pl.when(fast == 0)
        def _():
            for g, r0 in zip(groups, firsts):
                def body(c, inner, g=g, r0=r0):
                    run_passes([(g, pl.multiple_of(r0 + c * SUB, BF16_ROWS))], SUB)
                    return inner
                lax.fori_loop(0, lax.div(plan_ref[slot, g] + (SUB - 1), SUB), body, 0)
        return carry

    lax.fori_loop(0, N_GROUPS // 2, run_pair, 0)

    def scatter(n_rows):
        prepare(jnp.minimum(i + 1, n_tiles - 1), 1 - slot, rows_next_ref)
        cols = lax.broadcasted_iota(jnp.int32, (1, n_rows), 1).astype(F32)
        perm_t = jnp.where(pos_col == cols, 1.0, 0.0).astype(BF16)
        out_ref[...] = h2_ref[...] + jnp.dot(perm_t, ys_ref[:n_rows, :], preferred_element_type=F32)

    short = total_rows <= YS_SHORT

    @pl.when(short)
    def _():
        scatter(YS_SHORT)

    @pl.when(jnp.logical_not(short))
    def _():
        scatter(YS_ROWS)


YS_ROWS = -(-_max_sorted_rows(TM_MOE) // MXU_DEPTH) * MXU_DEPTH
YS_SHORT = YS_ROWS - MXU_DEPTH


def _as_f32(v):
    return float(v) if isinstance(v, int) else v.astype(F32)


def _const_spec(shape):
    n = len(shape)
    return pl.BlockSpec(shape, lambda *_: (0,) * n)


def _resident_spec(shape):
    n = len(shape)
    return pl.BlockSpec(shape, lambda *_: (0,) * n, pipeline_mode=pl.Buffered(1))


def kernel(x, meta_tokens, attn_norm_gain, w_in, w_pool, pool_scale, q_norm_gain, k_norm_gain,
           attn_sinks, w_out, ffn_norm_gain, w_group_router, w_expert_router, w_gate, w_up, w_down):
    B, S, D = x.shape
    assert D == D_MODEL and S % TQ_IN == 0 and S % TQ_ATT == 0 and (B * S) % TM_MOE == 0
    assert w_in.shape[0] == 1, "single layer"
    T = B * S

    rope = _rope_tables(N_META + S)
    fgain = ffn_norm_gain[0][:, None]
    qgain2 = jnp.tile(q_norm_gain[0] * LOG2_E, 2)[None, :]
    kgain2 = jnp.tile(k_norm_gain[0] * (HEAD_DIM ** 0.5), 2)[None, :]
    pscale = pool_scale[0][None, :]
    w_in_b = (attn_norm_gain[0][:, None] * w_in[0]).astype(BF16)
    w_pool_b = w_pool[0].astype(BF16)
    w_out_b = w_out[0].astype(BF16)
    w_r = (fgain * jnp.concatenate(
        [w_expert_router[0], w_group_router[0],
         jnp.zeros((D, LANES - N_EXPERTS - N_GROUPS), F32)], axis=1)).astype(BF16)
    params = pltpu.CompilerParams(vmem_limit_bytes=VMEM_LIMIT)

    u_meta, kd_meta, v_meta = pl.pallas_call(
        _meta_kernel,
        out_shape=(jax.ShapeDtypeStruct((N_META, POOL_WIDTH), F32),
                   jax.ShapeDtypeStruct((N_META, KV_EXP), BF16),
                   jax.ShapeDtypeStruct((N_META, KV_WIDTH), BF16)),
        compiler_params=params,
        name="meta_proj",
    )(meta_tokens, w_in_b, kgain2, rope[:N_META])
    vt_meta = v_meta.T

    yp, q, kd, vt = pl.pallas_call(
        _in_kernel,
        grid=(B, S // TQ_IN),
        in_specs=[
            pl.BlockSpec((1, TQ_IN, D), lambda b, j: (b, j, 0)),
            _const_spec((N_META, POOL_WIDTH)),
            _const_spec((D, IN_WIDTH)),
            _const_spec((len(POOL_WINDOWS), POOL_GROUP, POOL_GROUP)),
            _const_spec((1, POOL_WIDTH)),
            _const_spec((1, LANES)),
            _const_spec((1, LANES)),
            pl.BlockSpec((TQ_IN, 3 * LANES), lambda b, j: (j, 0)),
        ],
        out_specs=(
            pl.BlockSpec((1, TQ_IN, POOL_WIDTH), lambda b, j: (b, j, 0)),
            pl.BlockSpec((1, TQ_IN, ATTN_WIDTH), lambda b, j: (b, j, 0)),
            pl.BlockSpec((1, TQ_IN, KV_EXP), lambda b, j: (b, j, 0)),
            pl.BlockSpec((1, KV_WIDTH, TQ_IN), lambda b, j: (b, 0, j)),
        ),
        out_shape=(jax.ShapeDtypeStruct((B, S, POOL_WIDTH), BF16),
                   jax.ShapeDtypeStruct((B, S, ATTN_WIDTH), BF16),
                   jax.ShapeDtypeStruct((B, S, KV_EXP), BF16),
                   jax.ShapeDtypeStruct((B, KV_WIDTH, S), BF16)),
        scratch_shapes=[pltpu.VMEM((N_META, POOL_WIDTH), F32),
                        pltpu.VMEM((TQ_IN, IN_WIDTH), F32)],
        compiler_params=pltpu.CompilerParams(
            dimension_semantics=("arbitrary", "arbitrary"), vmem_limit_bytes=VMEM_LIMIT),
        name="in_proj",
    )(x, u_meta, w_in_b, w_pool_b, pscale, qgain2, kgain2, rope[N_META:])

    upper = jnp.asarray(np.triu(np.ones((ROUTE_ROWS, ROUTE_ROWS), np.float32), 1), dtype=BF16)
    per_row = S // TQ_ATT
    n_tiles = B * per_row

    def cur(s):
        t = jnp.minimum(s, n_tiles - 1)
        return t // per_row, t % per_row

    def lag(s):
        t = jnp.maximum(s - 1, 0)
        return t // per_row, t % per_row

    def lag_rows(s):
        return (*lag(s), 0)

    def lag_flat(s):
        return (jnp.maximum(s - 1, 0), 0, 0)

    w_steps = N_EXPERTS * W_SLICES
    assert n_tiles >= w_steps and D % W_SLICES == 0 and D_EXPERT % W_SLICES == 0

    def w_slice(s):
        k = jnp.minimum(s, w_steps - 1)
        return k // W_SLICES, k % W_SLICES, 0

    h2, m, table, rows, cnt, wg, wu, wd = pl.pallas_call(
        functools.partial(_attn_kernel, n_tiles, per_row),
        grid=(n_tiles + 1,),
        in_specs=[
            pl.BlockSpec(memory_space=pltpu.SMEM),
            pl.BlockSpec((1, TQ_ATT, D), lag_rows),
            pl.BlockSpec((1, TQ_ATT, POOL_WIDTH), lag_rows),
            pl.BlockSpec((1, TQ_ATT, ATTN_WIDTH), lambda s: (*cur(s), 0)),
            pl.BlockSpec((1, S, KV_EXP), lambda s: (cur(s)[0], 0, 0)),
            pl.BlockSpec((1, KV_WIDTH, S), lambda s: (cur(s)[0], 0, 0)),
            _const_spec((N_META, KV_EXP)),
            _const_spec((KV_WIDTH, N_META)),
            _const_spec((D, D)),
            _const_spec((D, LANES)),
            _const_spec((ROUTE_ROWS, ROUTE_ROWS)),
            pl.BlockSpec((D // W_SLICES, 1), lambda s: (w_slice(s)[1], 0)),
            pl.BlockSpec((1, D // W_SLICES, D_EXPERT), w_slice),
            pl.BlockSpec((1, D // W_SLICES, D_EXPERT), w_slice),
            pl.BlockSpec((1, D_EXPERT // W_SLICES, D), w_slice),
        ],
        out_specs=(
            pl.BlockSpec((1, TQ_ATT, D), lag_rows),
            pl.BlockSpec((1, TQ_ATT, D), lag_rows),
            pl.BlockSpec((1, TQ_ATT, LANES), lag_rows),
            pl.BlockSpec((TQ_ATT // ROUTE_ROWS, SUBLANES, ROUTE_ROWS), lag_flat),
            pl.BlockSpec((TQ_ATT // ROUTE_ROWS, SUBLANES, LANES), lag_flat),
            pl.BlockSpec((1, D // W_SLICES, D_EXPERT), w_slice),
            pl.BlockSpec((1, D // W_SLICES, D_EXPERT), w_slice),
            pl.BlockSpec((1, D_EXPERT // W_SLICES, D), w_slice),
        ),
        out_shape=(jax.ShapeDtypeStruct((B, S, D), F32),
                   jax.ShapeDtypeStruct((B, S, D), BF16),
                   jax.ShapeDtypeStruct((B, S, LANES), BF16),
                   jax.ShapeDtypeStruct((T // ROUTE_ROWS, SUBLANES, ROUTE_ROWS), F32),
                   jax.ShapeDtypeStruct((T // ROUTE_ROWS, SUBLANES, LANES), jnp.int32),
                   jax.ShapeDtypeStruct((N_EXPERTS, D, D_EXPERT), BF16),
                   jax.ShapeDtypeStruct((N_EXPERTS, D, D_EXPERT), BF16),
                   jax.ShapeDtypeStruct((N_EXPERTS, D_EXPERT, D), BF16)),
        scratch_shapes=[
            pltpu.VMEM((2, TQ_ATT, ATTN_WIDTH), BF16),
            pltpu.VMEM((TQ_ATT // WINDOW, N_KV_HEADS, KEY_ROWS, LANES), BF16),
            pltpu.VMEM((TQ_ATT // WINDOW, N_KV_HEADS, LANES, KEY_PAD), BF16),
            pltpu.VMEM((N_ITEMS, KEY_ROWS, 2 * LANES), F32),
            pltpu.VMEM((N_ITEMS, KEY_PAD, 2 * LANES), BF16),
            pltpu.VMEM((N_ITEMS, LANES, 2 * LANES), F32),
            pltpu.VMEM((TQ_ATT // ROUTE_ROWS, ROUTE_ROWS, LANES), F32),
        ],
        compiler_params=pltpu.CompilerParams(
            dimension_semantics=("arbitrary",), vmem_limit_bytes=VMEM_LIMIT),
        name="attn_out",
    )(attn_sinks[0] * LOG2_E, x, yp, q, kd, vt, kd_meta, vt_meta, w_out_b, w_r, upper,
      fgain, w_gate[0], w_up[0], w_down[0])

    cnt_flat = cnt[:, :N_GROUPS, 0].reshape(-1)
    n_moe = T // TM_MOE
    out = pl.pallas_call(
        functools.partial(_moe_kernel, n_moe),
        grid_spec=pltpu.PrefetchScalarGridSpec(
            num_scalar_prefetch=1,
            grid=(n_moe,),
            in_specs=[
                pl.BlockSpec((TM_MOE, D), lambda i, c: (i, 0)),
                pl.BlockSpec((TM_MOE, LANES), lambda i, c: (i, 0)),
                pl.BlockSpec((TM_MOE // ROUTE_ROWS, SUBLANES, ROUTE_ROWS), lambda i, c: (i, 0, 0)),
                pl.BlockSpec((TM_MOE // ROUTE_ROWS, SUBLANES, ROUTE_ROWS),
                             lambda i, c: (jnp.minimum(i + 1, n_moe - 1), 0, 0)),
                pl.BlockSpec((TM_MOE, D), lambda i, c: (i, 0)),
                _resident_spec((N_EXPERTS, D, D_EXPERT)),
                _resident_spec((N_EXPERTS, D, D_EXPERT)),
                _resident_spec((N_EXPERTS * D_EXPERT, D)),
            ],
            out_specs=pl.BlockSpec((TM_MOE, D), lambda i, c: (i, 0)),
            scratch_shapes=[pltpu.VMEM((YS_ROWS, D), BF16),
                            pltpu.VMEM((2, max(SUB, *PASS_SIZES), D_GROUP), BF16),
                            pltpu.SMEM((2, 3 * N_GROUPS + 1), jnp.int32),
                            pltpu.VMEM((2, SUBLANES, TM_MOE), F32),
                            pltpu.VMEM((2, TM_MOE, LANES), F32)],
        ),
        out_shape=jax.ShapeDtypeStruct((T, D), F32),
        compiler_params=pltpu.CompilerParams(
            dimension_semantics=("arbitrary",), vmem_limit_bytes=VMEM_LIMIT),
        name="moe",
    )(cnt_flat, m.reshape(T, D), table.reshape(T, LANES), rows, rows, h2.reshape(T, D), wg, wu,
      wd.reshape(N_EXPERTS * D_EXPERT, D))
    return out.reshape(B, S, D)
```

```python
import functools

import numpy as np
import jax
import jax.numpy as jnp
from jax import lax
from jax.experimental import pallas as pl
from jax.experimental.pallas import tpu as pltpu

D_MODEL = 1024
N_META = 16
POOL_WIDTH = 512
POOL_WINDOWS = (2, 4, 8, 16)
POOL_GROUP = 128
HEAD_DIM = 64
N_HEADS = 8
N_KV_HEADS = 2
ATTN_WIDTH = N_HEADS * HEAD_DIM
KV_WIDTH = N_KV_HEADS * HEAD_DIM
WINDOW = 128
ROT_DIM = HEAD_DIM // 4
ROPE_THETA = 500000.0
IN_WIDTH = POOL_WIDTH + ATTN_WIDTH + 2 * KV_WIDTH
N_GROUPS = 4
EXPERTS_PER_GROUP = 4
N_EXPERTS = 16
D_EXPERT = 256
EPS = 1e-6
NEG_INF = -1e30
LOG2_E = 1.4426950408889634

LANES = 128
SUBLANES = 8
KV_EXP = 4 * LANES
TQ_IN = 2048
RB_IN = 256
FIN_BLOCKS = 4
FIN_AHEAD = 4
TQ_ATT = 1024
KEY_ROWS = 2 * (2 * WINDOW + N_META)
KEY_PAD = -(-KEY_ROWS // LANES) * LANES
OUT_BLOCKS = 2
ROUTE_ROWS = OUT_BLOCKS * WINDOW
W_SLICES = 2
STAGE_LAGS = (0, 1, 2)
ATT_GROUP = 4
N_ITEMS = (TQ_ATT // WINDOW) * N_KV_HEADS
TM_MOE = 512
SUB = 144
PASS_SIZES = (128, 144, 160)
BF16_ROWS = 16
MXU_DEPTH = 256
GID_LANE = N_EXPERTS
RANK_LANE = 24
LO_SHIFT = 32
D_GROUP = EXPERTS_PER_GROUP * D_EXPERT
VMEM_LIMIT = 56 * 1024 * 1024

BF16 = jnp.bfloat16
F32 = jnp.float32


def _rope_tables(n_pos):
    half = ROT_DIM // 2
    inv_freq = 1.0 / (ROPE_THETA ** (np.arange(half, dtype=np.float64) / half))
    ang = np.arange(n_pos, dtype=np.float64)[:, None] * inv_freq[None, :]
    cos, sin = np.cos(ang), np.sin(ang)
    c = np.ones((n_pos, HEAD_DIM)); c[:, :half] = cos; c[:, half:ROT_DIM] = cos
    sa = np.zeros((n_pos, HEAD_DIM)); sa[:, :half] = -sin
    sb = np.zeros((n_pos, HEAD_DIM)); sb[:, half:ROT_DIM] = sin
    tab = np.concatenate([np.tile(c, (1, 2)), np.tile(sa, (1, 2)), np.tile(sb, (1, 2))], axis=1)
    return jnp.asarray(tab, dtype=F32)


def _rms_unit(x):
    ms = jnp.mean(x * x, axis=-1, keepdims=True)
    return x * lax.rsqrt(ms + EPS)


def _head_norm_rope_many(xs, gains, ropes):
    los = [lax.broadcasted_iota(jnp.int32, x.shape, 1) < HEAD_DIM for x in xs]
    sums = []
    for x, lo in zip(xs, los):
        sq = x * x
        sums.append((jnp.sum(jnp.where(lo, sq, 0.0), axis=-1, keepdims=True),
                     jnp.sum(jnp.where(lo, 0.0, sq), axis=-1, keepdims=True)))
    ys = [x * lax.rsqrt(jnp.where(lo, s_lo, s_hi) + HEAD_DIM * EPS) * g
          for x, g, lo, (s_lo, s_hi) in zip(xs, gains, los, sums)]
    half = ROT_DIM // 2
    rolled = [(pltpu.roll(y, LANES - half, 1), pltpu.roll(y, half, 1)) for y in ys]
    return [y * rope[:, 0:LANES] + ra * rope[:, LANES:2 * LANES] + rb * rope[:, 2 * LANES:3 * LANES]
            for y, (ra, rb), rope in zip(ys, rolled, ropes)]


def _head_norm_rope(xc, gain2, rope):
    return _head_norm_rope_many([xc], [gain2], [rope])[0]


def _expand_kv(t):
    lo = lax.broadcasted_iota(jnp.int32, t.shape, 1) < HEAD_DIM
    sw = pltpu.roll(t, HEAD_DIM, 1)
    z = jnp.zeros_like(t)
    return jnp.concatenate([jnp.where(lo, t, z), jnp.where(lo, z, sw),
                            jnp.where(lo, sw, z), jnp.where(lo, z, t)], axis=1)


def _project(x, w_in):
    return jnp.dot(_rms_unit(x).astype(BF16), w_in, preferred_element_type=F32)


def _meta_kernel(meta_ref, win_ref, kgain_ref, rope_ref, u_ref, kd_ref, v_ref):
    proj = _project(meta_ref[...], win_ref[...])
    u_ref[...] = proj[:, :POOL_WIDTH]
    k = proj[:, POOL_WIDTH + ATTN_WIDTH:POOL_WIDTH + ATTN_WIDTH + KV_WIDTH]
    v = proj[:, POOL_WIDTH + ATTN_WIDTH + KV_WIDTH:]
    k = _head_norm_rope(k, kgain_ref[...], rope_ref[...])
    kd_ref[...] = _expand_kv(k).astype(BF16)
    v_ref[...] = v.astype(BF16)


def _in_kernel(x_ref, umeta_ref, win_ref, wpool_ref, pscale_ref, qgain_ref, kgain_ref,
               rope_ref, yp_ref, q_ref, kd_ref, vt_ref, carry_ref, proj_ref):
    j = pl.program_id(1)

    @pl.when(j == 0)
    def _():
        carry_ref[...] = umeta_ref[...]

    sizes = [RB_IN] * (TQ_IN // RB_IN)
    blocks = [slice(sum(sizes[:i]), sum(sizes[:i + 1])) for i in range(len(sizes))]
    n_sub = len(blocks)

    def normalise(i):
        return _rms_unit(x_ref[0, blocks[i], :]).astype(BF16)

    def project(i, a):
        proj_ref[blocks[i], :] = jnp.dot(a, win_ref[...], preferred_element_type=F32)

    def pool(i):
        rows = blocks[i]
        u = proj_ref[rows, :POOL_WIDTH]
        prev = carry_ref[...] if i == 0 else proj_ref[rows.start - N_META:rows.start, :POOL_WIDTH]
        acc = jnp.concatenate([prev, u], axis=0)
        for gi, w in enumerate(POOL_WINDOWS):
            lo = gi * POOL_GROUP
            acc = acc[:, POOL_GROUP * (1 if gi else 0):]
            acc = acc + pltpu.roll(acc, w // 2, 0)
            mixed = acc[N_META:, :POOL_GROUP] * (1.0 / w) - u[:, lo:lo + POOL_GROUP]
            y = jnp.dot(mixed.astype(BF16), wpool_ref[gi], preferred_element_type=F32)
            yp_ref[0, rows, lo:lo + POOL_GROUP] = (y * pscale_ref[:, lo:lo + POOL_GROUP]).astype(BF16)

    n_qc = ATTN_WIDTH // LANES

    def finish(ids):
        for i in ids:
            pool(i)
        xs, gains, ropes = [], [], []
        for i in ids:
            rows = blocks[i]
            xs += [proj_ref[rows, POOL_WIDTH + c * LANES:POOL_WIDTH + (c + 1) * LANES] for c in range(n_qc + 1)]
            gains += [qgain_ref[...]] * n_qc + [kgain_ref[...]]
            ropes += [rope_ref[rows, :]] * (n_qc + 1)
        outs = _head_norm_rope_many(xs, gains, ropes)
        for k, i in enumerate(ids):
            rows = blocks[i]
            mine = outs[k * (n_qc + 1):(k + 1) * (n_qc + 1)]
            for c in range(n_qc):
                q_ref[0, rows, c * LANES:(c + 1) * LANES] = mine[c].astype(BF16)
            kd_ref[0, rows, :] = _expand_kv(mine[n_qc]).astype(BF16)
            v = proj_ref[rows, POOL_WIDTH + ATTN_WIDTH + KV_WIDTH:]
            vt_ref[0, :, rows] = jnp.transpose(v).astype(BF16)

    a_next = normalise(0)
    pending = []
    for i in range(n_sub):
        project(i, a_next)
        if i + 1 < n_sub:
            a_next = normalise(i + 1)
        pending.append(i)
        if len(pending) >= FIN_BLOCKS + FIN_AHEAD:
            finish(pending[:FIN_BLOCKS])
            pending = pending[FIN_BLOCKS:]
    while pending:
        finish(pending[:FIN_BLOCKS])
        pending = pending[FIN_BLOCKS:]
    carry_ref[...] = proj_ref[TQ_IN - N_META:, :POOL_WIDTH]


def _route(logits, upper):
    n = logits.shape[0]
    lt = jnp.transpose(logits)
    sub = SUBLANES
    row8 = lax.broadcasted_iota(jnp.int32, (sub, n), 0).astype(F32)
    row16 = lax.broadcasted_iota(jnp.int32, (N_EXPERTS, n), 0).astype(F32)
    g_ok = row8 < N_GROUPS
    gl = jnp.where(g_ok, lt[N_EXPERTS:N_EXPERTS + sub, :], NEG_INF)
    gmax = jnp.max(gl, axis=0, keepdims=True)
    gsum = jnp.sum(jnp.where(g_ok, jnp.exp(gl - gmax), 0.0), axis=0, keepdims=True)
    g_prob = 1.0 / gsum
    g_idx = jnp.min(jnp.where(gl == gmax, row8, float(sub)), axis=0, keepdims=True)
    e_lo = g_idx * EXPERTS_PER_GROUP
    emask = (row16 >= e_lo) & (row16 < e_lo + EXPERTS_PER_GROUP)
    el = jnp.where(emask, lt[:N_EXPERTS, :], NEG_INF)
    big = float(N_EXPERTS)
    e1 = jnp.max(el, axis=0, keepdims=True)
    i1 = jnp.min(jnp.where(el == e1, row16, big), axis=0, keepdims=True)
    el2 = jnp.where(row16 == i1, NEG_INF, el)
    e2 = jnp.max(el2, axis=0, keepdims=True)
    i2 = jnp.min(jnp.where(el2 == e2, row16, big), axis=0, keepdims=True)
    t = jnp.exp(e2 - e1)
    w1 = 1.0 / (1.0 + t)
    w2 = t * w1
    gates_t = jnp.where(row16 == i1, w1 * g_prob, 0.0) + jnp.where(row16 == i2, w2 * g_prob, 0.0)
    onehot = jnp.where(row8 == g_idx, 1.0, 0.0)
    earlier = jnp.dot(onehot.astype(BF16), upper, preferred_element_type=F32)
    rank = jnp.sum(onehot * earlier, axis=0, keepdims=True)
    hi = gates_t.astype(BF16).astype(F32)
    lo = (gates_t - hi).astype(BF16).astype(F32)
    gid8 = jnp.broadcast_to(g_idx, (sub, n))
    rank8 = jnp.broadcast_to(rank, (sub, n))
    assert GID_LANE == N_EXPERTS and RANK_LANE == GID_LANE + sub and LO_SHIFT == RANK_LANE + sub
    full_t = jnp.concatenate(
        [hi, gid8, rank8, lo, jnp.zeros((LANES - LO_SHIFT - N_EXPERTS, n), F32)], axis=0)
    rows = jnp.concatenate([g_idx, rank, jnp.zeros((sub - 2, n), F32)], axis=0)
    counts = jnp.sum(onehot, axis=1, keepdims=True)
    return (jnp.transpose(full_t).astype(BF16), rows,
            jnp.broadcast_to(counts, (sub, LANES)).astype(jnp.int32))


def _fold_rows(x, pair, final):
    rows = x.shape[0]
    while rows > SUBLANES:
        rows //= 2
        x = pair(x[:rows], x[rows:])
    return final(x, axis=0, keepdims=True)


def _attn_kernel(n_tiles, tiles_per_row, sink_ref, x_ref, yp_ref, q_ref, kd_ref, vt_ref, kmeta_ref, vtmeta_ref, wout_ref,
                 wr_ref, upper_ref, fg_ref, wg32_ref, wu32_ref, wd32_ref,
                 h2_ref, m_ref, table_ref, rows_ref, cnt_ref, wg_ref, wu_ref, wd_ref,
                 yattn_ref, kcat_ref, vbd_ref, s_ref, p_ref, inv_ref, logit_ref):
    step = pl.program_id(0)
    tile = jnp.minimum(step, n_tiles - 1)
    j = lax.rem(tile, tiles_per_row)
    slot = lax.rem(step, 2)
    nt = (((1,), (1,)), ((), ()))
    n_r = TQ_ATT // WINDOW
    band = 2 * WINDOW
    m0 = 2 * band

    @pl.when(step == 0)
    def _():
        vbd_ref[...] = jnp.zeros(vbd_ref.shape, BF16)
        p_ref[...] = jnp.zeros(p_ref.shape, BF16)
        yattn_ref[...] = jnp.zeros(yattn_ref.shape, BF16)

    wg_ref[0] = (wg32_ref[0] * fg_ref[...]).astype(BF16)
    wu_ref[0] = (wu32_ref[0] * fg_ref[...]).astype(BF16)
    wd_ref[0] = wd32_ref[0].astype(BF16)

    def window_start(r):
        return pl.multiple_of(jnp.maximum(j * TQ_ATT + (r - 1) * WINDOW, 0), WINDOW)

    def stage_scores(r):
        start = window_start(r)
        for g in range(N_KV_HEADS):
            vb = vt_ref[0, g * HEAD_DIM:(g + 1) * HEAD_DIM, pl.ds(start, band)]
            vm = vtmeta_ref[g * HEAD_DIM:(g + 1) * HEAD_DIM, :]
            for hh in range(2):
                col = (2 * g + hh) * LANES
                kcat_ref[r, g, hh * band:(hh + 1) * band, :] = kd_ref[0, pl.ds(start, band), col:col + LANES]
                kcat_ref[r, g, m0 + hh * N_META:m0 + (hh + 1) * N_META, :] = kmeta_ref[:, col:col + LANES]
                rows = slice(hh * HEAD_DIM, (hh + 1) * HEAD_DIM)
                vbd_ref[r, g, rows, hh * band:(hh + 1) * band] = vb
                vbd_ref[r, g, rows, m0 + hh * N_META:m0 + (hh + 1) * N_META] = vm
            rq = slice(r * WINDOW, (r + 1) * WINDOW)
            qq = jnp.concatenate([q_ref[0, rq, (2 * g + ch) * LANES:(2 * g + ch + 1) * LANES]
                                  for ch in range(2)], axis=0)
            s_ref[r * N_KV_HEADS + g] = lax.dot_general(kcat_ref[r, g], qq, nt, preferred_element_type=F32)

    def stage_softmax(r):
        kpos = window_start(r) + lax.broadcasted_iota(jnp.int32, (band, WINDOW), 0)
        qpos = j * TQ_ATT + r * WINDOW + lax.broadcasted_iota(jnp.int32, (band, WINDOW), 1)
        d = qpos - kpos
        bias = jnp.where((d >= 0) & (d < WINDOW), 0.0, NEG_INF)
        bias = jnp.concatenate([bias, bias], axis=1)
        left = lax.broadcasted_iota(jnp.int32, (1, 2 * WINDOW), 1) < WINDOW
        for g in range(N_KV_HEADS):
            it = r * N_KV_HEADS + g
            for hh in range(2):
                sink = jnp.where(left, sink_ref[4 * g + hh], sink_ref[4 * g + 2 + hh])
                s = s_ref[it, hh * band:(hh + 1) * band, :] + bias
                sm = s_ref[it, m0 + hh * N_META:m0 + (hh + 1) * N_META, :]
                mx = jnp.maximum(_fold_rows(s, jnp.maximum, jnp.max), jnp.max(sm, axis=0, keepdims=True))
                mx = jnp.maximum(mx, sink)
                p = jnp.exp2(s - mx)
                pm = jnp.exp2(sm - mx)
                den = (_fold_rows(p, jnp.add, jnp.sum) + jnp.sum(pm, axis=0, keepdims=True)
                       + jnp.exp2(sink - mx))
                p_ref[it, hh * band:(hh + 1) * band, :] = p.astype(BF16)
                p_ref[it, m0 + hh * N_META:m0 + (hh + 1) * N_META, :] = pm.astype(BF16)
                inv_ref[it, hh * HEAD_DIM:(hh + 1) * HEAD_DIM, :] = jnp.broadcast_to(
                    1.0 / den, (HEAD_DIM, 2 * WINDOW))

    def stage_values(r):
        for g in range(N_KV_HEADS):
            it = r * N_KV_HEADS + g
            o_t = jnp.dot(vbd_ref[r, g], p_ref[it], preferred_element_type=F32) * inv_ref[it]
            for ch in range(2):
                c = 2 * g + ch
                yattn_ref[slot, r * WINDOW:(r + 1) * WINDOW, c * LANES:(c + 1) * LANES] = (
                    jnp.transpose(o_t[:, ch * WINDOW:(ch + 1) * WINDOW]).astype(BF16))

    def stage_out():
        mix = (jnp.dot(yp_ref[0], wout_ref[:POOL_WIDTH, :], preferred_element_type=F32)
               + jnp.dot(yattn_ref[1 - slot], wout_ref[POOL_WIDTH:, :], preferred_element_type=F32))
        h2 = x_ref[0] + mix
        h2_ref[0] = h2
        m = _rms_unit(h2).astype(BF16)
        m_ref[0] = m
        logits = jnp.dot(m, wr_ref[...], preferred_element_type=F32)
        for p in range(TQ_ATT // ROUTE_ROWS):
            logit_ref[p] = logits[p * ROUTE_ROWS:(p + 1) * ROUTE_ROWS, :]

    def stage_route(p):
        table, rows, counts = _route(logit_ref[p], upper_ref[...])
        table_ref[0, p * ROUTE_ROWS:(p + 1) * ROUTE_ROWS, :] = table
        rows_ref[p] = rows
        cnt_ref[p] = counts

    n_p = TQ_ATT // ROUTE_ROWS
    n_grp = n_r // ATT_GROUP
    for t in range(max(n_grp + STAGE_LAGS[-1], n_p + 1)):
        for k, stage in zip(STAGE_LAGS, (stage_scores, stage_softmax, stage_values)):
            if 0 <= t - k < n_grp:
                for r in range((t - k) * ATT_GROUP, (t - k + 1) * ATT_GROUP):
                    stage(r)
        if t == 0:
            stage_out()
        if 1 <= t <= n_p:
            stage_route(t - 1)


def _pair_plan(na, nb):
    top = PASS_SIZES[-1]
    if isinstance(na, int) and isinstance(nb, int):
        if 1 <= na <= top and 1 <= nb <= top:
            size = next(s for s in PASS_SIZES if max(na, nb) <= s)
            return 1, size, size, size
        return 0, SUB, -(-na // SUB) * SUB, -(-nb // SUB) * SUB
    fast = (na >= 1) & (na <= top) & (nb >= 1) & (nb <= top)
    size = jnp.int32(top)
    for s in reversed(PASS_SIZES[:-1]):
        size = jnp.where(jnp.maximum(na, nb) <= s, s, size)
    slow = [lax.div(n + (SUB - 1), SUB) * SUB for n in (na, nb)]
    return (fast.astype(jnp.int32), size,
            jnp.where(fast, size, slow[0]), jnp.where(fast, size, slow[1]))


def _max_sorted_rows(n_tokens):
    pair = [max(sum(_pair_plan(na, t - na)[2:]) for na in range(t + 1)) for t in range(n_tokens + 1)]
    return max(pair[t] + pair[n_tokens - t] for t in range(n_tokens + 1))


def _moe_kernel(n_tiles, cnt_ref, m_ref, table_ref, rows_ref, rows_next_ref, h2_ref, wg_ref, wu_ref, wd_ref,
                out_ref, ys_ref, hid_ref, plan_ref, posr_ref, posc_ref):
    i = pl.program_id(0)
    slot = lax.rem(i, 2)
    per = TM_MOE // ROUTE_ROWS

    def prepare(tile, dst, rows_blk):
        counts = [[cnt_ref[(tile * per + h) * N_GROUPS + g] for g in range(N_GROUPS)] for h in range(per)]
        n_tok = [sum(counts[h][g] for h in range(per)) for g in range(N_GROUPS)]
        base = []
        total_rows = 0
        for p in range(N_GROUPS // 2):
            fast, size, rows_a, rows_b = _pair_plan(n_tok[2 * p], n_tok[2 * p + 1])
            plan_ref[dst, 2 * N_GROUPS + p] = fast
            plan_ref[dst, 2 * N_GROUPS + N_GROUPS // 2 + p] = size
            for g, rows in ((2 * p, rows_a), (2 * p + 1, rows_b)):
                base.append(total_rows)
                plan_ref[dst, g] = n_tok[g]
                plan_ref[dst, N_GROUPS + g] = total_rows + jnp.int32(0)
                total_rows = total_rows + rows
        plan_ref[dst, 3 * N_GROUPS] = total_rows
        first = [[base[g] + sum(counts[hh][g] for hh in range(h)) for g in range(N_GROUPS)]
                 for h in range(per)]

        def pick(gid, offsets):
            out = _as_f32(offsets[N_GROUPS - 1])
            for g in range(N_GROUPS - 2, -1, -1):
                out = jnp.where(gid == g, _as_f32(offsets[g]), out)
            return out

        pos = jnp.concatenate(
            [rows_blk[h, 1:2, :] + pick(rows_blk[h, 0:1, :], first[h]) for h in range(per)], axis=1)
        posr_ref[dst] = jnp.broadcast_to(pos, (SUBLANES, TM_MOE))
        posc_ref[dst] = jnp.transpose(jnp.broadcast_to(pos, (LANES, TM_MOE)))

    @pl.when(i == 0)
    def _():
        ys_ref[...] = jnp.zeros(ys_ref.shape, BF16)
        prepare(0, 0, rows_ref)

    table = table_ref[...]
    pos_row = posr_ref[slot, 0:1, :]
    pos_col = posc_ref[slot, :, 0:1]
    total_rows = plan_ref[slot, 3 * N_GROUPS]

    def gather(g, r0, size):
        rows = (r0 + lax.broadcasted_iota(jnp.int32, (size, 1), 0)).astype(F32)
        perm = jnp.where(pos_row == rows, 1.0, 0.0).astype(BF16)
        xs = jnp.dot(perm, m_ref[...], preferred_element_type=F32).astype(BF16)
        gs = jnp.dot(perm, table, preferred_element_type=F32)
        return xs, gs

    def experts_up(g, xs, gs, size, slot):
        lane = lax.broadcasted_iota(jnp.int32, gs.shape, 1)
        for jj in range(EXPERTS_PER_GROUP):
            e = g * EXPERTS_PER_GROUP + jj
            gt = jnp.dot(xs, wg_ref[e], preferred_element_type=F32)
            up = jnp.dot(xs, wu_ref[e], preferred_element_type=F32)
            gate = jnp.sum(jnp.where((lane == e) | (lane == e + LO_SHIFT), gs, 0.0),
                           axis=-1, keepdims=True)
            hid = gt * (1.0 / (1.0 + jnp.exp(-gt))) * up * gate
            hid_ref[slot, :size, jj * D_EXPERT:(jj + 1) * D_EXPERT] = hid.astype(BF16)

    def experts_down(g, r0, size, slot):
        w_down = wd_ref[pl.ds(pl.multiple_of(g * D_GROUP, D_GROUP), D_GROUP), :]
        y = jnp.dot(hid_ref[slot, :size, :], w_down, preferred_element_type=F32)
        ys_ref[pl.ds(r0, size), :] = y.astype(BF16)

    def run_passes(jobs, size):
        got = [gather(g, r0, size) for g, r0 in jobs]
        for slot, ((g, _), (xs, gs)) in enumerate(zip(jobs, got)):
            experts_up(g, xs, gs, size, slot)
        for slot, (g, r0) in enumerate(jobs):
            experts_down(g, r0, size, slot)

    def run_pair(p, carry):
        groups = (2 * p, 2 * p + 1)
        firsts = [pl.multiple_of(plan_ref[slot, N_GROUPS + g], BF16_ROWS) for g in groups]
        fast = plan_ref[slot, 2 * N_GROUPS + p]
        size_p = plan_ref[slot, 2 * N_GROUPS + N_GROUPS // 2 + p]
        for size in PASS_SIZES:
            @pl.when((fast == 1) & (size_p == size))
            def _(size=size):
                run_passes(list(zip(groups, firsts)), size)

        @pl.when(fast == 0)
        def _():
            for g, r0 in zip(groups, firsts):
                def body(c, inner, g=g, r0=r0):
                    run_passes([(g, pl.multiple_of(r0 + c * SUB, BF16_ROWS))], SUB)
                    return inner
                lax.fori_loop(0, lax.div(plan_ref[slot, g] + (SUB - 1), SUB), body, 0)
        return carry

    lax.fori_loop(0, N_GROUPS // 2, run_pair, 0)

    def scatter(n_rows):
        prepare(jnp.minimum(i + 1, n_tiles - 1), 1 - slot, rows_next_ref)
        cols = lax.broadcasted_iota(jnp.int32, (1, n_rows), 1).astype(F32)
        perm_t = jnp.where(pos_col == cols, 1.0, 0.0).astype(BF16)
        out_ref[...] = h2_ref[...] + jnp.dot(perm_t, ys_ref[:n_rows, :], preferred_element_type=F32)

    short = total_rows <= YS_SHORT

    @pl.when(short)
    def _():
        scatter(YS_SHORT)

    @pl.when(jnp.logical_not(short))
    def _():
        scatter(YS_ROWS)


YS_ROWS = -(-_max_sorted_rows(TM_MOE) // MXU_DEPTH) * MXU_DEPTH
YS_SHORT = YS_ROWS - MXU_DEPTH


def _as_f32(v):
    return float(v) if isinstance(v, int) else v.astype(F32)


def _const_spec(shape):
    n = len(shape)
    return pl.BlockSpec(shape, lambda *_: (0,) * n)


def _resident_spec(shape):
    n = len(shape)
    return pl.BlockSpec(shape, lambda *_: (0,) * n, pipeline_mode=pl.Buffered(1))


def kernel(x, meta_tokens, attn_norm_gain, w_in, w_pool, pool_scale, q_norm_gain, k_norm_gain,
           attn_sinks, w_out, ffn_norm_gain, w_group_router, w_expert_router, w_gate, w_up, w_down):
    B, S, D = x.shape
    assert D == D_MODEL and S % TQ_IN == 0 and S % TQ_ATT == 0 and (B * S) % TM_MOE == 0
    assert w_in.shape[0] == 1, "single layer"
    T = B * S

    rope = _rope_tables(N_META + S)
    fgain = ffn_norm_gain[0][:, None]
    qgain2 = jnp.tile(q_norm_gain[0] * LOG2_E, 2)[None, :]
    kgain2 = jnp.tile(k_norm_gain[0] * (HEAD_DIM ** 0.5), 2)[None, :]
    pscale = pool_scale[0][None, :]
    w_in_b = (attn_norm_gain[0][:, None] * w_in[0]).astype(BF16)
    w_pool_b = w_pool[0].astype(BF16)
    w_out_b = w_out[0].astype(BF16)
    w_r = (fgain * jnp.concatenate(
        [w_expert_router[0], w_group_router[0],
         jnp.zeros((D, LANES - N_EXPERTS - N_GROUPS), F32)], axis=1)).astype(BF16)
    params = pltpu.CompilerParams(vmem_limit_bytes=VMEM_LIMIT)

    u_meta, kd_meta, v_meta = pl.pallas_call(
        _meta_kernel,
        out_shape=(jax.ShapeDtypeStruct((N_META, POOL_WIDTH), F32),
                   jax.ShapeDtypeStruct((N_META, KV_EXP), BF16),
                   jax.ShapeDtypeStruct((N_META, KV_WIDTH), BF16)),
        compiler_params=params,
        name="meta_proj",
    )(meta_tokens, w_in_b, kgain2, rope[:N_META])
    vt_meta = v_meta.T

    yp, q, kd, vt = pl.pallas_call(
        _in_kernel,
        grid=(B, S // TQ_IN),
        in_specs=[
            pl.BlockSpec((1, TQ_IN, D), lambda b, j: (b, j, 0)),
            _const_spec((N_META, POOL_WIDTH)),
            _const_spec((D, IN_WIDTH)),
            _const_spec((len(POOL_WINDOWS), POOL_GROUP, POOL_GROUP)),
            _const_spec((1, POOL_WIDTH)),
            _const_spec((1, LANES)),
            _const_spec((1, LANES)),
            pl.BlockSpec((TQ_IN, 3 * LANES), lambda b, j: (j, 0)),
        ],
        out_specs=(
            pl.BlockSpec((1, TQ_IN, POOL_WIDTH), lambda b, j: (b, j, 0)),
            pl.BlockSpec((1, TQ_IN, ATTN_WIDTH), lambda b, j: (b, j, 0)),
            pl.BlockSpec((1, TQ_IN, KV_EXP), lambda b, j: (b, j, 0)),
            pl.BlockSpec((1, KV_WIDTH, TQ_IN), lambda b, j: (b, 0, j)),
        ),
        out_shape=(jax.ShapeDtypeStruct((B, S, POOL_WIDTH), BF16),
                   jax.ShapeDtypeStruct((B, S, ATTN_WIDTH), BF16),
                   jax.ShapeDtypeStruct((B, S, KV_EXP), BF16),
                   jax.ShapeDtypeStruct((B, KV_WIDTH, S), BF16)),
        scratch_shapes=[pltpu.VMEM((N_META, POOL_WIDTH), F32),
                        pltpu.VMEM((TQ_IN, IN_WIDTH), F32)],
        compiler_params=pltpu.CompilerParams(
            dimension_semantics=("arbitrary", "arbitrary"), vmem_limit_bytes=VMEM_LIMIT),
        name="in_proj",
    )(x, u_meta, w_in_b, w_pool_b, pscale, qgain2, kgain2, rope[N_META:])

    upper = jnp.asarray(np.triu(np.ones((ROUTE_ROWS, ROUTE_ROWS), np.float32), 1), dtype=BF16)
    per_row = S // TQ_ATT
    n_tiles = B * per_row

    def cur(s):
        t = jnp.minimum(s, n_tiles - 1)
        return t // per_row, t % per_row

    def lag(s):
        t = jnp.maximum(s - 1, 0)
        return t // per_row, t % per_row

    def lag_rows(s):
        return (*lag(s), 0)

    def lag_flat(s):
        return (jnp.maximum(s - 1, 0), 0, 0)

    w_steps = N_EXPERTS * W_SLICES
    assert n_tiles >= w_steps and D % W_SLICES == 0 and D_EXPERT % W_SLICES == 0

    def w_slice(s):
        k = jnp.minimum(s, w_steps - 1)
        return k // W_SLICES, k % W_SLICES, 0

    h2, m, table, rows, cnt, wg, wu, wd = pl.pallas_call(
        functools.partial(_attn_kernel, n_tiles, per_row),
        grid=(n_tiles + 1,),
        in_specs=[
            pl.BlockSpec(memory_space=pltpu.SMEM),
            pl.BlockSpec((1, TQ_ATT, D), lag_rows),
            pl.BlockSpec((1, TQ_ATT, POOL_WIDTH), lag_rows),
            pl.BlockSpec((1, TQ_ATT, ATTN_WIDTH), lambda s: (*cur(s), 0)),
            pl.BlockSpec((1, S, KV_EXP), lambda s: (cur(s)[0], 0, 0)),
            pl.BlockSpec((1, KV_WIDTH, S), lambda s: (cur(s)[0], 0, 0)),
            _const_spec((N_META, KV_EXP)),
            _const_spec((KV_WIDTH, N_META)),
            _const_spec((D, D)),
            _const_spec((D, LANES)),
            _const_spec((ROUTE_ROWS, ROUTE_ROWS)),
            pl.BlockSpec((D // W_SLICES, 1), lambda s: (w_slice(s)[1], 0)),
            pl.BlockSpec((1, D // W_SLICES, D_EXPERT), w_slice),
            pl.BlockSpec((1, D // W_SLICES, D_EXPERT), w_slice),
            pl.BlockSpec((1, D_EXPERT // W_SLICES, D), w_slice),
        ],
        out_specs=(
            pl.BlockSpec((1, TQ_ATT, D), lag_rows),
            pl.BlockSpec((1, TQ_ATT, D), lag_rows),
            pl.BlockSpec((1, TQ_ATT, LANES), lag_rows),
            pl.BlockSpec((TQ_ATT // ROUTE_ROWS, SUBLANES, ROUTE_ROWS), lag_flat),
            pl.BlockSpec((TQ_ATT // ROUTE_ROWS, SUBLANES, LANES), lag_flat),
            pl.BlockSpec((1, D // W_SLICES, D_EXPERT), w_slice),
            pl.BlockSpec((1, D // W_SLICES, D_EXPERT), w_slice),
            pl.BlockSpec((1, D_EXPERT // W_SLICES, D), w_slice),
        ),
        out_shape=(jax.ShapeDtypeStruct((B, S, D), F32),
                   jax.ShapeDtypeStruct((B, S, D), BF16),
                   jax.ShapeDtypeStruct((B, S, LANES), BF16),
                   jax.ShapeDtypeStruct((T // ROUTE_ROWS, SUBLANES, ROUTE_ROWS), F32),
                   jax.ShapeDtypeStruct((T // ROUTE_ROWS, SUBLANES, LANES), jnp.int32),
                   jax.ShapeDtypeStruct((N_EXPERTS, D, D_EXPERT), BF16),
                   jax.ShapeDtypeStruct((N_EXPERTS, D, D_EXPERT), BF16),
                   jax.ShapeDtypeStruct((N_EXPERTS, D_EXPERT, D), BF16)),
        scratch_shapes=[
            pltpu.VMEM((2, TQ_ATT, ATTN_WIDTH), BF16),
            pltpu.VMEM((TQ_ATT // WINDOW, N_KV_HEADS, KEY_ROWS, LANES), BF16),
            pltpu.VMEM((TQ_ATT // WINDOW, N_KV_HEADS, LANES, KEY_PAD), BF16),
            pltpu.VMEM((N_ITEMS, KEY_ROWS, 2 * LANES), F32),
            pltpu.VMEM((N_ITEMS, KEY_PAD, 2 * LANES), BF16),
            pltpu.VMEM((N_ITEMS, LANES, 2 * LANES), F32),
            pltpu.VMEM((TQ_ATT // ROUTE_ROWS, ROUTE_ROWS, LANES), F32),
        ],
        compiler_params=pltpu.CompilerParams(
            dimension_semantics=("arbitrary",), vmem_limit_bytes=VMEM_LIMIT),
        name="attn_out",
    )(attn_sinks[0] * LOG2_E, x, yp, q, kd, vt, kd_meta, vt_meta, w_out_b, w_r, upper,
      fgain, w_gate[0], w_up[0], w_down[0])

    cnt_flat = cnt[:, :N_GROUPS, 0].reshape(-1)
    n_moe = T // TM_MOE
    out = pl.pallas_call(
        functools.partial(_moe_kernel, n_moe),
        grid_spec=pltpu.PrefetchScalarGridSpec(
            num_scalar_prefetch=1,
            grid=(n_moe,),
            in_specs=[
                pl.BlockSpec((TM_MOE, D), lambda i, c: (i, 0)),
                pl.BlockSpec((TM_MOE, LANES), lambda i, c: (i, 0)),
                pl.BlockSpec((TM_MOE // ROUTE_ROWS, SUBLANES, ROUTE_ROWS), lambda i, c: (i, 0, 0)),
                pl.BlockSpec((TM_MOE // ROUTE_ROWS, SUBLANES, ROUTE_ROWS),
                             lambda i, c: (jnp.minimum(i + 1, n_moe - 1), 0, 0)),
                pl.BlockSpec((TM_MOE, D), lambda i, c: (i, 0)),
                _resident_spec((N_EXPERTS, D, D_EXPERT)),
                _resident_spec((N_EXPERTS, D, D_EXPERT)),
                _resident_spec((N_EXPERTS * D_EXPERT, D)),
            ],
            out_specs=pl.BlockSpec((TM_MOE, D), lambda i, c: (i, 0)),
            scratch_shapes=[pltpu.VMEM((YS_ROWS, D), BF16),
                            pltpu.VMEM((2, max(SUB, *PASS_SIZES), D_GROUP), BF16),
                            pltpu.SMEM((2, 3 * N_GROUPS + 1), jnp.int32),
                            pltpu.VMEM((2, SUBLANES, TM_MOE), F32),
                            pltpu.VMEM((2, TM_MOE, LANES), F32)],
        ),
        out_shape=jax.ShapeDtypeStruct((T, D), F32),
        compiler_params=pltpu.CompilerParams(
            dimension_semantics=("arbitrary",), vmem_limit_bytes=VMEM_LIMIT),
        name="moe",
    )(cnt_flat, m.reshape(T, D), table.reshape(T, LANES), rows, rows, h2.reshape(T, D), wg, wu,
      wd.reshape(N_EXPERTS * D_EXPERT, D))
    return out.reshape(B, S, D)
```
